```python
import math
import jax, jax.numpy as jnp
from jax import lax
import numpy as np

D_MODEL = 1024
BATCH = 4
SEQ = 8192
DEPTH = 1
DEC_BATCH = 32
DEC_SEQ = 8
PAST_LEN = 16384
PAGE_SIZE = 128

HEAD_DIM = 64
N_HEADS_A = 8
N_HEADS_B = 8
N_IDX_HEADS = 4
IDX_DIM = 64
INDEX_TOPK = 256
N_BUCKETS = 32
MAX_EXACT = N_BUCKETS // 2
MAX_DISTANCE = 128
D_FF = -(-8 * D_MODEL // (3 * 256)) * 256
Q_BLOCK = 128
EPS = 1e-6
POOL_NUM = 5
POOL_DEN = 4

SPLIT_SIZES = (N_HEADS_A * HEAD_DIM, N_HEADS_A * HEAD_DIM, N_HEADS_A * HEAD_DIM,
               N_IDX_HEADS * IDX_DIM, IDX_DIM, N_IDX_HEADS,
               N_HEADS_B * HEAD_DIM, N_HEADS_B * HEAD_DIM, N_HEADS_B * HEAD_DIM,
               D_MODEL, D_MODEL)
IN_WIDTH = int(sum(SPLIT_SIZES))
SPLIT_IDX = tuple(int(v) for v in np.cumsum(SPLIT_SIZES)[:-1])

kernel_name = "dsa_stickbreaking_gated_hybrid_step"


def _rmsnorm(x, g):
    xf = x.astype(jnp.float32)
    var = jnp.mean(xf * xf, axis=-1, keepdims=True)
    return (xf * lax.rsqrt(var + EPS) * g.astype(jnp.float32)).astype(x.dtype)


def _t5_bucket(rel):
    rel = jnp.maximum(rel, 0)
    small = rel < MAX_EXACT
    lf = (jnp.log(jnp.maximum(rel, 1).astype(jnp.float32) / MAX_EXACT)
          / math.log(MAX_DISTANCE / MAX_EXACT) * (N_BUCKETS - MAX_EXACT))
    large = jnp.minimum(MAX_EXACT + lf.astype(jnp.int32), N_BUCKETS - 1)
    return jnp.where(small, rel, large)


def _blocks(a):
    return a.reshape(a.shape[0], -1, Q_BLOCK, *a.shape[2:]).swapaxes(0, 1)


def _unblocks(a):
    a = a.swapaxes(0, 1)
    return a.reshape(a.shape[0], -1, *a.shape[3:])


def _dsa_block(q, qi, wi, q_pos, k_idx, k_pos, gather_kv, rel_bias, topk):
    dots = jnp.einsum('bthd,bsd->bths', qi.astype(jnp.float32), k_idx.astype(jnp.float32)) * IDX_DIM ** -0.5
    score = jnp.einsum('bth,bths->bts', wi.astype(jnp.float32) * N_IDX_HEADS ** -0.5, jax.nn.relu(dots))
    visible = k_pos[None, :] <= q_pos[:, None]
    score = jnp.where(visible[None], score, -jnp.inf)
    _, idx = lax.top_k(score, topk)
    sel_pos = k_pos[idx]
    valid = sel_pos <= q_pos[None, :, None]
    k_sel, v_sel = gather_kv(idx)
    logits = jnp.einsum('bthd,btkhd->bthk', q.astype(jnp.float32), k_sel.astype(jnp.float32)) * HEAD_DIM ** -0.5
    bias = rel_bias.astype(jnp.float32)[_t5_bucket(q_pos[None, :, None] - sel_pos)]
    logits = logits + bias.transpose(0, 1, 3, 2)
    logits = jnp.where(valid[:, :, None, :], logits, -jnp.inf)
    p = jax.nn.softmax(logits, axis=-1)
    return jnp.einsum('bthk,btkhd->bthd', p, v_sel.astype(jnp.float32))


def _sb_block(q, k, v, q_pos, k_pos):
    z = jnp.einsum('bthd,bshd->bhts', q.astype(jnp.float32), k.astype(jnp.float32)) * HEAD_DIM ** -0.5
    earlier = (k_pos[None, :] < q_pos[:, None])[None, None]
    log_1m = jnp.where(earlier, jax.nn.log_sigmoid(-z), 0.0)
    between = lax.cumsum(log_1m, axis=3, reverse=True) - log_1m
    a = jnp.where(earlier, jnp.exp(jax.nn.log_sigmoid(z) + between), 0.0)
    return jnp.einsum('bhts,bshd->bthd', a, v.astype(jnp.float32))


def _layer(x, c, attend, w_ada, b_ada, norm_pre_mix, norm_post_mix, norm_pre_ffn, norm_post_ffn,
           w_in, w_br_a, w_br_b, w_out, w_ffn_gate, w_ffn_up, w_ffn_down):
    B, T, _ = x.shape
    ada = (c @ w_ada + b_ada)[:, None, :]
    sh1, sc1, g1, sh2, sc2, g2 = jnp.split(ada, 6, axis=-1)
    h = _rmsnorm(x, norm_pre_mix) * (1 + sc1) + sh1
    qa, ka, va, qi, ki, wi, qb, kb, vb, gate_a, gate_b = jnp.split(h @ w_in, SPLIT_IDX, axis=-1)
    qa = qa.reshape(B, T, N_HEADS_A, HEAD_DIM)
    ka = ka.reshape(B, T, N_HEADS_A, HEAD_DIM)
    va = va.reshape(B, T, N_HEADS_A, HEAD_DIM)
    qi = qi.reshape(B, T, N_IDX_HEADS, IDX_DIM)
    qb = qb.reshape(B, T, N_HEADS_B, HEAD_DIM)
    kb = kb.reshape(B, T, N_HEADS_B, HEAD_DIM)
    vb = vb.reshape(B, T, N_HEADS_B, HEAD_DIM)
    ya, yb = attend(qa, ka, va, qi, ki, wi, qb, kb, vb)
    ya = ya.reshape(B, T, -1).astype(x.dtype)
    yb = yb.reshape(B, T, -1).astype(x.dtype)
    merged = jax.nn.sigmoid(gate_a) * (ya @ w_br_a) + jax.nn.sigmoid(gate_b) * (yb @ w_br_b)
    x = x + g1 * _rmsnorm(merged @ w_out, norm_post_mix)
    h2 = _rmsnorm(x, norm_pre_ffn) * (1 + sc2) + sh2
    f = (jax.nn.silu(h2 @ w_ffn_gate) * (h2 @ w_ffn_up)) @ w_ffn_down
    x = x + g2 * _rmsnorm(f, norm_post_ffn)
    return x, (ka, va, ki, kb, vb)


def setup_inputs(seed: int = 0) -> dict:
    key = jax.random.key(seed)
    ks = jax.random.split(key, 32)
    f32 = jnp.float32

    def nrm(k, shape, scale):
        return jax.random.normal(k, shape, f32) * scale

    n_pages = PAST_LEN // PAGE_SIZE
    n_phys = DEC_BATCH * n_pages * POOL_NUM // POOL_DEN
    kv_shape = (DEPTH, n_phys, PAGE_SIZE, N_HEADS_A, HEAD_DIM)
    kvb_shape = (DEPTH, n_phys, PAGE_SIZE, N_HEADS_B, HEAD_DIM)
    page_table = jax.random.permutation(ks[9], n_phys)[:DEC_BATCH * n_pages].reshape(DEC_BATCH, n_pages).astype(jnp.int32)
    dA = N_HEADS_A * HEAD_DIM
    dB = N_HEADS_B * HEAD_DIM
    return {
        "x_prompt": nrm(ks[0], (BATCH, SEQ, D_MODEL), 1.0),
        "x_sample": nrm(ks[1], (DEC_BATCH, DEC_SEQ, D_MODEL), 1.0),
        "c_prompt": nrm(ks[2], (BATCH, D_MODEL), 1.0),
        "c_sample": nrm(ks[3], (DEC_BATCH, D_MODEL), 1.0),
        "cache_k_a": nrm(ks[4], kv_shape, 1.0),
        "cache_v_a": nrm(ks[5], kv_shape, 1.0),
        "cache_k_idx": nrm(ks[6], (DEPTH, n_phys, PAGE_SIZE, IDX_DIM), 1.0),
        "cache_k_b": nrm(ks[7], kvb_shape, 1.0),
        "cache_v_b": nrm(ks[8], kvb_shape, 1.0),
        "page_table": page_table,
        "rel_bias": nrm(ks[10], (N_BUCKETS, N_HEADS_A), 0.5),
        "w_ada": nrm(ks[11], (DEPTH, D_MODEL, 6 * D_MODEL), 0.5 * D_MODEL ** -0.5),
        "b_ada": nrm(ks[12], (DEPTH, 6 * D_MODEL), 0.01),
        "norm_pre_mix": 1.0 + nrm(ks[13], (DEPTH, D_MODEL), 0.05),
        "norm_post_mix": 1.0 + nrm(ks[14], (DEPTH, D_MODEL), 0.05),
        "norm_pre_ffn": 1.0 + nrm(ks[15], (DEPTH, D_MODEL), 0.05),
        "norm_post_ffn": 1.0 + nrm(ks[16], (DEPTH, D_MODEL), 0.05),
        "w_in": nrm(ks[17], (DEPTH, D_MODEL, IN_WIDTH), D_MODEL ** -0.5),
        "w_br_a": nrm(ks[18], (DEPTH, dA, D_MODEL), dA ** -0.5),
        "w_br_b": nrm(ks[19], (DEPTH, dB, D_MODEL), dB ** -0.5),
        "w_out": nrm(ks[20], (DEPTH, D_MODEL, D_MODEL), D_MODEL ** -0.5),
        "w_ffn_gate": nrm(ks[21], (DEPTH, D_MODEL, D_FF), D_MODEL ** -0.5),
        "w_ffn_up": nrm(ks[22], (DEPTH, D_MODEL, D_FF), D_MODEL ** -0.5),
        "w_ffn_down": nrm(ks[23], (DEPTH, D_FF, D_MODEL), D_FF ** -0.5),
    }


def reference(x_prompt, x_sample, c_prompt, c_sample, cache_k_a, cache_v_a, cache_k_idx, cache_k_b, cache_v_b,
              page_table, rel_bias, w_ada, b_ada, norm_pre_mix, norm_post_mix, norm_pre_ffn, norm_post_ffn,
              w_in, w_br_a, w_br_b, w_out, w_ffn_gate, w_ffn_up, w_ffn_down):
    topk_prompt = min(INDEX_TOPK, SEQ // 4)
    topk_sample = min(INDEX_TOPK, (PAST_LEN + DEC_SEQ) // 4)
    pos_p = jnp.arange(SEQ, dtype=jnp.int32)
    k_pos_s = jnp.arange(PAST_LEN + DEC_SEQ, dtype=jnp.int32)
    q_pos_s = PAST_LEN + jnp.arange(DEC_SEQ, dtype=jnp.int32)
    b_p = jnp.arange(BATCH)[:, None, None]
    b_s = jnp.arange(DEC_BATCH)[:, None, None]

    x_p, x_s = x_prompt, x_sample
    states_p, states_s = [], []
    for l in range(DEPTH):
        wl = (w_ada[l], b_ada[l], norm_pre_mix[l], norm_post_mix[l], norm_pre_ffn[l], norm_post_ffn[l],
              w_in[l], w_br_a[l], w_br_b[l], w_out[l], w_ffn_gate[l], w_ffn_up[l], w_ffn_down[l])

        def attend_prompt(qa, ka, va, qi, ki, wi, qb, kb, vb):
            def gather(idx):
                return ka[b_p, idx], va[b_p, idx]
            ya = _unblocks(lax.map(
                lambda xs: _dsa_block(xs[0], xs[1], xs[2], xs[3], ki, pos_p, gather, rel_bias, topk_prompt),
                (_blocks(qa), _blocks(qi), _blocks(wi), pos_p.reshape(-1, Q_BLOCK))))
            yb = _unblocks(lax.map(
                lambda xs: _sb_block(xs[0], kb, vb, xs[1], pos_p),
                (_blocks(qb), pos_p.reshape(-1, Q_BLOCK))))
            return ya, yb

        ck_a, cv_a, ck_i, ck_b, cv_b = cache_k_a[l], cache_v_a[l], cache_k_idx[l], cache_k_b[l], cache_v_b[l]

        def attend_sample(qa, ka, va, qi, ki, wi, qb, kb, vb):
            def gather(idx):
                past = (idx < PAST_LEN)[..., None, None]
                pi = jnp.minimum(idx, PAST_LEN - 1)
                phys = page_table[b_s, pi // PAGE_SIZE]
                off = pi % PAGE_SIZE
                ni = jnp.clip(idx - PAST_LEN, 0, DEC_SEQ - 1)
                return (jnp.where(past, ck_a[phys, off], ka[b_s, ni]),
                        jnp.where(past, cv_a[phys, off], va[b_s, ni]))
            k_idx_all = jnp.concatenate(
                [ck_i[page_table].reshape(DEC_BATCH, PAST_LEN, IDX_DIM), ki.astype(ck_i.dtype)], axis=1)
            ya = _dsa_block(qa, qi, wi, q_pos_s, k_idx_all, k_pos_s, gather, rel_bias, topk_sample)
            kb_all = jnp.concatenate(
                [ck_b[page_table].reshape(DEC_BATCH, PAST_LEN, N_HEADS_B, HEAD_DIM), kb.astype(ck_b.dtype)], axis=1)
            vb_all = jnp.concatenate(
                [cv_b[page_table].reshape(DEC_BATCH, PAST_LEN, N_HEADS_B, HEAD_DIM), vb.astype(cv_b.dtype)], axis=1)
            yb = _sb_block(qb, kb_all, vb_all, q_pos_s, k_pos_s)
            return ya, yb

        x_p, st_p = _layer(x_p, c_prompt, attend_prompt, *wl)
        x_s, st_s = _layer(x_s, c_sample, attend_sample, *wl)
        states_p.append(st_p)
        states_s.append(st_s)

    k_a_prompt = jnp.stack([s[0] for s in states_p])
    v_a_prompt = jnp.stack([s[1] for s in states_p])
    k_idx_prompt = jnp.stack([s[2] for s in states_p])
    k_b_prompt = jnp.stack([s[3] for s in states_p])
    v_b_prompt = jnp.stack([s[4] for s in states_p])
    k_a_sample = jnp.stack([s[0] for s in states_s])
    v_a_sample = jnp.stack([s[1] for s in states_s])
    k_idx_sample = jnp.stack([s[2] for s in states_s])
    k_b_sample = jnp.stack([s[3] for s in states_s])
    v_b_sample = jnp.stack([s[4] for s in states_s])
    return (x_p, x_s, k_a_prompt, v_a_prompt, k_idx_prompt, k_b_prompt, v_b_prompt,
            k_a_sample, v_a_sample, k_idx_sample, k_b_sample, v_b_sample)
```

```python
import functools
import math

import numpy as np
import jax
import jax.numpy as jnp
from jax import lax
from jax.experimental import pallas as pl
from jax.experimental.pallas import tpu as pltpu

F32 = jnp.float32
BF16 = jnp.bfloat16
I32 = jnp.int32
SDS = jax.ShapeDtypeStruct

HEAD_DIM = 64
N_HEADS = 8
N_PAIRS = N_HEADS // 2
N_IDX_HEADS = 4
IDX_DIM = 64
INDEX_TOPK = 256
N_BUCKETS = 32
MAX_EXACT = N_BUCKETS // 2
MAX_DISTANCE = 128
EPS = 1e-6
LANES = 128
NEG = -1e30
VMEM_LIMIT = 56 * 1024 * 1024

_I32_MIN = np.int32(-2 ** 31)
_KEY_NEG_INF = np.int32(0x807FFFFF - 2 ** 32)


def _params(*sem):
    return pltpu.CompilerParams(dimension_semantics=sem, vmem_limit_bytes=VMEM_LIMIT)


def _dot(a, b):
    return jnp.dot(a, b, preferred_element_type=F32)


def _dot_nt(a, b):
    return lax.dot_general(a, b, (((1,), (1,)), ((), ())), preferred_element_type=F32)


def _split2(x):
    hi = x.astype(BF16)
    lo = (x - hi.astype(F32)).astype(BF16)
    return hi, lo


def _dot3(a_hi, a_lo, b_hi, b_lo):
    return _dot(a_hi, b_hi) + _dot(a_lo, b_hi) + _dot(a_hi, b_lo)


def _dot3_nt(a_hi, a_lo, b_hi, b_lo):
    return _dot_nt(a_hi, b_hi) + _dot_nt(a_lo, b_hi) + _dot_nt(a_hi, b_lo)


def _rms(x, g):
    var = jnp.mean(x * x, axis=-1, keepdims=True)
    return x * lax.rsqrt(var + EPS) * g


def _ada_kernel(c_ref, w_ref, b_ref, o_ref):
    c_hi, c_lo = _split2(c_ref[...])
    w_hi, w_lo = _split2(w_ref[...])
    o_ref[...] = _dot3(c_hi, c_lo, w_hi, w_lo) + b_ref[...]


def _ada(c, w, b):
    rows, d = c.shape
    n = w.shape[1]
    tn = 1024
    return pl.pallas_call(
        _ada_kernel,
        grid=(n // tn,),
        in_specs=[pl.BlockSpec((rows, d), lambda j: (0, 0)),
                  pl.BlockSpec((d, tn), lambda j: (0, j)),
                  pl.BlockSpec((1, tn), lambda j: (0, j))],
        out_specs=pl.BlockSpec((rows, tn), lambda j: (0, j)),
        out_shape=SDS((rows, n), F32),
        compiler_params=_params("arbitrary"),
        name="ada",
    )(c, w, b)


def _inproj_kernel(x_ref, sc_ref, sh_ref, g_ref, wm_ref, wg_ref, wih_ref, wil_ref,
                   qa_ref, ka_ref, va_ref, qb_ref, kb_ref, vb_ref,
                   kab_ref, vab_ref, kbb_ref, vbb_ref,
                   gate_ref, iq_ref, ik_ref, wi_ref, kidx_ref):
    dh = N_HEADS * HEAD_DIM
    h = _rms(x_ref[...], g_ref[...]) * (1.0 + sc_ref[0]) + sh_ref[0]
    h_hi, h_lo = _split2(h)

    def seg(n):
        return _dot(h_hi, wm_ref[:, n * dh:(n + 1) * dh])

    qa_ref[...] = (seg(0) * HEAD_DIM ** -0.5).astype(BF16)
    ka = seg(1)
    ka_ref[...] = ka
    kab_ref[...] = ka.astype(BF16)
    va = seg(2)
    va_ref[...] = va
    vab_ref[...] = va.astype(BF16)
    qb_ref[...] = (seg(3) * HEAD_DIM ** -0.5).astype(BF16)
    kb = seg(4)
    kb_ref[...] = kb
    kbb_ref[...] = kb.astype(BF16)
    vb = seg(5)
    vb_ref[...] = vb
    vbb_ref[...] = vb.astype(BF16)

    d = x_ref.shape[1]
    for n in range(2):
        g = _dot(h_hi, wg_ref[:, n * d:(n + 1) * d])
        gate_ref[:, n * d:(n + 1) * d] = (1.0 / (1.0 + jnp.exp(-g))).astype(BF16)

    nq = N_IDX_HEADS * IDX_DIM
    idx = _dot3(h_hi, h_lo, wih_ref[...], wil_ref[...])
    iq_ref[...] = idx[:, :nq] * IDX_DIM ** -0.5
    ik_ref[...] = idx[:, nq:nq + LANES]
    kidx_ref[...] = idx[:, nq:nq + IDX_DIM]
    wi_ref[...] = idx[:, nq + LANES:nq + 2 * LANES] * N_IDX_HEADS ** -0.5


def _inproj(x, sc, sh, g, wm, wg, wih, wil, tm, rows_per_mod):
    m, d = x.shape
    dh = N_HEADS * HEAD_DIM
    r = sc.shape[1]
    if r == 1:
        mod_map = lambda i: ((i * tm) // rows_per_mod, 0, 0)
    else:
        mod_map = lambda i: (0, i, 0)
    row = lambda w: pl.BlockSpec((tm, w), lambda i: (i, 0))
    full = lambda a: pl.BlockSpec(a.shape, lambda i: (0, 0))
    nq = N_IDX_HEADS * IDX_DIM
    outs = [(dh, BF16), (dh, F32), (dh, F32), (dh, BF16), (dh, F32), (dh, F32),
            (dh, BF16), (dh, BF16), (dh, BF16), (dh, BF16),
            (2 * d, BF16), (nq, F32), (LANES, F32), (LANES, F32), (IDX_DIM, F32)]
    return pl.pallas_call(
        _inproj_kernel,
        grid=(m // tm,),
        in_specs=[row(d), pl.BlockSpec((1, r, d), mod_map), pl.BlockSpec((1, r, d), mod_map),
                  full(g), full(wm), full(wg), full(wih), full(wil)],
        out_specs=[row(w) for w, _ in outs],
        out_shape=[SDS((m, w), dt) for w, dt in outs],
        compiler_params=_params("arbitrary"),
        name="inproj",
    )(x, sc, sh, g, wm, wg, wih, wil)


def _key_to_float(key):
    bits = jnp.where(key < 0, key ^ jnp.int32(0x7FFFFFFF), key)
    f = lax.bitcast_convert_type(bits, F32)
    return jnp.where(key < _KEY_NEG_INF, -jnp.inf, f)


def _count(score_ref, nblk, bw, cf, strict):
    rows = score_ref.shape[0]
    cfb = jnp.broadcast_to(cf, (rows, LANES))

    def body(j, acc):
        base = pl.multiple_of(j * bw, bw)
        for c in range(bw // LANES):
            s = score_ref[:, pl.ds(base + c * LANES, LANES)]
            hit = (s > cfb) if strict else (s >= cfb)
            acc = acc + jnp.where(hit, 1.0, 0.0)
        return acc

    acc = lax.fori_loop(0, nblk, body, jnp.zeros((rows, LANES), F32))
    return jnp.sum(acc, axis=1, keepdims=True)


def _kth_largest(score_ref, nblk, bw, k):
    rows = score_ref.shape[0]

    def body(it, u):
        cand = u | lax.shift_left(jnp.int32(1), 31 - it)
        cnt = _count(score_ref, nblk, bw, _key_to_float(cand ^ _I32_MIN), strict=False)
        return jnp.where(cnt >= k, cand, u)

    u = lax.fori_loop(0, 32, body, jnp.zeros((rows, 1), I32))
    return _key_to_float(u ^ _I32_MIN)


def _emit_selection(score_ref, nblk, bw, k, emit):
    rows = score_ref.shape[0]
    thr = _kth_largest(score_ref, nblk, bw, k)
    need = k - _count(score_ref, nblk, bw, thr, strict=True)
    finite = jnp.where(thr > -jnp.inf, 1.0, 0.0)
    before = (lax.broadcasted_iota(I32, (bw, bw), 0) < lax.broadcasted_iota(I32, (bw, bw), 1))
    before = jnp.where(before, 1.0, 0.0).astype(BF16)

    def body(j, carry):
        s = score_ref[:, pl.ds(pl.multiple_of(j * bw, bw), bw)]
        tie = jnp.where(s == thr, finite, 0.0)
        rank = _dot(tie.astype(BF16), before) + carry
        sel = jnp.where(s > thr, 1.0, jnp.where(rank < need, tie, 0.0))
        emit(j, sel)
        return carry + jnp.sum(tie, axis=1, keepdims=True)

    lax.fori_loop(0, nblk, body, jnp.zeros((rows, 1), F32))


def _half_masks(shape):
    lane = lax.broadcasted_iota(I32, shape, 1)
    return lane < HEAD_DIM


def _index_kernel(iq_ref, wi_ref, ik_ref, mask_ref, score_ref, *, tq, topk):
    i = pl.program_id(1)
    nblk_total = mask_ref.shape[2] // tq
    lo_half = _half_masks((tq, LANES))
    q_parts = []
    for h in range(N_IDX_HEADS):
        qp = iq_ref[0, :, (h // 2) * LANES:(h // 2 + 1) * LANES]
        qh = jnp.where(lo_half, qp, 0.0) if h % 2 == 0 else jnp.where(lo_half, 0.0, qp)
        q_parts.append(_split2(qh))
    wi = wi_ref[0]
    w_cols = [wi[:, h:h + 1] for h in range(N_IDX_HEADS)]
    row = lax.broadcasted_iota(I32, (tq, tq), 0)
    col = lax.broadcasted_iota(I32, (tq, tq), 1)

    def score_block(j, _):
        base = pl.multiple_of(j * tq, tq)
        k_hi, k_lo = _split2(ik_ref[0, pl.ds(base, tq), :])
        s = jnp.zeros((tq, tq), F32)
        for h in range(N_IDX_HEADS):
            d = _dot3_nt(q_parts[h][0], q_parts[h][1], k_hi, k_lo)
            s = s + w_cols[h] * jnp.maximum(d, 0.0)
        s = jnp.where(col + (j - i) * tq <= row, s, -jnp.inf)
        score_ref[:, pl.ds(base, tq)] = s
        return 0

    lax.fori_loop(0, i + 1, score_block, 0)

    def emit(j, sel):
        mask_ref[0, :, pl.ds(pl.multiple_of(j * tq, tq), tq)] = sel.astype(jnp.int8)

    _emit_selection(score_ref, i + 1, tq, topk, emit)

    def clear(j, _):
        mask_ref[0, :, pl.ds(pl.multiple_of(j * tq, tq), tq)] = jnp.zeros((tq, tq), jnp.int8)
        return 0

    lax.fori_loop(i + 1, nblk_total, clear, 0)


def _prompt_index(iq, wi, ik, tq, topk):
    b, t, _ = iq.shape
    return pl.pallas_call(
        functools.partial(_index_kernel, tq=tq, topk=float(topk)),
        grid=(b, t // tq),
        in_specs=[pl.BlockSpec((1, tq, iq.shape[2]), lambda bi, i: (bi, i, 0)),
                  pl.BlockSpec((1, tq, LANES), lambda bi, i: (bi, i, 0)),
                  pl.BlockSpec((1, t, LANES), lambda bi, i: (bi, 0, 0))],
        out_specs=pl.BlockSpec((1, tq, t), lambda bi, i: (bi, i, 0)),
        out_shape=SDS((b, t, t), jnp.int8),
        scratch_shapes=[pltpu.VMEM((tq, t), F32)],
        compiler_params=_params("arbitrary", "arbitrary"),
        name="prompt_index",
    )(iq, wi, ik)


def _t5_bucket(rel):
    rel = jnp.maximum(rel, 0)
    lf = (jnp.log(jnp.maximum(rel, 1).astype(F32) / MAX_EXACT)
          / math.log(MAX_DISTANCE / MAX_EXACT) * (N_BUCKETS - MAX_EXACT))
    large = jnp.minimum(MAX_EXACT + lf.astype(I32), N_BUCKETS - 1)
    return jnp.where(rel < MAX_EXACT, rel, large)


def _bias_lookup(rb_ref, bucket, h):
    val = jnp.zeros(bucket.shape, F32)
    for b in range(N_BUCKETS):
        val = jnp.where(bucket == b, rb_ref[b, h], val)
    return val


def _bias_tiles_kernel(rb_ref, o_ref, *, tq):
    h = pl.program_id(0)
    row = lax.broadcasted_iota(I32, (tq, tq), 0)
    col = lax.broadcasted_iota(I32, (tq, tq), 1)
    for tile in range(2):
        o_ref[tile, 0] = _bias_lookup(rb_ref, _t5_bucket(row - col + tile * tq), h)
    o_ref[2, 0] = jnp.full((tq, tq), rb_ref[N_BUCKETS - 1, h], F32)


def _bias_tiles(rel_bias, tq):
    assert tq >= MAX_DISTANCE
    return pl.pallas_call(
        functools.partial(_bias_tiles_kernel, tq=tq),
        grid=(N_HEADS,),
        in_specs=[pl.BlockSpec(memory_space=pltpu.SMEM)],
        out_specs=pl.BlockSpec((3, 1, tq, tq), lambda h: (0, h, 0, 0)),
        out_shape=SDS((3, N_HEADS, tq, tq), F32),
        compiler_params=_params("arbitrary"),
        name="bias_tiles",
    )(rel_bias)


def _pair_queries(q_ref, tq):
    lo_half = _half_masks((tq, LANES))
    out = []
    for p in range(N_PAIRS):
        qp = q_ref[0, :, p * LANES:(p + 1) * LANES].astype(F32)
        out.append((jnp.where(lo_half, qp, 0.0).astype(BF16), jnp.where(lo_half, 0.0, qp).astype(BF16)))
    return out, lo_half


def _dsa_kernel(q_ref, k_ref, v_ref, mask_ref, bias_ref, o_ref, m_ref, l_ref, acc_ref, *, tq):
    i = pl.program_id(1)
    qs, lo_half = _pair_queries(q_ref, tq)
    m_ref[...] = jnp.full(m_ref.shape, NEG, F32)
    l_ref[...] = jnp.zeros(l_ref.shape, F32)
    acc_ref[...] = jnp.zeros(acc_ref.shape, F32)

    def body(j, _):
        base = pl.multiple_of(j * tq, tq)
        valid = mask_ref[0, :, pl.ds(base, tq)].astype(I32) != 0
        tile = jnp.minimum(i - j, 2)
        for p in range(N_PAIRS):
            kp = k_ref[0, pl.ds(base, tq), p * LANES:(p + 1) * LANES]
            vp = v_ref[0, pl.ds(base, tq), p * LANES:(p + 1) * LANES]
            alphas, pvs = [], []
            for hh in range(2):
                h = 2 * p + hh
                logits = _dot_nt(qs[p][hh], kp) + bias_ref[tile, h]
                m_old = m_ref[h]
                m_new = jnp.maximum(m_old, jnp.max(jnp.where(valid, logits, NEG), axis=1, keepdims=True))
                alpha = jnp.exp(m_old - m_new)
                pexp = jnp.where(valid, jnp.exp(logits - m_new), 0.0)
                l_ref[h] = alpha * l_ref[h] + jnp.sum(pexp, axis=1, keepdims=True)
                m_ref[h] = m_new
                alphas.append(alpha)
                pvs.append(_dot(pexp.astype(BF16), vp))
            sl = slice(p * LANES, (p + 1) * LANES)
            acc_ref[:, sl] = (jnp.where(lo_half, alphas[0], alphas[1]) * acc_ref[:, sl]
                              + jnp.where(lo_half, pvs[0], pvs[1]))
        return 0

    lax.fori_loop(0, i + 1, body, 0)
    for p in range(N_PAIRS):
        sl = slice(p * LANES, (p + 1) * LANES)
        o_ref[0, :, sl] = acc_ref[:, sl] / jnp.where(lo_half, l_ref[2 * p], l_ref[2 * p + 1])


def _prompt_dsa(q, k, v, mask, bias, tq):
    b, t, dh = q.shape
    seq = pl.BlockSpec((1, t, dh), lambda bi, i: (bi, 0, 0))
    return pl.pallas_call(
        functools.partial(_dsa_kernel, tq=tq),
        grid=(b, t // tq),
        in_specs=[pl.BlockSpec((1, tq, dh), lambda bi, i: (bi, i, 0)), seq, seq,
                  pl.BlockSpec((1, tq, t), lambda bi, i: (bi, i, 0)),
                  pl.BlockSpec(bias.shape, lambda bi, i: (0, 0, 0, 0))],
        out_specs=pl.BlockSpec((1, tq, dh), lambda bi, i: (bi, i, 0)),
        out_shape=SDS((b, t, dh), F32),
        scratch_shapes=[pltpu.VMEM((N_HEADS, tq, 1), F32), pltpu.VMEM((N_HEADS, tq, 1), F32),
                        pltpu.VMEM((tq, dh), F32)],
        compiler_params=_params("arbitrary", "arbitrary"),
        name="prompt_dsa",
    )(q, k, v, mask, bias)


def _log_one_minus_beta(z):
    return -(jnp.maximum(z, 0.0) + jnp.log(1.0 + jnp.exp(-jnp.abs(z))))


def _sb_kernel(q_ref, k_ref, v_ref, o_ref, c_ref, acc_ref, *, tq):
    i = pl.program_id(1)
    qs, lo_half = _pair_queries(q_ref, tq)
    c_ref[...] = jnp.zeros(c_ref.shape, F32)
    acc_ref[...] = jnp.zeros(acc_ref.shape, F32)
    row = lax.broadcasted_iota(I32, (tq, tq), 0)
    col = lax.broadcasted_iota(I32, (tq, tq), 1)
    later = jnp.where(row > col, 1.0, 0.0).astype(BF16)
    earlier = col < row

    def block(j, diag):
        base = pl.multiple_of(j * tq, tq)
        for p in range(N_PAIRS):
            kp = k_ref[0, pl.ds(base, tq), p * LANES:(p + 1) * LANES]
            vp = v_ref[0, pl.ds(base, tq), p * LANES:(p + 1) * LANES]
            pvs = []
            for hh in range(2):
                h = 2 * p + hh
                z = _dot_nt(qs[p][hh], kp)
                lsm = _log_one_minus_beta(z)
                if diag:
                    lsm = jnp.where(earlier, lsm, 0.0)
                hi, lo = _split2(lsm)
                between = _dot(hi, later) + _dot(lo, later)
                a = jnp.exp(z + lsm + between + c_ref[h])
                if diag:
                    a = jnp.where(earlier, a, 0.0)
                c_ref[h] = c_ref[h] + between[:, :1] + lsm[:, :1]
                pvs.append(_dot(a.astype(BF16), vp))
            sl = slice(p * LANES, (p + 1) * LANES)
            acc_ref[:, sl] = acc_ref[:, sl] + jnp.where(lo_half, pvs[0], pvs[1])

    block(i, True)

    def body(step, _):
        block(i - 1 - step, False)
        return 0

    lax.fori_loop(0, i, body, 0)
    o_ref[0] = acc_ref[...]


def _prompt_sb(q, k, v, tq):
    b, t, dh = q.shape
    seq = pl.BlockSpec((1, t, dh), lambda bi, i: (bi, 0, 0))
    return pl.pallas_call(
        functools.partial(_sb_kernel, tq=tq),
        grid=(b, t // tq),
        in_specs=[pl.BlockSpec((1, tq, dh), lambda bi, i: (bi, i, 0)), seq, seq],
        out_specs=pl.BlockSpec((1, tq, dh), lambda bi, i: (bi, i, 0)),
        out_shape=SDS((b, t, dh), F32),
        scratch_shapes=[pltpu.VMEM((N_HEADS, tq, 1), F32), pltpu.VMEM((tq, dh), F32)],
        compiler_params=_params("arbitrary", "arbitrary"),
        name="prompt_sb",
    )(q, k, v)


def _merge_kernel(ya_ref, yb_ref, gate_ref, x_ref, g1_ref, n_ref, wa_ref, wb_ref, wo_ref, o_ref):
    d = x_ref.shape[1]
    ma = _dot(ya_ref[...].astype(BF16), wa_ref[...])
    mb = _dot(yb_ref[...].astype(BF16), wb_ref[...])
    merged = gate_ref[:, :d].astype(F32) * ma + gate_ref[:, d:].astype(F32) * mb
    mix = _dot(merged.astype(BF16), wo_ref[...])
    o_ref[...] = x_ref[...] + g1_ref[0] * _rms(mix, n_ref[...])


def _mod_spec(mod, tm, rows_per_mod):
    d = mod.shape[2]
    if mod.shape[1] == 1:
        return pl.BlockSpec((1, 1, d), lambda i: ((i * tm) // rows_per_mod, 0, 0))
    return pl.BlockSpec((1, tm, d), lambda i: (0, i, 0))


def _merge(ya, yb, gate, x, g1, norm, wa, wb, wo, tm, rows_per_mod):
    m, d = x.shape
    row = lambda a: pl.BlockSpec((tm, a.shape[1]), lambda i: (i, 0))
    full = lambda a: pl.BlockSpec(a.shape, lambda i: (0, 0))
    return pl.pallas_call(
        _merge_kernel,
        grid=(m // tm,),
        in_specs=[row(ya), row(yb), row(gate), row(x), _mod_spec(g1, tm, rows_per_mod),
                  full(norm), full(wa), full(wb), full(wo)],
        out_specs=pl.BlockSpec((tm, d), lambda i: (i, 0)),
        out_shape=SDS((m, d), F32),
        compiler_params=_params("arbitrary"),
        name="merge",
    )(ya, yb, gate, x, g1, norm, wa, wb, wo)


def _ffn_kernel(x_ref, sc_ref, sh_ref, g2_ref, npre_ref, npost_ref, wg_ref, wu_ref, wd_ref, o_ref, *, chunk):
    x = x_ref[...]
    h = (_rms(x, npre_ref[...]) * (1.0 + sc_ref[0]) + sh_ref[0]).astype(BF16)
    f = jnp.zeros(x.shape, F32)
    for c0 in range(0, wg_ref.shape[1], chunk):
        gate = _dot(h, wg_ref[:, c0:c0 + chunk])
        up = _dot(h, wu_ref[:, c0:c0 + chunk])
        act = gate / (1.0 + jnp.exp(-gate)) * up
        f = f + _dot(act.astype(BF16), wd_ref[c0:c0 + chunk, :])
    o_ref[...] = x + g2_ref[0] * _rms(f, npost_ref[...])


def _ffn(x, sc, sh, g2, npre, npost, wg, wu, wd, tm, rows_per_mod):
    m, d = x.shape
    full = lambda a: pl.BlockSpec(a.shape, lambda i: (0, 0))
    mod = lambda a: _mod_spec(a, tm, rows_per_mod)
    return pl.pallas_call(
        functools.partial(_ffn_kernel, chunk=256),
        grid=(m // tm,),
        in_specs=[pl.BlockSpec((tm, d), lambda i: (i, 0)), mod(sc), mod(sh), mod(g2),
                  full(npre), full(npost), full(wg), full(wu), full(wd)],
        out_specs=pl.BlockSpec((tm, d), lambda i: (i, 0)),
        out_shape=SDS((m, d), F32),
        compiler_params=_params("arbitrary"),
        name="ffn",
    )(x, sc, sh, g2, npre, npost, wg, wu, wd)


def _sample_score_kernel(pt_ref, q_ref, w_ref, knew_ref, *rest, pages):
    page_refs = rest[:pages]
    past_ref, new_ref = rest[pages:]
    q_hi, q_lo = _split2(q_ref[0])
    w = w_ref[0]
    t = q_ref.shape[1] // N_IDX_HEADS

    def score(keys):
        k_hi, k_lo = _split2(keys)
        r = jnp.maximum(_dot3_nt(q_hi, q_lo, k_hi, k_lo), 0.0) * w
        s = r[0:t]
        for h in range(1, N_IDX_HEADS):
            s = s + r[h * t:(h + 1) * t]
        return s

    for n in range(pages):
        past_ref[0, :, n * LANES:(n + 1) * LANES] = score(page_refs[n][...])
    new_ref[0] = score(knew_ref[0])


def _sample_scores(page_table, q_rows, w_rows, k_new, cache_k_idx, pages):
    nb, n_pages = page_table.shape
    page = cache_k_idx.shape[2]
    assert page == LANES
    rows = q_rows.shape[1]
    t = rows // N_IDX_HEADS

    def page_spec(n):
        return pl.BlockSpec((None, None, page, IDX_DIM), lambda b, g, pt: (0, pt[b, g * pages + n], 0, 0))

    grid_spec = pltpu.PrefetchScalarGridSpec(
        num_scalar_prefetch=1,
        grid=(nb, n_pages // pages),
        in_specs=[pl.BlockSpec((1, rows, IDX_DIM), lambda b, g, pt: (b, 0, 0)),
                  pl.BlockSpec((1, rows, LANES), lambda b, g, pt: (b, 0, 0)),
                  pl.BlockSpec((1, page, IDX_DIM), lambda b, g, pt: (b, 0, 0))]
                 + [page_spec(n) for n in range(pages)],
        out_specs=[pl.BlockSpec((1, t, pages * page), lambda b, g, pt: (b, 0, g)),
                   pl.BlockSpec((1, t, page), lambda b, g, pt: (b, 0, 0))],
    )
    return pl.pallas_call(
        functools.partial(_sample_score_kernel, pages=pages),
        grid_spec=grid_spec,
        out_shape=[SDS((nb, t, n_pages * page), F32), SDS((nb, t, page), F32)],
        compiler_params=_params("arbitrary", "arbitrary"),
        name="sample_scores",
    )(page_table, q_rows, w_rows, k_new, *([cache_k_idx] * pages))


def _sample_select_kernel(past_ref, new_ref, mask_ref, score_ref, *, t, topk):
    rows, past = past_ref.shape
    score_ref[:, :past] = past_ref[...]
    q_t = lax.broadcasted_iota(I32, (rows, LANES), 0) % t
    col = lax.broadcasted_iota(I32, (rows, LANES), 1)
    score_ref[:, past:] = jnp.where(col <= q_t, new_ref[...], -jnp.inf)

    def emit(j, sel):
        mask_ref[:, pl.ds(pl.multiple_of(j * LANES, LANES), LANES)] = sel.astype(jnp.int8)

    _emit_selection(score_ref, score_ref.shape[1] // LANES, LANES, topk, emit)


def _sample_select(s_past, s_new, t, topk, tr):
    rows, past = s_past.shape
    width = past + LANES
    return pl.pallas_call(
        functools.partial(_sample_select_kernel, t=t, topk=float(topk)),
        grid=(rows // tr,),
        in_specs=[pl.BlockSpec((tr, past), lambda i: (i, 0)), pl.BlockSpec((tr, LANES), lambda i: (i, 0))],
        out_specs=pl.BlockSpec((tr, width), lambda i: (i, 0)),
        out_shape=SDS((rows, width), jnp.int8),
        scratch_shapes=[pltpu.VMEM((tr, width), F32)],
        compiler_params=_params("arbitrary"),
        name="sample_select",
    )(s_past, s_new)


def _sample_bias_kernel(rb_ref, o_ref, *, t, page):
    key = lax.broadcasted_iota(I32, (page, N_HEADS * t), 0)
    col = lax.broadcasted_iota(I32, (page, N_HEADS * t), 1)
    q_t = col % t
    head = col // t
    for tile in range(2):
        bucket = _t5_bucket(q_t - key + tile * page)
        val = jnp.zeros(bucket.shape, F32)
        for h in range(N_HEADS):
            val = jnp.where(head == h, _bias_lookup(rb_ref, bucket, h), val)
        o_ref[tile] = val
    far = jnp.zeros(bucket.shape, F32)
    for h in range(N_HEADS):
        far = jnp.where(head == h, rb_ref[N_BUCKETS - 1, h], far)
    o_ref[2] = far


def _sample_bias(rel_bias, t, page):
    assert page >= MAX_DISTANCE
    return pl.pallas_call(
        functools.partial(_sample_bias_kernel, t=t, page=page),
        in_specs=[pl.BlockSpec(memory_space=pltpu.SMEM)],
        out_shape=SDS((3, page, N_HEADS * t), F32),
        name="sample_bias",
    )(rel_bias)


def _to_column(row_vec, n):
    eye = lax.broadcasted_iota(I32, (n, n), 0) == lax.broadcasted_iota(I32, (n, n), 1)
    return jnp.sum(jnp.where(eye, row_vec, 0.0), axis=1, keepdims=True)


def _sample_attn_kernel(pt_ref, qa_ref, qb_ref, knew_a_ref, vnew_a_ref, knew_b_ref, vnew_b_ref,
                        mnew_ref, mask_ref, bias_ref, *rest, pages, t):
    ka_refs = rest[0 * pages:1 * pages]
    va_refs = rest[1 * pages:2 * pages]
    kb_refs = rest[2 * pages:3 * pages]
    vb_refs = rest[3 * pages:4 * pages]
    ya_ref, yb_ref, m_ref, l_ref, acca_ref, c_ref, accb_ref = rest[4 * pages:]
    g = pl.program_id(1)
    n_groups = pl.num_programs(1)
    page = LANES
    cols = N_HEADS * t
    qa = qa_ref[0]
    qb = qb_ref[0]
    srow = lax.broadcasted_iota(I32, (page, page), 0)
    scol = lax.broadcasted_iota(I32, (page, page), 1)
    later = jnp.where(scol > srow, 1.0, 0.0).astype(BF16)

    def dsa_logits(k, bias):
        return _dot(k.astype(BF16), qa) + bias

    def dsa_update(blocks):
        m_old = m_ref[...]
        m_new = m_old
        for logits, valid, _ in blocks:
            m_new = jnp.maximum(m_new, jnp.max(jnp.where(valid, logits, NEG), axis=0, keepdims=True))
        alpha = jnp.exp(m_old - m_new)
        l_new = alpha * l_ref[...]
        contrib = jnp.zeros(acca_ref.shape, F32)
        for logits, valid, v in blocks:
            pexp = jnp.where(valid, jnp.exp(logits - m_new), 0.0)
            l_new = l_new + jnp.sum(pexp, axis=0, keepdims=True)
            contrib = contrib + _dot(pexp.T.astype(BF16), v.astype(BF16))
        m_ref[...] = m_new
        l_ref[...] = l_new
        acca_ref[...] = _to_column(alpha, cols) * acca_ref[...] + contrib

    def sb_block(k, v, earlier):
        z = _dot(k.astype(BF16), qb)
        lsm = _log_one_minus_beta(z)
        if earlier is not None:
            lsm = jnp.where(earlier, lsm, 0.0)
        hi, lo = _split2(lsm)
        between = _dot(later, hi) + _dot(later, lo)
        a = jnp.exp(z + lsm + between + c_ref[...])
        if earlier is not None:
            a = jnp.where(earlier, a, 0.0)
        c_ref[...] = c_ref[...] + between[:1, :] + lsm[:1, :]
        accb_ref[...] = accb_ref[...] + _dot(a.T.astype(BF16), v.astype(BF16))

    @pl.when(g == 0)
    def _():
        m_ref[...] = jnp.full(m_ref.shape, NEG, F32)
        l_ref[...] = jnp.zeros(l_ref.shape, F32)
        acca_ref[...] = jnp.zeros(acca_ref.shape, F32)
        c_ref[...] = jnp.zeros(c_ref.shape, F32)
        accb_ref[...] = jnp.zeros(accb_ref.shape, F32)
        key = lax.broadcasted_iota(I32, (page, cols), 0)
        q_t = lax.broadcasted_iota(I32, (page, cols), 1) % t
        valid = mnew_ref[0].astype(I32) != 0
        dsa_update([(dsa_logits(knew_a_ref[0], bias_ref[0]), valid, vnew_a_ref[0])])
        sb_block(knew_b_ref[0], vnew_b_ref[0], key < q_t)

    blocks = []
    for n in range(pages):
        valid = mask_ref[0, (pages - 1 - n) * page:(pages - n) * page, :].astype(I32) != 0
        last_page = jnp.logical_and(g == 0, n == 0)
        bias = jnp.where(last_page, bias_ref[1], bias_ref[2])
        blocks.append((dsa_logits(ka_refs[n][...], bias), valid, va_refs[n][...]))
    dsa_update(blocks)
    for n in range(pages):
        sb_block(kb_refs[n][...], vb_refs[n][...], None)

    @pl.when(g == n_groups - 1)
    def _():
        head_of_col = lax.broadcasted_iota(I32, (t, N_HEADS * HEAD_DIM), 1) // HEAD_DIM
        norm = acca_ref[...] / _to_column(l_ref[...], cols)
        ya = jnp.zeros((t, N_HEADS * HEAD_DIM), F32)
        yb = jnp.zeros((t, N_HEADS * HEAD_DIM), F32)
        for h in range(N_HEADS):
            ya = ya + jnp.where(head_of_col == h, norm[h * t:(h + 1) * t], 0.0)
            yb = yb + jnp.where(head_of_col == h, accb_ref[h * t:(h + 1) * t, :], 0.0)
        ya_ref[0] = ya
        yb_ref[0] = yb


def _sample_attn(page_table, qa_bd, qb_bd, knew_a, vnew_a, knew_b, vnew_b, mask_new, mask_past, bias,
                 cache_k_a, cache_v_a, cache_k_b, cache_v_b, pages, t):
    nb, n_pages = page_table.shape
    page = LANES
    dh = N_HEADS * HEAD_DIM
    cols = N_HEADS * t
    n_groups = n_pages // pages

    def page_spec(n):
        return pl.BlockSpec((None, None, page, dh),
                            lambda b, g, pt: (0, pt[b, n_pages - 1 - (g * pages + n)], 0, 0))

    per_seq = lambda shape: pl.BlockSpec((1,) + shape, lambda b, g, pt: (b, 0, 0))
    grid_spec = pltpu.PrefetchScalarGridSpec(
        num_scalar_prefetch=1,
        grid=(nb, n_groups),
        in_specs=[per_seq((dh, cols)), per_seq((dh, cols)),
                  per_seq((page, dh)), per_seq((page, dh)), per_seq((page, dh)), per_seq((page, dh)),
                  per_seq((page, cols)),
                  pl.BlockSpec((1, pages * page, cols), lambda b, g, pt: (b, n_groups - 1 - g, 0)),
                  pl.BlockSpec(bias.shape, lambda b, g, pt: (0, 0, 0))]
                 + [page_spec(n) for n in range(pages)] * 4,
        out_specs=[per_seq((t, dh)), per_seq((t, dh))],
        scratch_shapes=[pltpu.VMEM((1, cols), F32), pltpu.VMEM((1, cols), F32), pltpu.VMEM((cols, dh), F32),
                        pltpu.VMEM((1, cols), F32), pltpu.VMEM((cols, dh), F32)],
    )
    flat = lambda c: c.reshape(c.shape[0], c.shape[1], c.shape[2], dh)
    return pl.pallas_call(
        functools.partial(_sample_attn_kernel, pages=pages, t=t),
        grid_spec=grid_spec,
        out_shape=[SDS((nb, t, dh), F32), SDS((nb, t, dh), F32)],
        compiler_params=_params("arbitrary", "arbitrary"),
        name="sample_attn",
    )(page_table, qa_bd, qb_bd, knew_a, vnew_a, knew_b, vnew_b, mask_new, mask_past, bias,
      *([flat(cache_k_a)] * pages), *([flat(cache_v_a)] * pages),
      *([flat(cache_k_b)] * pages), *([flat(cache_v_b)] * pages))


def _largest_divisor(n, cap):
    d = min(n, cap)
    while n % d:
        d -= 1
    return d


def _block_diag_queries(q, nb, t):
    q4 = q.astype(F32).reshape(nb, t, N_HEADS, HEAD_DIM)
    eye = jnp.eye(N_HEADS, dtype=F32)
    bd = jnp.einsum('bthd,hg->bhdgt', q4, eye)
    return bd.reshape(nb, N_HEADS * HEAD_DIM, N_HEADS * t).astype(BF16)


def kernel(x_prompt, x_sample, c_prompt, c_sample, cache_k_a, cache_v_a, cache_k_idx, cache_k_b, cache_v_b,
           page_table, rel_bias, w_ada, b_ada, norm_pre_mix, norm_post_mix, norm_pre_ffn, norm_post_ffn,
           w_in, w_br_a, w_br_b, w_out, w_ffn_gate, w_ffn_up, w_ffn_down):
    depth = w_in.shape[0]
    assert depth == 1
    nb_p, t_p, d = x_prompt.shape
    nb_s, t_s, _ = x_sample.shape
    n_pages = page_table.shape[1]
    page = cache_k_a.shape[2]
    past = n_pages * page
    dh = N_HEADS * HEAD_DIM
    nq = N_IDX_HEADS * IDX_DIM
    topk_p = min(INDEX_TOPK, t_p // 4)
    topk_s = min(INDEX_TOPK, (past + t_s) // 4)
    tq = 256
    assert t_p % tq == 0 and topk_p <= tq and t_s * N_HEADS <= LANES and page == LANES

    w = w_in[0]
    o = np.cumsum([0, dh, dh, dh, nq, IDX_DIM, N_IDX_HEADS, dh, dh, dh, d, d])
    col = lambda n: w[:, o[n]:o[n + 1]]
    w_main = jnp.concatenate([col(0), col(1), col(2), col(6), col(7), col(8)], axis=1).astype(BF16)
    w_gate = jnp.concatenate([col(9), col(10)], axis=1).astype(BF16)
    w_idx = jnp.concatenate([col(3), col(4), col(4), col(5),
                             jnp.zeros((d, LANES - N_IDX_HEADS), F32)], axis=1)
    w_idx_hi = w_idx.astype(BF16)
    w_idx_lo = (w_idx - w_idx_hi.astype(F32)).astype(BF16)
    wa, wb, wo = w_br_a[0].astype(BF16), w_br_b[0].astype(BF16), w_out[0].astype(BF16)
    wg, wu, wd = w_ffn_gate[0].astype(BF16), w_ffn_up[0].astype(BF16), w_ffn_down[0].astype(BF16)
    n_pre_mix, n_post_mix = norm_pre_mix[0][None], norm_post_mix[0][None]
    n_pre_ffn, n_post_ffn = norm_pre_ffn[0][None], norm_post_ffn[0][None]

    n_c = nb_p + nb_s
    c_all = jnp.concatenate([c_prompt, c_sample, jnp.zeros((-n_c % 8, d), F32)], axis=0)
    ada = _ada(c_all, w_ada[0], b_ada[0][None])
    mods_p = [m[:, None, :] for m in jnp.split(ada[:nb_p], 6, axis=-1)]
    mods_s = [jnp.repeat(m, t_s, axis=0)[None] for m in jnp.split(ada[nb_p:n_c], 6, axis=-1)]

    def dense_front(x2, mods, tm, rows_per_mod):
        return _inproj(x2, mods[1], mods[0], n_pre_mix, w_main, w_gate, w_idx_hi, w_idx_lo, tm, rows_per_mod)

    def dense_back(ya, yb, gate, x2, mods, tm, rows_per_mod):
        x1 = _merge(ya, yb, gate, x2, mods[2], n_post_mix, wa, wb, wo, tm, rows_per_mod)
        return _ffn(x1, mods[4], mods[3], mods[5], n_pre_ffn, n_post_ffn, wg, wu, wd, tm, rows_per_mod)

    m_p = nb_p * t_p
    xp2 = x_prompt.reshape(m_p, d)
    tm_p = 256
    (qa, ka, va, qb, kb, vb, kab, vab, kbb, vbb, gate, iq, ik, wi, kidx) = dense_front(xp2, mods_p, tm_p, t_p)
    seq3 = lambda a: a.reshape(nb_p, t_p, a.shape[-1])
    mask = _prompt_index(seq3(iq), seq3(wi), seq3(ik), tq, topk_p)
    ya = _prompt_dsa(seq3(qa), seq3(kab), seq3(vab), mask, _bias_tiles(rel_bias, tq), tq)
    yb = _prompt_sb(seq3(qb), seq3(kbb), seq3(vbb), tq)
    y_p = dense_back(ya.reshape(m_p, dh), yb.reshape(m_p, dh), gate, xp2, mods_p, tm_p, t_p)
    heads_p = lambda a: a.reshape(depth, nb_p, t_p, N_HEADS, HEAD_DIM)
    outs_p = (heads_p(ka), heads_p(va), kidx.reshape(depth, nb_p, t_p, IDX_DIM), heads_p(kb), heads_p(vb))

    m_s = nb_s * t_s
    xs2 = x_sample.reshape(m_s, d)
    tm_s = _largest_divisor(m_s, 256)
    (qa, ka, va, qb, kb, vb, _, _, _, _, gate, iq, ik, wi, kidx) = dense_front(xs2, mods_s, tm_s, t_s)
    pad_keys = lambda a: jnp.pad(a.reshape(nb_s, t_s, a.shape[-1]), ((0, 0), (0, page - t_s), (0, 0)))
    q_rows = iq.reshape(nb_s, t_s, N_IDX_HEADS, IDX_DIM).transpose(0, 2, 1, 3).reshape(nb_s, N_IDX_HEADS * t_s, IDX_DIM)
    w_rows = wi.reshape(nb_s, t_s, LANES)[:, :, :N_IDX_HEADS].transpose(0, 2, 1).reshape(nb_s, N_IDX_HEADS * t_s, 1)
    w_rows = jnp.broadcast_to(w_rows, (nb_s, N_IDX_HEADS * t_s, LANES))
    pages_idx = _largest_divisor(n_pages, 16)
    s_past, s_new = _sample_scores(page_table, q_rows, w_rows, pad_keys(kidx), cache_k_idx, pages_idx)
    mask_s = _sample_select(s_past.reshape(m_s, past), s_new.reshape(m_s, page), t_s, topk_s,
                            _largest_divisor(m_s, 128))
    mask_t = jnp.tile(mask_s.reshape(nb_s, t_s, past + page).transpose(0, 2, 1), (1, 1, N_HEADS))
    pages_attn = _largest_divisor(n_pages, 4)
    ya, yb = _sample_attn(page_table, _block_diag_queries(qa, nb_s, t_s), _block_diag_queries(qb, nb_s, t_s),
                          pad_keys(ka), pad_keys(va), pad_keys(kb), pad_keys(vb),
                          mask_t[:, past:], mask_t[:, :past], _sample_bias(rel_bias, t_s, page),
                          cache_k_a, cache_v_a, cache_k_b, cache_v_b, pages_attn, t_s)
    y_s = dense_back(ya.reshape(m_s, dh), yb.reshape(m_s, dh), gate, xs2, mods_s, tm_s, t_s)
    heads_s = lambda a: a.reshape(depth, nb_s, t_s, N_HEADS, HEAD_DIM)
    outs_s = (heads_s(ka), heads_s(va), kidx.reshape(depth, nb_s, t_s, IDX_DIM), heads_s(kb), heads_s(vb))

    return (y_p.reshape(nb_p, t_p, d), y_s.reshape(nb_s, t_s, d)) + outs_p + outs_s
```

```python
import functools
import math

import numpy as np
import jax
import jax.numpy as jnp
from jax import lax
from jax.experimental import pallas as pl
from jax.experimental.pallas import tpu as pltpu

F32 = jnp.float32
BF16 = jnp.bfloat16
I32 = jnp.int32
SDS = jax.ShapeDtypeStruct

HEAD_DIM = 64
N_HEADS = 8
N_PAIRS = N_HEADS // 2
N_IDX_HEADS = 4
IDX_DIM = 64
INDEX_TOPK = 256
N_BUCKETS = 32
MAX_EXACT = N_BUCKETS // 2
MAX_DISTANCE = 128
EPS = 1e-6
LANES = 128
SUBLANES = 8
NEG = -1e30
LOG2E = 1.4426950408889634
VMEM_LIMIT = 56 * 1024 * 1024

_I32_MIN = np.int32(-2 ** 31)
_KEY_NEG_INF = np.int32(0x807FFFFF - 2 ** 32)


def _params(*sem):
    return pltpu.CompilerParams(dimension_semantics=sem, vmem_limit_bytes=VMEM_LIMIT)


def _dot(a, b):
    return jnp.dot(a, b, preferred_element_type=F32)


def _dot_nt(a, b):
    return lax.dot_general(a, b, (((1,), (1,)), ((), ())), preferred_element_type=F32)


def _split2(x):
    hi = x.astype(BF16)
    lo = (x - hi.astype(F32)).astype(BF16)
    return hi, lo


def _dot3(a_hi, a_lo, b_hi, b_lo):
    return _dot(a_hi, b_hi) + _dot(a_lo, b_hi) + _dot(a_hi, b_lo)


def _dot3_nt(a_hi, a_lo, b_hi, b_lo):
    return _dot_nt(a_hi, b_hi) + _dot_nt(a_lo, b_hi) + _dot_nt(a_hi, b_lo)


def _rms(x, g):
    var = jnp.mean(x * x, axis=-1, keepdims=True)
    return x * lax.rsqrt(var + EPS) * g


def _ada_kernel(c_ref, w_ref, b_ref, o_ref):
    c_hi, c_lo = _split2(c_ref[...])
    w_hi, w_lo = _split2(w_ref[...])
    o_ref[...] = _dot3(c_hi, c_lo, w_hi, w_lo) + b_ref[...]


def _ada(c, w, b):
    rows, d = c.shape
    n = w.shape[1]
    tn = 1024
    return pl.pallas_call(
        _ada_kernel,
        grid=(n // tn,),
        in_specs=[pl.BlockSpec((rows, d), lambda j: (0, 0)),
                  pl.BlockSpec((d, tn), lambda j: (0, j)),
                  pl.BlockSpec((1, tn), lambda j: (0, j))],
        out_specs=pl.BlockSpec((rows, tn), lambda j: (0, j)),
        out_shape=SDS((rows, n), F32),
        compiler_params=_params("arbitrary"),
        name="ada",
    )(c, w, b)


def _inproj_kernel(x_ref, sc_ref, sh_ref, g_ref, wm_ref, wg_ref, wih_ref, wil_ref,
                   qa_ref, ka_ref, va_ref, qb_ref, kb_ref, vb_ref,
                   kab_ref, vab_ref, kbb_ref, vbb_ref,
                   gate_ref, iq_ref, ik_ref, wi_ref, kidx_ref, wit_ref, *, q_scale):
    dh = N_HEADS * HEAD_DIM
    h = _rms(x_ref[...], g_ref[...]) * (1.0 + sc_ref[0]) + sh_ref[0]
    h_hi, h_lo = _split2(h)

    def seg(n):
        return _dot(h_hi, wm_ref[:, n * dh:(n + 1) * dh])

    qa_ref[...] = (seg(0) * q_scale).astype(BF16)
    ka = seg(1)
    ka_ref[...] = ka
    kab_ref[...] = ka.astype(BF16)
    va = seg(2)
    va_ref[...] = va
    vab_ref[...] = va.astype(BF16)
    qb_ref[...] = (seg(3) * q_scale).astype(BF16)
    kb = seg(4)
    kb_ref[...] = kb
    kbb_ref[...] = kb.astype(BF16)
    vb = seg(5)
    vb_ref[...] = vb
    vbb_ref[...] = vb.astype(BF16)

    d = x_ref.shape[1]
    for n in range(2):
        g = _dot(h_hi, wg_ref[:, n * d:(n + 1) * d])
        gate_ref[:, n * d:(n + 1) * d] = (1.0 / (1.0 + jnp.exp(-g))).astype(BF16)

    nq = N_IDX_HEADS * IDX_DIM
    idx = _dot3(h_hi, h_lo, wih_ref[...], wil_ref[...])
    iq_ref[...] = idx[:, :nq] * IDX_DIM ** -0.5
    ik_ref[...] = idx[:, nq:nq + LANES]
    kidx_ref[...] = idx[:, nq:nq + IDX_DIM]
    wi = idx[:, nq + LANES:nq + 2 * LANES] * N_IDX_HEADS ** -0.5
    wi_ref[...] = wi
    wit_ref[...] = wi.T[:SUBLANES, :]


def _inproj(x, sc, sh, g, wm, wg, wih, wil, tm, rows_per_mod, q_scale):
    m, d = x.shape
    dh = N_HEADS * HEAD_DIM
    r = sc.shape[1]
    if r == 1:
        mod_map = lambda i: ((i * tm) // rows_per_mod, 0, 0)
    else:
        mod_map = lambda i: (0, i, 0)
    row = lambda w: pl.BlockSpec((tm, w), lambda i: (i, 0))
    full = lambda a: pl.BlockSpec(a.shape, lambda i: (0, 0))
    nq = N_IDX_HEADS * IDX_DIM
    outs = [(dh, BF16), (dh, F32), (dh, F32), (dh, BF16), (dh, F32), (dh, F32),
            (dh, BF16), (dh, BF16), (dh, BF16), (dh, BF16),
            (2 * d, BF16), (nq, F32), (LANES, F32), (LANES, F32), (IDX_DIM, F32)]
    return pl.pallas_call(
        functools.partial(_inproj_kernel, q_scale=q_scale),
        grid=(m // tm,),
        in_specs=[row(d), pl.BlockSpec((1, r, d), mod_map), pl.BlockSpec((1, r, d), mod_map),
                  full(g), full(wm), full(wg), full(wih), full(wil)],
        out_specs=[row(w) for w, _ in outs] + [pl.BlockSpec((SUBLANES, tm), lambda i: (0, i))],
        out_shape=[SDS((m, w), dt) for w, dt in outs] + [SDS((SUBLANES, m), F32)],
        compiler_params=_params("arbitrary"),
        name="inproj",
    )(x, sc, sh, g, wm, wg, wih, wil)


ROW_CHUNK = 64
MAX_GROUPS = 256


def _key_to_float(key):
    bits = jnp.where(key < 0, key ^ jnp.int32(0x7FFFFFFF), key)
    f = lax.bitcast_convert_type(bits, F32)
    return jnp.where(key < _KEY_NEG_INF, -jnp.inf, f)


def _float_to_key(f):
    bits = lax.bitcast_convert_type(f, I32)
    return jnp.where(bits < 0, bits ^ jnp.int32(0x7FFFFFFF), bits)


def _group_width(width):
    gw = min(width, 8 * LANES)
    while width % gw:
        gw -= LANES
    return gw


def _count(score_ref, r0, ngroups, gw, cf, strict):
    cfb = jnp.broadcast_to(cf, (ROW_CHUNK, LANES))

    def body(g, acc):
        base = pl.multiple_of(g * gw, gw)
        for c in range(gw // LANES):
            s = score_ref[pl.ds(r0, ROW_CHUNK), pl.ds(base + c * LANES, LANES)]
            hit = (s > cfb) if strict else (s >= cfb)
            acc = acc + jnp.where(hit, 1.0, 0.0)
        return acc

    acc = lax.fori_loop(0, ngroups, body, jnp.zeros((ROW_CHUNK, LANES), F32))
    return jnp.sum(acc, axis=1, keepdims=True)


def _kth_largest(score_ref, r0, ngroups, gw, k):
    ninf = jnp.full((ROW_CHUNK, LANES), -jnp.inf, F32)

    def bounds(g, carry):
        base = pl.multiple_of(g * gw, gw)
        tops = list(carry)
        for c in range(gw // LANES):
            s = score_ref[pl.ds(r0, ROW_CHUNK), pl.ds(base + c * LANES, LANES)]
            tops[c % 2] = jnp.maximum(tops[c % 2], s)
        return tuple(tops)

    top_even, top_odd = lax.fori_loop(0, ngroups, bounds, (ninf, ninf))
    upper = jnp.max(jnp.maximum(top_even, top_odd), axis=1, keepdims=True)
    lower = jnp.min(jnp.minimum(top_even, top_odd), axis=1, keepdims=True)
    lo0 = _float_to_key(lower)
    hi0 = _float_to_key(upper) + 1

    def n_active(lo, hi):
        return jnp.sum(jnp.where(hi > lo + 1, 1.0, 0.0))

    def cond(state):
        return state[2] > 0.0

    def step(state):
        lo, hi, _ = state
        mid = lax.shift_right_arithmetic(lo, 1) + lax.shift_right_arithmetic(hi, 1) + (lo & hi & 1)
        cnt = _count(score_ref, r0, ngroups, gw, _key_to_float(mid), strict=False)
        keep = cnt >= k
        lo = jnp.where(keep, mid, lo)
        hi = jnp.where(cnt == k, mid + 1, jnp.where(keep, hi, mid))
        return lo, hi, n_active(lo, hi)

    lo, _, _ = lax.while_loop(cond, step, (lo0, hi0, n_active(lo0, hi0)))
    return _key_to_float(lo)


def _emit_selection(score_ref, thr_ref, need_ref, ngroups, gw, nblk, bw, k, emit):
    rows = score_ref.shape[0]
    assert k <= MAX_GROUPS and rows % ROW_CHUNK == 0

    def chunk(n, _):
        r0 = pl.multiple_of(n * ROW_CHUNK, ROW_CHUNK)
        thr = _kth_largest(score_ref, r0, ngroups, gw, k)
        thr_ref[pl.ds(r0, ROW_CHUNK), :] = thr
        need_ref[pl.ds(r0, ROW_CHUNK), :] = k - _count(score_ref, r0, ngroups, gw, thr, strict=True)
        return 0

    lax.fori_loop(0, rows // ROW_CHUNK, chunk, 0)
    thr = thr_ref[...]
    need = need_ref[...]
    finite = jnp.where(thr > -jnp.inf, 1.0, 0.0)
    before = (lax.broadcasted_iota(I32, (bw, bw), 0) < lax.broadcasted_iota(I32, (bw, bw), 1))
    before = jnp.where(before, 1.0, 0.0).astype(BF16)

    def body(j, carry):
        s = score_ref[:, pl.ds(pl.multiple_of(j * bw, bw), bw)]
        tie = jnp.where(s == thr, finite, 0.0)
        rank = _dot(tie.astype(BF16), before) + carry
        sel = jnp.where(s > thr, 1.0, jnp.where(rank < need, tie, 0.0))
        emit(j, sel)
        return carry + jnp.sum(tie, axis=1, keepdims=True)

    lax.fori_loop(0, nblk, body, jnp.zeros((rows, 1), F32))


def _half_masks(shape):
    lane = lax.broadcasted_iota(I32, shape, 1)
    return lane < HEAD_DIM


def _index_kernel(iq_ref, wit_ref, ik_ref, mask_ref, score_ref, top_ref, *, tq, topk):
    i = pl.program_id(1)
    nblk_total = mask_ref.shape[2] // tq
    assert topk <= tq
    lo_half = _half_masks((tq, LANES))
    q_parts = []
    for h in range(N_IDX_HEADS):
        qp = iq_ref[0, :, (h // 2) * LANES:(h // 2 + 1) * LANES]
        qh = jnp.where(lo_half, qp, 0.0) if h % 2 == 0 else jnp.where(lo_half, 0.0, qp)
        q_hi, q_lo = _split2(qh)
        q_parts.append(jnp.concatenate([q_hi, q_hi, q_lo], axis=1))
    w_rows = [wit_ref[h:h + 1, :] for h in range(N_IDX_HEADS)]
    key = lax.broadcasted_iota(I32, (tq, tq), 0)
    qry = lax.broadcasted_iota(I32, (tq, tq), 1)
    top_ref[...] = jnp.full((tq, tq), -jnp.inf, F32)

    def score_block(j, _):
        base = pl.multiple_of(j * tq, tq)
        k_hi, k_lo = _split2(ik_ref[0, pl.ds(base, tq), :])
        k_parts = jnp.concatenate([k_hi, k_lo, k_hi], axis=1)
        s = jnp.zeros((tq, tq), F32)
        for h in range(N_IDX_HEADS):
            d = _dot_nt(k_parts, q_parts[h])
            s = s + w_rows[h] * jnp.maximum(d, 0.0)
        s = jnp.where(key + (j - i) * tq <= qry, s, -jnp.inf)
        score_ref[pl.ds(base, tq), :] = s
        top_ref[...] = jnp.maximum(top_ref[...], s)
        return 0

    lax.fori_loop(0, i + 1, score_block, 0)

    def count(cf, strict):
        cfb = jnp.broadcast_to(cf, (SUBLANES, tq))

        def body(j, accs):
            blk = score_ref[pl.ds(pl.multiple_of(j * tq, tq), tq), :]
            accs = list(accs)
            for r in range(tq // SUBLANES):
                s = blk[r * SUBLANES:(r + 1) * SUBLANES]
                hit = (s > cfb) if strict else (s >= cfb)
                accs[r % len(accs)] = accs[r % len(accs)] + jnp.where(hit, 1.0, 0.0)
            return tuple(accs)

        zero = jnp.zeros((SUBLANES, tq), F32)
        accs = lax.fori_loop(0, i + 1, body, (zero, zero, zero, zero))
        return jnp.sum((accs[0] + accs[1]) + (accs[2] + accs[3]), axis=0, keepdims=True)

    tops = top_ref[...]
    lo0 = _float_to_key(jnp.min(tops, axis=0, keepdims=True))
    hi0 = _float_to_key(jnp.max(tops, axis=0, keepdims=True)) + 1

    def n_active(lo, hi):
        return jnp.sum(jnp.where(hi > lo + 1, 1.0, 0.0))

    def step(state):
        lo, hi, _ = state
        mid = lax.shift_right_arithmetic(lo, 1) + lax.shift_right_arithmetic(hi, 1) + (lo & hi & 1)
        cnt = count(_key_to_float(mid), strict=False)
        keep = cnt >= topk
        lo = jnp.where(keep, mid, lo)
        hi = jnp.where(cnt == topk, mid + 1, jnp.where(keep, hi, mid))
        return lo, hi, n_active(lo, hi)

    lo, _, _ = lax.while_loop(lambda state: state[2] > 0.0, step, (lo0, hi0, n_active(lo0, hi0)))
    thr = _key_to_float(lo)
    need = topk - count(thr, strict=True)
    finite = jnp.where(thr > -jnp.inf, 1.0, 0.0)
    before = jnp.where(qry < key, 1.0, 0.0).astype(BF16)
    eye = jnp.where(qry == key, 1.0, 0.0).astype(BF16)

    def emit(j, carry):
        base = pl.multiple_of(j * tq, tq)
        s = score_ref[pl.ds(base, tq), :]
        tie = jnp.where(s == thr, finite, 0.0)
        rank = _dot(before, tie.astype(BF16)) + carry
        sel = jnp.where(s > thr, 1.0, jnp.where(rank < need, tie, 0.0))
        mask_ref[0, :, pl.ds(base, tq)] = _dot_nt(eye, sel.astype(BF16)).astype(jnp.int8)
        return carry + jnp.sum(tie, axis=0, keepdims=True)

    carry = lax.fori_loop(0, (i + 1) // 2, lambda jj, c: emit(2 * jj + 1, emit(2 * jj, c)), jnp.zeros((1, tq), F32))

    @pl.when((i + 1) % 2 == 1)
    def _():
        emit(i, carry)

    def clear(j, _):
        mask_ref[0, :, pl.ds(pl.multiple_of(j * tq, tq), tq)] = jnp.zeros((tq, tq), jnp.int8)
        return 0

    lax.fori_loop(i + 1, nblk_total, clear, 0)


def _prompt_index(iq, wit, ik, tq, topk):
    b, t, _ = iq.shape
    nq = t // tq
    return pl.pallas_call(
        functools.partial(_index_kernel, tq=tq, topk=float(topk)),
        grid=(b, nq),
        in_specs=[pl.BlockSpec((1, tq, iq.shape[2]), lambda bi, i: (bi, i, 0)),
                  pl.BlockSpec((SUBLANES, tq), lambda bi, i: (0, bi * nq + i)),
                  pl.BlockSpec((1, t, LANES), lambda bi, i: (bi, 0, 0))],
        out_specs=pl.BlockSpec((1, tq, t), lambda bi, i: (bi, i, 0)),
        out_shape=SDS((b, t, t), jnp.int8),
        scratch_shapes=[pltpu.VMEM((t, tq), F32), pltpu.VMEM((tq, tq), F32)],
        compiler_params=_params("arbitrary", "arbitrary"),
        name="prompt_index",
    )(iq, wit, ik)


def _t5_bucket(rel):
    rel = jnp.maximum(rel, 0)
    lf = (jnp.log(jnp.maximum(rel, 1).astype(F32) / MAX_EXACT)
          / math.log(MAX_DISTANCE / MAX_EXACT) * (N_BUCKETS - MAX_EXACT))
    large = jnp.minimum(MAX_EXACT + lf.astype(I32), N_BUCKETS - 1)
    return jnp.where(rel < MAX_EXACT, rel, large)


def _bias_lookup(rb_ref, bucket, h):
    val = jnp.zeros(bucket.shape, F32)
    for b in range(N_BUCKETS):
        val = jnp.where(bucket == b, rb_ref[b, h], val)
    return val


def _bias_tiles_kernel(rb_ref, o_ref, *, tq):
    h = pl.program_id(0)
    row = lax.broadcasted_iota(I32, (tq, tq), 0)
    col = lax.broadcasted_iota(I32, (tq, tq), 1)
    far = rb_ref[N_BUCKETS - 1, h]
    for tile in range(2):
        o_ref[tile, 0] = (_bias_lookup(rb_ref, _t5_bucket(row - col + tile * tq), h) - far) * LOG2E
    o_ref[2, 0] = jnp.zeros((tq, tq), F32)


def _bias_tiles(rel_bias, tq):
    assert tq >= MAX_DISTANCE
    return pl.pallas_call(
        functools.partial(_bias_tiles_kernel, tq=tq),
        grid=(N_HEADS,),
        in_specs=[pl.BlockSpec(memory_space=pltpu.SMEM)],
        out_specs=pl.BlockSpec((3, 1, tq, tq), lambda h: (0, h, 0, 0)),
        out_shape=SDS((3, N_HEADS, tq, tq), F32),
        compiler_params=_params("arbitrary"),
        name="bias_tiles",
    )(rel_bias)


def _pair_queries(q_ref, tq):
    lo_half = _half_masks((tq, LANES))
    out = []
    for p in range(N_PAIRS):
        qp = q_ref[0, :, p * LANES:(p + 1) * LANES].astype(F32)
        out.append((jnp.where(lo_half, qp, 0.0).astype(BF16), jnp.where(lo_half, 0.0, qp).astype(BF16)))
    return out, lo_half


def _dsa_kernel(q_ref, k_ref, v_ref, mask_ref, bias_ref, o_ref, logit_ref, madd_ref, m_ref, acc_ref, *, tq):
    i = pl.program_id(1)
    tk = 2 * tq
    qs, lo_half = _pair_queries(q_ref, tq)
    m_ref[...] = jnp.full(m_ref.shape, NEG, F32)
    acc_ref[...] = jnp.zeros(acc_ref.shape, F32)
    keep_lo = jnp.where(_half_masks((tk, LANES)), 1.0, 0.0).astype(BF16)
    keep_hi = 1.0 - keep_lo

    def pair_step(jj, near):
        base = pl.multiple_of(jj * tk, tk)
        madd_ref[...] = jnp.where(mask_ref[0, :, pl.ds(base, tk)].astype(I32) != 0, 0.0, NEG)
        for p in range(N_PAIRS):
            for u in range(2):
                kp = k_ref[0, pl.ds(base + u * tq, tq), p * LANES:(p + 1) * LANES]
                for hh in range(2):
                    h = 2 * p + hh
                    lm = _dot_nt(qs[p][hh], kp) + madd_ref[:, u * tq:(u + 1) * tq]
                    if near:
                        lm = lm + bias_ref[jnp.clip(i - (2 * jj + u), 0, 2), h]
                    logit_ref[h, :, u * tq:(u + 1) * tq] = lm
        for p in range(N_PAIRS):
            vp = v_ref[0, pl.ds(base, tk), p * LANES:(p + 1) * LANES]
            for hh in range(2):
                h = 2 * p + hh
                m_old = m_ref[h]
                m_new = jnp.maximum(m_old, jnp.max(logit_ref[h], axis=1, keepdims=True))
                m_ref[h] = m_new
                pexp = jnp.concatenate(
                    [jnp.exp2(logit_ref[h, :, c * LANES:(c + 1) * LANES] - m_new) for c in range(tk // LANES)],
                    axis=1)
                vaug = vp * keep_lo + keep_hi if hh == 0 else vp * keep_hi + keep_lo
                acc_ref[h] = jnp.exp2(m_old - m_new) * acc_ref[h] + _dot(pexp.astype(BF16), vaug)

    n_pairs = (i + 2) // 2
    n_far = jnp.maximum(n_pairs - 2, 0)

    def far_body(jj, _):
        pair_step(jj, False)
        return 0

    def near_body(jj, _):
        pair_step(jj, True)
        return 0

    lax.fori_loop(0, n_far, far_body, 0)
    lax.fori_loop(n_far, n_pairs, near_body, 0)
    for p in range(N_PAIRS):
        outs = []
        for hh in range(2):
            acc = acc_ref[2 * p + hh]
            outs.append(acc / pltpu.roll(acc, HEAD_DIM, axis=1))
        o_ref[0, :, p * LANES:(p + 1) * LANES] = jnp.where(lo_half, outs[0], outs[1])


def _prompt_dsa(q, k, v, mask, bias, tq):
    b, t, dh = q.shape
    assert t % (2 * tq) == 0
    once = pl.Buffered(1)
    seq = pl.BlockSpec((1, t, dh), lambda bi, i: (bi, 0, 0), pipeline_mode=once)
    return pl.pallas_call(
        functools.partial(_dsa_kernel, tq=tq),
        grid=(b, t // tq),
        in_specs=[pl.BlockSpec((1, tq, dh), lambda bi, i: (bi, i, 0)), seq, seq,
                  pl.BlockSpec((1, tq, t), lambda bi, i: (bi, i, 0)),
                  pl.BlockSpec(bias.shape, lambda bi, i: (0, 0, 0, 0), pipeline_mode=once)],
        out_specs=pl.BlockSpec((1, tq, dh), lambda bi, i: (bi, i, 0)),
        out_shape=SDS((b, t, dh), F32),
        scratch_shapes=[pltpu.VMEM((N_HEADS, tq, 2 * tq), F32), pltpu.VMEM((tq, 2 * tq), F32),
                        pltpu.VMEM((N_HEADS, tq, LANES), F32), pltpu.VMEM((N_HEADS, tq, LANES), F32)],
        compiler_params=_params("arbitrary", "arbitrary"),
        name="prompt_dsa",
    )(q, k, v, mask, bias)


def _log_one_minus_beta(z):
    return -(jnp.maximum(z, 0.0) + jnp.log(1.0 + jnp.exp(-jnp.abs(z))))


def _sb_kernel(q_ref, k_ref, v_ref, o_ref, z_ref, tail_ref, c_ref, acc_ref, *, tq):
    i = pl.program_id(1)
    qs, lo_half = _pair_queries(q_ref, tq)
    c_ref[...] = jnp.zeros(c_ref.shape, F32)
    acc_ref[...] = jnp.zeros(acc_ref.shape, F32)
    row = lax.broadcasted_iota(I32, (tq, tq), 0)
    col = lax.broadcasted_iota(I32, (tq, tq), 1)
    not_before = jnp.where(row >= col, 1.0, 0.0).astype(BF16)
    not_before = jnp.concatenate([not_before, not_before], axis=0)
    earlier = col < row

    def block(j, diag):
        base = pl.multiple_of(j * tq, tq)
        for p in range(N_PAIRS):
            kp = k_ref[0, pl.ds(base, tq), p * LANES:(p + 1) * LANES]
            for hh in range(2):
                z_ref[2 * p + hh] = _dot_nt(qs[p][hh], kp)
        for h in range(N_HEADS):
            z = z_ref[h]
            sp = jnp.maximum(z, 0.0) + jnp.log2(1.0 + jnp.exp2(-jnp.abs(z)))
            if diag:
                sp = jnp.where(earlier, sp, 0.0)
            hi, lo = _split2(sp)
            tail_ref[h] = _dot(jnp.concatenate([hi, lo], axis=1), not_before)
        for p in range(N_PAIRS):
            vp = v_ref[0, pl.ds(base, tq), p * LANES:(p + 1) * LANES]
            pvs = []
            for hh in range(2):
                h = 2 * p + hh
                c_old = c_ref[h]
                a = jnp.concatenate(
                    [jnp.exp2(z_ref[h, :, n * LANES:(n + 1) * LANES] - tail_ref[h, :, n * LANES:(n + 1) * LANES]
                             - c_old) for n in range(tq // LANES)], axis=1)
                if diag:
                    a = jnp.where(earlier, a, 0.0)
                c_ref[h] = c_old + tail_ref[h, :, :1]
                pvs.append(_dot(a.astype(BF16), vp))
            sl = slice(p * LANES, (p + 1) * LANES)
            acc_ref[:, sl] = acc_ref[:, sl] + jnp.where(lo_half, pvs[0], pvs[1])

    block(i, True)

    def body(step, _):
        block(i - 1 - step, False)
        return 0

    lax.fori_loop(0, i, body, 0)
    o_ref[0] = acc_ref[...]


def _prompt_sb(q, k, v, tq):
    b, t, dh = q.shape
    seq = pl.BlockSpec((1, t, dh), lambda bi, i: (bi, 0, 0), pipeline_mode=pl.Buffered(1))
    return pl.pallas_call(
        functools.partial(_sb_kernel, tq=tq),
        grid=(b, t // tq),
        in_specs=[pl.BlockSpec((1, tq, dh), lambda bi, i: (bi, i, 0)), seq, seq],
        out_specs=pl.BlockSpec((1, tq, dh), lambda bi, i: (bi, i, 0)),
        out_shape=SDS((b, t, dh), F32),
        scratch_shapes=[pltpu.VMEM((N_HEADS, tq, tq), F32), pltpu.VMEM((N_HEADS, tq, tq), F32),
                        pltpu.VMEM((N_HEADS, tq, LANES), F32),
                        pltpu.VMEM((tq, dh), F32)],
        compiler_params=_params("arbitrary", "arbitrary"),
        name="prompt_sb",
    )(q, k, v)


def _merge_kernel(ya_ref, yb_ref, gate_ref, x_ref, g1_ref, n_ref, wa_ref, wb_ref, wo_ref, o_ref):
    d = x_ref.shape[1]
    ma = _dot(ya_ref[...].astype(BF16), wa_ref[...])
    mb = _dot(yb_ref[...].astype(BF16), wb_ref[...])
    merged = gate_ref[:, :d].astype(F32) * ma + gate_ref[:, d:].astype(F32) * mb
    mix = _dot(merged.astype(BF16), wo_ref[...])
    o_ref[...] = x_ref[...] + g1_ref[0] * _rms(mix, n_ref[...])


def _mod_spec(mod, tm, rows_per_mod):
    d = mod.shape[2]
    if mod.shape[1] == 1:
        return pl.BlockSpec((1, 1, d), lambda i: ((i * tm) // rows_per_mod, 0, 0))
    return pl.BlockSpec((1, tm, d), lambda i: (0, i, 0))


def _merge(ya, yb, gate, x, g1, norm, wa, wb, wo, tm, rows_per_mod):
    m, d = x.shape
    row = lambda a: pl.BlockSpec((tm, a.shape[1]), lambda i: (i, 0))
    full = lambda a: pl.BlockSpec(a.shape, lambda i: (0, 0))
    return pl.pallas_call(
        _merge_kernel,
        grid=(m // tm,),
        in_specs=[row(ya), row(yb), row(gate), row(x), _mod_spec(g1, tm, rows_per_mod),
                  full(norm), full(wa), full(wb), full(wo)],
        out_specs=pl.BlockSpec((tm, d), lambda i: (i, 0)),
        out_shape=SDS((m, d), F32),
        compiler_params=_params("arbitrary"),
        name="merge",
    )(ya, yb, gate, x, g1, norm, wa, wb, wo)


def _ffn_kernel(x_ref, sc_ref, sh_ref, g2_ref, npre_ref, npost_ref, wg_ref, wu_ref, wd_ref, o_ref, *, chunk):
    x = x_ref[...]
    h = (_rms(x, npre_ref[...]) * (1.0 + sc_ref[0]) + sh_ref[0]).astype(BF16)
    f = jnp.zeros(x.shape, F32)
    for c0 in range(0, wg_ref.shape[1], chunk):
        gate = _dot(h, wg_ref[:, c0:c0 + chunk])
        up = _dot(h, wu_ref[:, c0:c0 + chunk])
        act = gate / (1.0 + jnp.exp(-gate)) * up
        f = f + _dot(act.astype(BF16), wd_ref[c0:c0 + chunk, :])
    o_ref[...] = x + g2_ref[0] * _rms(f, npost_ref[...])


def _ffn(x, sc, sh, g2, npre, npost, wg, wu, wd, tm, rows_per_mod):
    m, d = x.shape
    full = lambda a: pl.BlockSpec(a.shape, lambda i: (0, 0))
    mod = lambda a: _mod_spec(a, tm, rows_per_mod)
    return pl.pallas_call(
        functools.partial(_ffn_kernel, chunk=256),
        grid=(m // tm,),
        in_specs=[pl.BlockSpec((tm, d), lambda i: (i, 0)), mod(sc), mod(sh), mod(g2),
                  full(npre), full(npost), full(wg), full(wu), full(wd)],
        out_specs=pl.BlockSpec((tm, d), lambda i: (i, 0)),
        out_shape=SDS((m, d), F32),
        compiler_params=_params("arbitrary"),
        name="ffn",
    )(x, sc, sh, g2, npre, npost, wg, wu, wd)


def _sample_score_kernel(pt_ref, q_ref, w_ref, knew_ref, *rest, pages):
    page_refs = rest[:pages]
    past_ref, new_ref = rest[pages:]
    q_hi, q_lo = _split2(q_ref[0])
    w = w_ref[0]
    t = q_ref.shape[1] // N_IDX_HEADS

    def score(keys):
        k_hi, k_lo = _split2(keys)
        r = jnp.maximum(_dot3_nt(q_hi, q_lo, k_hi, k_lo), 0.0) * w
        s = r[0:t]
        for h in range(1, N_IDX_HEADS):
            s = s + r[h * t:(h + 1) * t]
        return s

    for n in range(pages):
        past_ref[0, :, n * LANES:(n + 1) * LANES] = score(page_refs[n][...])
    new_ref[0] = score(knew_ref[0])


def _sample_scores(page_table, q_rows, w_rows, k_new, cache_k_idx, pages):
    nb, n_pages = page_table.shape
    page = cache_k_idx.shape[2]
    assert page == LANES
    rows = q_rows.shape[1]
    t = rows // N_IDX_HEADS

    def page_spec(n):
        return pl.BlockSpec((None, None, page, IDX_DIM), lambda b, g, pt: (0, pt[b, g * pages + n], 0, 0))

    grid_spec = pltpu.PrefetchScalarGridSpec(
        num_scalar_prefetch=1,
        grid=(nb, n_pages // pages),
        in_specs=[pl.BlockSpec((1, rows, IDX_DIM), lambda b, g, pt: (b, 0, 0)),
                  pl.BlockSpec((1, rows, LANES), lambda b, g, pt: (b, 0, 0)),
                  pl.BlockSpec((1, page, IDX_DIM), lambda b, g, pt: (b, 0, 0))]
                 + [page_spec(n) for n in range(pages)],
        out_specs=[pl.BlockSpec((1, t, pages * page), lambda b, g, pt: (b, 0, g)),
                   pl.BlockSpec((1, t, page), lambda b, g, pt: (b, 0, 0))],
    )
    return pl.pallas_call(
        functools.partial(_sample_score_kernel, pages=pages),
        grid_spec=grid_spec,
        out_shape=[SDS((nb, t, n_pages * page), F32), SDS((nb, t, page), F32)],
        compiler_params=_params("arbitrary", "arbitrary"),
        name="sample_scores",
    )(page_table, q_rows, w_rows, k_new, *([cache_k_idx] * pages))


def _sample_select_kernel(past_ref, new_ref, mask_ref, score_ref, thr_ref, need_ref, *, t, topk):
    rows, past = past_ref.shape
    score_ref[:, :past] = past_ref[...]
    q_t = lax.broadcasted_iota(I32, (rows, LANES), 0) % t
    col = lax.broadcasted_iota(I32, (rows, LANES), 1)
    score_ref[:, past:] = jnp.where(col <= q_t, new_ref[...], -jnp.inf)

    def emit(j, sel):
        mask_ref[:, pl.ds(pl.multiple_of(j * LANES, LANES), LANES)] = sel.astype(jnp.int8)

    width = score_ref.shape[1]
    gw = _group_width(width)
    _emit_selection(score_ref, thr_ref, need_ref, width // gw, gw, width // LANES, LANES, topk, emit)


def _sample_select(s_past, s_new, t, topk, tr):
    rows, past = s_past.shape
    width = past + LANES
    return pl.pallas_call(
        functools.partial(_sample_select_kernel, t=t, topk=float(topk)),
        grid=(rows // tr,),
        in_specs=[pl.BlockSpec((tr, past), lambda i: (i, 0)), pl.BlockSpec((tr, LANES), lambda i: (i, 0))],
        out_specs=pl.BlockSpec((tr, width), lambda i: (i, 0)),
        out_shape=SDS((rows, width), jnp.int8),
        scratch_shapes=[pltpu.VMEM((tr, width), F32), pltpu.VMEM((tr, 1), F32), pltpu.VMEM((tr, 1), F32)],
        compiler_params=_params("arbitrary"),
        name="sample_select",
    )(s_past, s_new)


def _sample_bias_kernel(rb_ref, o_ref, *, t, page):
    key = lax.broadcasted_iota(I32, (page, N_HEADS * t), 0)
    col = lax.broadcasted_iota(I32, (page, N_HEADS * t), 1)
    q_t = col % t
    head = col // t
    for tile in range(2):
        bucket = _t5_bucket(q_t - key + tile * page)
        val = jnp.zeros(bucket.shape, F32)
        for h in range(N_HEADS):
            val = jnp.where(head == h, _bias_lookup(rb_ref, bucket, h), val)
        o_ref[tile] = val
    far = jnp.zeros(bucket.shape, F32)
    for h in range(N_HEADS):
        far = jnp.where(head == h, rb_ref[N_BUCKETS - 1, h], far)
    o_ref[2] = far


def _sample_bias(rel_bias, t, page):
    assert page >= MAX_DISTANCE
    return pl.pallas_call(
        functools.partial(_sample_bias_kernel, t=t, page=page),
        in_specs=[pl.BlockSpec(memory_space=pltpu.SMEM)],
        out_shape=SDS((3, page, N_HEADS * t), F32),
        name="sample_bias",
    )(rel_bias)


def _to_column(row_vec, n):
    eye = lax.broadcasted_iota(I32, (n, n), 0) == lax.broadcasted_iota(I32, (n, n), 1)
    return jnp.sum(jnp.where(eye, row_vec, 0.0), axis=1, keepdims=True)


def _sample_attn_kernel(pt_ref, qa_ref, qb_ref, knew_a_ref, vnew_a_ref, knew_b_ref, vnew_b_ref,
                        mnew_ref, mask_ref, bias_ref, *rest, pages, t):
    ka_refs = rest[0 * pages:1 * pages]
    va_refs = rest[1 * pages:2 * pages]
    kb_refs = rest[2 * pages:3 * pages]
    vb_refs = rest[3 * pages:4 * pages]
    ya_ref, yb_ref, m_ref, l_ref, acca_ref, c_ref, accb_ref = rest[4 * pages:]
    g = pl.program_id(1)
    n_groups = pl.num_programs(1)
    page = LANES
    cols = N_HEADS * t
    qa = qa_ref[0]
    qb = qb_ref[0]
    srow = lax.broadcasted_iota(I32, (page, page), 0)
    scol = lax.broadcasted_iota(I32, (page, page), 1)
    later = jnp.where(scol > srow, 1.0, 0.0).astype(BF16)

    def dsa_logits(k, bias):
        return _dot(k.astype(BF16), qa) + bias

    def dsa_update(blocks):
        m_old = m_ref[...]
        m_new = m_old
        for logits, valid, _ in blocks:
            m_new = jnp.maximum(m_new, jnp.max(jnp.where(valid, logits, NEG), axis=0, keepdims=True))
        alpha = jnp.exp(m_old - m_new)
        l_new = alpha * l_ref[...]
        contrib = jnp.zeros(acca_ref.shape, F32)
        for logits, valid, v in blocks:
            pexp = jnp.where(valid, jnp.exp(logits - m_new), 0.0)
            l_new = l_new + jnp.sum(pexp, axis=0, keepdims=True)
            contrib = contrib + _dot(pexp.T.astype(BF16), v.astype(BF16))
        m_ref[...] = m_new
        l_ref[...] = l_new
        acca_ref[...] = _to_column(alpha, cols) * acca_ref[...] + contrib

    def sb_block(k, v, earlier):
        z = _dot(k.astype(BF16), qb)
        lsm = _log_one_minus_beta(z)
        if earlier is not None:
            lsm = jnp.where(earlier, lsm, 0.0)
        hi, lo = _split2(lsm)
        between = _dot(later, hi) + _dot(later, lo)
        a = jnp.exp(z + lsm + between + c_ref[...])
        if earlier is not None:
            a = jnp.where(earlier, a, 0.0)
        c_ref[...] = c_ref[...] + between[:1, :] + lsm[:1, :]
        accb_ref[...] = accb_ref[...] + _dot(a.T.astype(BF16), v.astype(BF16))

    @pl.when(g == 0)
    def _():
        m_ref[...] = jnp.full(m_ref.shape, NEG, F32)
        l_ref[...] = jnp.zeros(l_ref.shape, F32)
        acca_ref[...] = jnp.zeros(acca_ref.shape, F32)
        c_ref[...] = jnp.zeros(c_ref.shape, F32)
        accb_ref[...] = jnp.zeros(accb_ref.shape, F32)
        key = lax.broadcasted_iota(I32, (page, cols), 0)
        q_t = lax.broadcasted_iota(I32, (page, cols), 1) % t
        valid = mnew_ref[0].astype(I32) != 0
        dsa_update([(dsa_logits(knew_a_ref[0], bias_ref[0]), valid, vnew_a_ref[0])])
        sb_block(knew_b_ref[0], vnew_b_ref[0], key < q_t)

    blocks = []
    for n in range(pages):
        valid = mask_ref[0, (pages - 1 - n) * page:(pages - n) * page, :].astype(I32) != 0
        last_page = jnp.logical_and(g == 0, n == 0)
        bias = jnp.where(last_page, bias_ref[1], bias_ref[2])
        blocks.append((dsa_logits(ka_refs[n][...], bias), valid, va_refs[n][...]))
    dsa_update(blocks)
    for n in range(pages):
        sb_block(kb_refs[n][...], vb_refs[n][...], None)

    @pl.when(g == n_groups - 1)
    def _():
        head_of_col = lax.broadcasted_iota(I32, (t, N_HEADS * HEAD_DIM), 1) // HEAD_DIM
        norm = acca_ref[...] / _to_column(l_ref[...], cols)
        ya = jnp.zeros((t, N_HEADS * HEAD_DIM), F32)
        yb = jnp.zeros((t, N_HEADS * HEAD_DIM), F32)
        for h in range(N_HEADS):
            ya = ya + jnp.where(head_of_col == h, norm[h * t:(h + 1) * t], 0.0)
            yb = yb + jnp.where(head_of_col == h, accb_ref[h * t:(h + 1) * t, :], 0.0)
        ya_ref[0] = ya
        yb_ref[0] = yb


def _sample_attn(page_table, qa_bd, qb_bd, knew_a, vnew_a, knew_b, vnew_b, mask_new, mask_past, bias,
                 cache_k_a, cache_v_a, cache_k_b, cache_v_b, pages, t):
    nb, n_pages = page_table.shape
    page = LANES
    dh = N_HEADS * HEAD_DIM
    cols = N_HEADS * t
    n_groups = n_pages // pages

    def page_spec(n):
        return pl.BlockSpec((None, None, page, dh),
                            lambda b, g, pt: (0, pt[b, n_pages - 1 - (g * pages + n)], 0, 0))

    per_seq = lambda shape: pl.BlockSpec((1,) + shape, lambda b, g, pt: (b, 0, 0))
    grid_spec = pltpu.PrefetchScalarGridSpec(
        num_scalar_prefetch=1,
        grid=(nb, n_groups),
        in_specs=[per_seq((dh, cols)), per_seq((dh, cols)),
                  per_seq((page, dh)), per_seq((page, dh)), per_seq((page, dh)), per_seq((page, dh)),
                  per_seq((page, cols)),
                  pl.BlockSpec((1, pages * page, cols), lambda b, g, pt: (b, n_groups - 1 - g, 0)),
                  pl.BlockSpec(bias.shape, lambda b, g, pt: (0, 0, 0))]
                 + [page_spec(n) for n in range(pages)] * 4,
        out_specs=[per_seq((t, dh)), per_seq((t, dh))],
        scratch_shapes=[pltpu.VMEM((1, cols), F32), pltpu.VMEM((1, cols), F32), pltpu.VMEM((cols, dh), F32),
                        pltpu.VMEM((1, cols), F32), pltpu.VMEM((cols, dh), F32)],
    )
    flat = lambda c: c.reshape(c.shape[0], c.shape[1], c.shape[2], dh)
    return pl.pallas_call(
        functools.partial(_sample_attn_kernel, pages=pages, t=t),
        grid_spec=grid_spec,
        out_shape=[SDS((nb, t, dh), F32), SDS((nb, t, dh), F32)],
        compiler_params=_params("arbitrary", "arbitrary"),
        name="sample_attn",
    )(page_table, qa_bd, qb_bd, knew_a, vnew_a, knew_b, vnew_b, mask_new, mask_past, bias,
      *([flat(cache_k_a)] * pages), *([flat(cache_v_a)] * pages),
      *([flat(cache_k_b)] * pages), *([flat(cache_v_b)] * pages))


def _largest_divisor(n, cap):
    d = min(n, cap)
    while n % d:
        d -= 1
    return d


def _block_diag_queries(q, nb, t):
    q4 = q.astype(F32).reshape(nb, t, N_HEADS, HEAD_DIM)
    eye = jnp.eye(N_HEADS, dtype=F32)
    bd = jnp.einsum('bthd,hg->bhdgt', q4, eye)
    return bd.reshape(nb, N_HEADS * HEAD_DIM, N_HEADS * t).astype(BF16)


def kernel(x_prompt, x_sample, c_prompt, c_sample, cache_k_a, cache_v_a, cache_k_idx, cache_k_b, cache_v_b,
           page_table, rel_bias, w_ada, b_ada, norm_pre_mix, norm_post_mix, norm_pre_ffn, norm_post_ffn,
           w_in, w_br_a, w_br_b, w_out, w_ffn_gate, w_ffn_up, w_ffn_down):
    depth = w_in.shape[0]
    assert depth == 1
    nb_p, t_p, d = x_prompt.shape
    nb_s, t_s, _ = x_sample.shape
    n_pages = page_table.shape[1]
    page = cache_k_a.shape[2]
    past = n_pages * page
    dh = N_HEADS * HEAD_DIM
    nq = N_IDX_HEADS * IDX_DIM
    topk_p = min(INDEX_TOPK, t_p // 4)
    topk_s = min(INDEX_TOPK, (past + t_s) // 4)
    tq = 256
    assert t_p % tq == 0 and topk_p <= tq and t_s * N_HEADS <= LANES and page == LANES

    w = w_in[0]
    o = np.cumsum([0, dh, dh, dh, nq, IDX_DIM, N_IDX_HEADS, dh, dh, dh, d, d])
    col = lambda n: w[:, o[n]:o[n + 1]]
    w_main = jnp.concatenate([col(0), col(1), col(2), col(6), col(7), col(8)], axis=1).astype(BF16)
    w_gate = jnp.concatenate([col(9), col(10)], axis=1).astype(BF16)
    w_idx = jnp.concatenate([col(3), col(4), col(4), col(5),
                             jnp.zeros((d, LANES - N_IDX_HEADS), F32)], axis=1)
    w_idx_hi = w_idx.astype(BF16)
    w_idx_lo = (w_idx - w_idx_hi.astype(F32)).astype(BF16)
    wa, wb, wo = w_br_a[0].astype(BF16), w_br_b[0].astype(BF16), w_out[0].astype(BF16)
    wg, wu, wd = w_ffn_gate[0].astype(BF16), w_ffn_up[0].astype(BF16), w_ffn_down[0].astype(BF16)
    n_pre_mix, n_post_mix = norm_pre_mix[0][None], norm_post_mix[0][None]
    n_pre_ffn, n_post_ffn = norm_pre_ffn[0][None], norm_post_ffn[0][None]

    n_c = nb_p + nb_s
    c_all = jnp.concatenate([c_prompt, c_sample, jnp.zeros((-n_c % 8, d), F32)], axis=0)
    ada = _ada(c_all, w_ada[0], b_ada[0][None])
    mods_p = [m[:, None, :] for m in jnp.split(ada[:nb_p], 6, axis=-1)]
    mods_s = [jnp.repeat(m, t_s, axis=0)[None] for m in jnp.split(ada[nb_p:n_c], 6, axis=-1)]

    def dense_front(x2, mods, tm, rows_per_mod, q_scale):
        return _inproj(x2, mods[1], mods[0], n_pre_mix, w_main, w_gate, w_idx_hi, w_idx_lo, tm, rows_per_mod,
                       q_scale)

    def dense_back(ya, yb, gate, x2, mods, tm, rows_per_mod):
        x1 = _merge(ya, yb, gate, x2, mods[2], n_post_mix, wa, wb, wo, tm, rows_per_mod)
        return _ffn(x1, mods[4], mods[3], mods[5], n_pre_ffn, n_post_ffn, wg, wu, wd, tm, rows_per_mod)

    m_p = nb_p * t_p
    xp2 = x_prompt.reshape(m_p, d)
    tm_p = 256
    (qa, ka, va, qb, kb, vb, kab, vab, kbb, vbb, gate, iq, ik, wi, kidx, wit) = dense_front(
        xp2, mods_p, tm_p, t_p, LOG2E * HEAD_DIM ** -0.5)
    seq3 = lambda a: a.reshape(nb_p, t_p, a.shape[-1])
    mask = _prompt_index(seq3(iq), wit, seq3(ik), tq, topk_p)
    ya = _prompt_dsa(seq3(qa), seq3(kab), seq3(vab), mask, _bias_tiles(rel_bias, tq), tq)
    yb = _prompt_sb(seq3(qb), seq3(kbb), seq3(vbb), tq)
    y_p = dense_back(ya.reshape(m_p, dh), yb.reshape(m_p, dh), gate, xp2, mods_p, tm_p, t_p)
    heads_p = lambda a: a.reshape(depth, nb_p, t_p, N_HEADS, HEAD_DIM)
    outs_p = (heads_p(ka), heads_p(va), kidx.reshape(depth, nb_p, t_p, IDX_DIM), heads_p(kb), heads_p(vb))

    m_s = nb_s * t_s
    xs2 = x_sample.reshape(m_s, d)
    tm_s = _largest_divisor(m_s, 256)
    (qa, ka, va, qb, kb, vb, _, _, _, _, gate, iq, ik, wi, kidx, _) = dense_front(
        xs2, mods_s, tm_s, t_s, HEAD_DIM ** -0.5)
    pad_keys = lambda a: jnp.pad(a.reshape(nb_s, t_s, a.shape[-1]), ((0, 0), (0, page - t_s), (0, 0)))
    q_rows = iq.reshape(nb_s, t_s, N_IDX_HEADS, IDX_DIM).transpose(0, 2, 1, 3).reshape(nb_s, N_IDX_HEADS * t_s, IDX_DIM)
    w_rows = wi.reshape(nb_s, t_s, LANES)[:, :, :N_IDX_HEADS].transpose(0, 2, 1).reshape(nb_s, N_IDX_HEADS * t_s, 1)
    w_rows = jnp.broadcast_to(w_rows, (nb_s, N_IDX_HEADS * t_s, LANES))
    pages_idx = _largest_divisor(n_pages, 16)
    s_past, s_new = _sample_scores(page_table, q_rows, w_rows, pad_keys(kidx), cache_k_idx, pages_idx)
    mask_s = _sample_select(s_past.reshape(m_s, past), s_new.reshape(m_s, page), t_s, topk_s,
                            _largest_divisor(m_s, 128))
    mask_t = jnp.tile(mask_s.reshape(nb_s, t_s, past + page).transpose(0, 2, 1), (1, 1, N_HEADS))
    pages_attn = _largest_divisor(n_pages, 4)
    ya, yb = _sample_attn(page_table, _block_diag_queries(qa, nb_s, t_s), _block_diag_queries(qb, nb_s, t_s),
                          pad_keys(ka), pad_keys(va), pad_keys(kb), pad_keys(vb),
                          mask_t[:, past:], mask_t[:, :past], _sample_bias(rel_bias, t_s, page),
                          cache_k_a, cache_v_a, cache_k_b, cache_v_b, pages_attn, t_s)
    y_s = dense_back(ya.reshape(m_s, dh), yb.reshape(m_s, dh), gate, xs2, mods_s, tm_s, t_s)
    heads_s = lambda a: a.reshape(depth, nb_s, t_s, N_HEADS, HEAD_DIM)
    outs_s = (heads_s(ka), heads_s(va), kidx.reshape(depth, nb_s, t_s, IDX_DIM), heads_s(kb), heads_s(vb))

    return (y_p.reshape(nb_p, t_p, d), y_s.reshape(nb_s, t_s, d)) + outs_p + outs_s
```

```python
import functools
import math

import numpy as np
import jax
import jax.numpy as jnp
from jax import lax
from jax.experimental import pallas as pl
from jax.experimental.pallas import tpu as pltpu

F32 = jnp.float32
BF16 = jnp.bfloat16
I32 = jnp.int32
SDS = jax.ShapeDtypeStruct

HEAD_DIM = 64
N_HEADS = 8
N_PAIRS = N_HEADS // 2
N_IDX_HEADS = 4
IDX_DIM = 64
INDEX_TOPK = 256
N_BUCKETS = 32
MAX_EXACT = N_BUCKETS // 2
MAX_DISTANCE = 128
EPS = 1e-6
LANES = 128
SUBLANES = 8
NEG = -1e30
LOG2E = 1.4426950408889634
VMEM_LIMIT = 56 * 1024 * 1024

_I32_MIN = np.int32(-2 ** 31)
_KEY_NEG_INF = np.int32(0x807FFFFF - 2 ** 32)


def _params(*sem):
    return pltpu.CompilerParams(dimension_semantics=sem, vmem_limit_bytes=VMEM_LIMIT)


def _dot(a, b):
    return jnp.dot(a, b, preferred_element_type=F32)


def _dot_nt(a, b):
    return lax.dot_general(a, b, (((1,), (1,)), ((), ())), preferred_element_type=F32)


def _split2(x):
    hi = x.astype(BF16)
    lo = (x - hi.astype(F32)).astype(BF16)
    return hi, lo


def _dot3(a_hi, a_lo, b_hi, b_lo):
    return _dot(a_hi, b_hi) + _dot(a_lo, b_hi) + _dot(a_hi, b_lo)


def _dot3_nt(a_hi, a_lo, b_hi, b_lo):
    return _dot_nt(a_hi, b_hi) + _dot_nt(a_lo, b_hi) + _dot_nt(a_hi, b_lo)


def _rms(x, g):
    var = jnp.mean(x * x, axis=-1, keepdims=True)
    return x * lax.rsqrt(var + EPS) * g


def _ada_kernel(c_ref, w_ref, b_ref, o_ref):
    c_hi, c_lo = _split2(c_ref[...])
    w_hi, w_lo = _split2(w_ref[...])
    o_ref[...] = _dot3(c_hi, c_lo, w_hi, w_lo) + b_ref[...]


def _ada(c, w, b):
    rows, d = c.shape
    n = w.shape[1]
    tn = 1024
    return pl.pallas_call(
        _ada_kernel,
        grid=(n // tn,),
        in_specs=[pl.BlockSpec((rows, d), lambda j: (0, 0)),
                  pl.BlockSpec((d, tn), lambda j: (0, j)),
                  pl.BlockSpec((1, tn), lambda j: (0, j))],
        out_specs=pl.BlockSpec((rows, tn), lambda j: (0, j)),
        out_shape=SDS((rows, n), F32),
        compiler_params=_params("arbitrary"),
        name="ada",
    )(c, w, b)


def _inproj_kernel(x_ref, sc_ref, sh_ref, g_ref, wm_ref, wg_ref, wih_ref, wil_ref,
                   qa_ref, ka_ref, va_ref, qb_ref, kb_ref, vb_ref,
                   kab_ref, vab_ref, kbb_ref, vbb_ref,
                   gate_ref, iq_ref, ik_ref, wi_ref, kidx_ref, wit_ref, *, q_scale):
    dh = N_HEADS * HEAD_DIM
    h = _rms(x_ref[...], g_ref[...]) * (1.0 + sc_ref[0]) + sh_ref[0]
    h_hi, h_lo = _split2(h)

    def seg(n):
        return _dot(h_hi, wm_ref[:, n * dh:(n + 1) * dh])

    qa_ref[...] = (seg(0) * q_scale).astype(BF16)
    ka = seg(1)
    ka_ref[...] = ka
    kab_ref[...] = ka.astype(BF16)
    va = seg(2)
    va_ref[...] = va
    vab_ref[...] = va.astype(BF16)
    qb_ref[...] = (seg(3) * q_scale).astype(BF16)
    kb = seg(4)
    kb_ref[...] = kb
    kbb_ref[...] = kb.astype(BF16)
    vb = seg(5)
    vb_ref[...] = vb
    vbb_ref[...] = vb.astype(BF16)

    d = x_ref.shape[1]
    for n in range(2):
        g = _dot(h_hi, wg_ref[:, n * d:(n + 1) * d])
        gate_ref[:, n * d:(n + 1) * d] = (1.0 / (1.0 + jnp.exp(-g))).astype(BF16)

    nq = N_IDX_HEADS * IDX_DIM
    idx = _dot3(h_hi, h_lo, wih_ref[...], wil_ref[...])
    iq_ref[...] = idx[:, :nq] * IDX_DIM ** -0.5
    ik_ref[...] = idx[:, nq:nq + LANES]
    kidx_ref[...] = idx[:, nq:nq + IDX_DIM]
    wi = idx[:, nq + LANES:nq + 2 * LANES] * N_IDX_HEADS ** -0.5
    wi_ref[...] = wi
    wit_ref[...] = wi.T[:SUBLANES, :]


def _inproj(x, sc, sh, g, wm, wg, wih, wil, tm, rows_per_mod, q_scale):
    m, d = x.shape
    dh = N_HEADS * HEAD_DIM
    r = sc.shape[1]
    if r == 1:
        mod_map = lambda i: ((i * tm) // rows_per_mod, 0, 0)
    else:
        mod_map = lambda i: (0, i, 0)
    row = lambda w: pl.BlockSpec((tm, w), lambda i: (i, 0))
    full = lambda a: pl.BlockSpec(a.shape, lambda i: (0, 0))
    nq = N_IDX_HEADS * IDX_DIM
    outs = [(dh, BF16), (dh, F32), (dh, F32), (dh, BF16), (dh, F32), (dh, F32),
            (dh, BF16), (dh, BF16), (dh, BF16), (dh, BF16),
            (2 * d, BF16), (nq, F32), (LANES, F32), (LANES, F32), (IDX_DIM, F32)]
    return pl.pallas_call(
        functools.partial(_inproj_kernel, q_scale=q_scale),
        grid=(m // tm,),
        in_specs=[row(d), pl.BlockSpec((1, r, d), mod_map), pl.BlockSpec((1, r, d), mod_map),
                  full(g), full(wm), full(wg), full(wih), full(wil)],
        out_specs=[row(w) for w, _ in outs] + [pl.BlockSpec((SUBLANES, tm), lambda i: (0, i))],
        out_shape=[SDS((m, w), dt) for w, dt in outs] + [SDS((SUBLANES, m), F32)],
        compiler_params=_params("arbitrary"),
        name="inproj",
    )(x, sc, sh, g, wm, wg, wih, wil)


ROW_CHUNK = 64
MAX_GROUPS = 256


def _key_to_float(key):
    bits = jnp.where(key < 0, key ^ jnp.int32(0x7FFFFFFF), key)
    f = lax.bitcast_convert_type(bits, F32)
    return jnp.where(key < _KEY_NEG_INF, -jnp.inf, f)


def _float_to_key(f):
    bits = lax.bitcast_convert_type(f, I32)
    return jnp.where(bits < 0, bits ^ jnp.int32(0x7FFFFFFF), bits)


def _group_width(width):
    gw = min(width, 8 * LANES)
    while width % gw:
        gw -= LANES
    return gw


def _count(score_ref, r0, ngroups, gw, cf, strict):
    cfb = jnp.broadcast_to(cf, (ROW_CHUNK, LANES))

    def body(g, acc):
        base = pl.multiple_of(g * gw, gw)
        for c in range(gw // LANES):
            s = score_ref[pl.ds(r0, ROW_CHUNK), pl.ds(base + c * LANES, LANES)]
            hit = (s > cfb) if strict else (s >= cfb)
            acc = acc + jnp.where(hit, 1.0, 0.0)
        return acc

    acc = lax.fori_loop(0, ngroups, body, jnp.zeros((ROW_CHUNK, LANES), F32))
    return jnp.sum(acc, axis=1, keepdims=True)


def _kth_largest(score_ref, r0, ngroups, gw, k):
    ninf = jnp.full((ROW_CHUNK, LANES), -jnp.inf, F32)

    def bounds(g, carry):
        base = pl.multiple_of(g * gw, gw)
        tops = list(carry)
        for c in range(gw // LANES):
            s = score_ref[pl.ds(r0, ROW_CHUNK), pl.ds(base + c * LANES, LANES)]
            tops[c % 2] = jnp.maximum(tops[c % 2], s)
        return tuple(tops)

    top_even, top_odd = lax.fori_loop(0, ngroups, bounds, (ninf, ninf))
    upper = jnp.max(jnp.maximum(top_even, top_odd), axis=1, keepdims=True)
    lower = jnp.min(jnp.minimum(top_even, top_odd), axis=1, keepdims=True)
    lo0 = _float_to_key(lower)
    hi0 = _float_to_key(upper) + 1

    def n_active(lo, hi):
        return jnp.sum(jnp.where(hi > lo + 1, 1.0, 0.0))

    def cond(state):
        return state[2] > 0.0

    def step(state):
        lo, hi, _ = state
        mid = lax.shift_right_arithmetic(lo, 1) + lax.shift_right_arithmetic(hi, 1) + (lo & hi & 1)
        cnt = _count(score_ref, r0, ngroups, gw, _key_to_float(mid), strict=False)
        keep = cnt >= k
        lo = jnp.where(keep, mid, lo)
        hi = jnp.where(cnt == k, mid + 1, jnp.where(keep, hi, mid))
        return lo, hi, n_active(lo, hi)

    lo, _, _ = lax.while_loop(cond, step, (lo0, hi0, n_active(lo0, hi0)))
    return _key_to_float(lo)


def _emit_selection(score_ref, thr_ref, need_ref, ngroups, gw, nblk, bw, k, emit):
    rows = score_ref.shape[0]
    assert k <= MAX_GROUPS and rows % ROW_CHUNK == 0

    def chunk(n, _):
        r0 = pl.multiple_of(n * ROW_CHUNK, ROW_CHUNK)
        thr = _kth_largest(score_ref, r0, ngroups, gw, k)
        thr_ref[pl.ds(r0, ROW_CHUNK), :] = thr
        need_ref[pl.ds(r0, ROW_CHUNK), :] = k - _count(score_ref, r0, ngroups, gw, thr, strict=True)
        return 0

    lax.fori_loop(0, rows // ROW_CHUNK, chunk, 0)
    thr = thr_ref[...]
    need = need_ref[...]
    finite = jnp.where(thr > -jnp.inf, 1.0, 0.0)
    before = (lax.broadcasted_iota(I32, (bw, bw), 0) < lax.broadcasted_iota(I32, (bw, bw), 1))
    before = jnp.where(before, 1.0, 0.0).astype(BF16)

    def body(j, carry):
        s = score_ref[:, pl.ds(pl.multiple_of(j * bw, bw), bw)]
        tie = jnp.where(s == thr, finite, 0.0)
        rank = _dot(tie.astype(BF16), before) + carry
        sel = jnp.where(s > thr, 1.0, jnp.where(rank < need, tie, 0.0))
        emit(j, sel)
        return carry + jnp.sum(tie, axis=1, keepdims=True)

    lax.fori_loop(0, nblk, body, jnp.zeros((rows, 1), F32))


def _half_masks(shape):
    lane = lax.broadcasted_iota(I32, shape, 1)
    return lane < HEAD_DIM


def _index_kernel(iq_ref, wit_ref, ik_ref, mask_ref, score_ref, top_ref, *, tq, topk):
    i = pl.program_id(1)
    nblk_total = mask_ref.shape[2] // tq
    assert topk <= tq
    lo_half = _half_masks((tq, LANES))
    q_parts = []
    for h in range(N_IDX_HEADS):
        qp = iq_ref[0, :, (h // 2) * LANES:(h // 2 + 1) * LANES]
        qh = jnp.where(lo_half, qp, 0.0) if h % 2 == 0 else jnp.where(lo_half, 0.0, qp)
        q_hi, q_lo = _split2(qh)
        q_parts.append(jnp.concatenate([q_hi, q_hi, q_lo], axis=1))
    w_rows = [wit_ref[h:h + 1, :] for h in range(N_IDX_HEADS)]
    key = lax.broadcasted_iota(I32, (tq, tq), 0)
    qry = lax.broadcasted_iota(I32, (tq, tq), 1)
    top_ref[...] = jnp.full((tq, tq), -jnp.inf, F32)

    def score_block(j, _):
        base = pl.multiple_of(j * tq, tq)
        k_hi, k_lo = _split2(ik_ref[0, pl.ds(base, tq), :])
        k_parts = jnp.concatenate([k_hi, k_lo, k_hi], axis=1)
        s = jnp.zeros((tq, tq), F32)
        for h in range(N_IDX_HEADS):
            d = _dot_nt(k_parts, q_parts[h])
            s = s + w_rows[h] * jnp.maximum(d, 0.0)
        s = jnp.where(key + (j - i) * tq <= qry, s, -jnp.inf)
        score_ref[pl.ds(base, tq), :] = s
        top_ref[...] = jnp.maximum(top_ref[...], s)
        return 0

    lax.fori_loop(0, i + 1, score_block, 0)

    def count(cf, strict):
        cfb = jnp.broadcast_to(cf, (SUBLANES, tq))

        def body(j, accs):
            blk = score_ref[pl.ds(pl.multiple_of(j * tq, tq), tq), :]
            accs = list(accs)
            for r in range(tq // SUBLANES):
                s = blk[r * SUBLANES:(r + 1) * SUBLANES]
                hit = (s > cfb) if strict else (s >= cfb)
                accs[r % len(accs)] = accs[r % len(accs)] + jnp.where(hit, 1.0, 0.0)
            return tuple(accs)

        zero = jnp.zeros((SUBLANES, tq), F32)
        accs = lax.fori_loop(0, i + 1, body, (zero, zero, zero, zero))
        return jnp.sum((accs[0] + accs[1]) + (accs[2] + accs[3]), axis=0, keepdims=True)

    tops = top_ref[...]
    lo0 = _float_to_key(jnp.min(tops, axis=0, keepdims=True))
    hi0 = _float_to_key(jnp.max(tops, axis=0, keepdims=True)) + 1

    def n_active(lo, hi):
        return jnp.sum(jnp.where(hi > lo + 1, 1.0, 0.0))

    def step(state):
        lo, hi, _ = state
        mid = lax.shift_right_arithmetic(lo, 1) + lax.shift_right_arithmetic(hi, 1) + (lo & hi & 1)
        cnt = count(_key_to_float(mid), strict=False)
        keep = cnt >= topk
        lo = jnp.where(keep, mid, lo)
        hi = jnp.where(cnt == topk, mid + 1, jnp.where(keep, hi, mid))
        return lo, hi, n_active(lo, hi)

    at_least_zero = count(jnp.zeros((1, tq), F32), strict=False)
    above_zero = count(jnp.zeros((1, tq), F32), strict=True)
    zero_kth = jnp.logical_and(above_zero < topk, at_least_zero >= topk)
    lo0 = jnp.where(zero_kth, 0, jnp.where(above_zero >= topk, jnp.maximum(lo0, 1), lo0))
    hi0 = jnp.where(zero_kth, 1, jnp.where(at_least_zero < topk, jnp.minimum(hi0, 0), hi0))

    lo, _, _ = lax.while_loop(lambda state: state[2] > 0.0, step, (lo0, hi0, n_active(lo0, hi0)))
    thr = _key_to_float(lo)
    need = topk - count(thr, strict=True)
    finite = jnp.where(thr > -jnp.inf, 1.0, 0.0)
    before = jnp.where(qry < key, 1.0, 0.0).astype(BF16)
    eye = jnp.where(qry == key, 1.0, 0.0).astype(BF16)

    def emit(j, carry):
        base = pl.multiple_of(j * tq, tq)
        s = score_ref[pl.ds(base, tq), :]
        tie = jnp.where(s == thr, finite, 0.0)
        rank = _dot(before, tie.astype(BF16)) + carry
        sel = jnp.where(s > thr, 1.0, jnp.where(rank < need, tie, 0.0))
        mask_ref[0, :, pl.ds(base, tq)] = _dot_nt(eye, sel.astype(BF16)).astype(jnp.int8)
        return carry + jnp.sum(tie, axis=0, keepdims=True)

    carry = lax.fori_loop(0, (i + 1) // 2, lambda jj, c: emit(2 * jj + 1, emit(2 * jj, c)), jnp.zeros((1, tq), F32))

    @pl.when((i + 1) % 2 == 1)
    def _():
        emit(i, carry)

    def clear(j, _):
        mask_ref[0, :, pl.ds(pl.multiple_of(j * tq, tq), tq)] = jnp.zeros((tq, tq), jnp.int8)
        return 0

    lax.fori_loop(i + 1, nblk_total, clear, 0)


def _prompt_index(iq, wit, ik, tq, topk):
    b, t, _ = iq.shape
    nq = t // tq
    return pl.pallas_call(
        functools.partial(_index_kernel, tq=tq, topk=float(topk)),
        grid=(b, nq),
        in_specs=[pl.BlockSpec((1, tq, iq.shape[2]), lambda bi, i: (bi, i, 0)),
                  pl.BlockSpec((SUBLANES, tq), lambda bi, i: (0, bi * nq + i)),
                  pl.BlockSpec((1, t, LANES), lambda bi, i: (bi, 0, 0))],
        out_specs=pl.BlockSpec((1, tq, t), lambda bi, i: (bi, i, 0)),
        out_shape=SDS((b, t, t), jnp.int8),
        scratch_shapes=[pltpu.VMEM((t, tq), F32), pltpu.VMEM((tq, tq), F32)],
        compiler_params=_params("arbitrary", "arbitrary"),
        name="prompt_index",
    )(iq, wit, ik)


def _t5_bucket(rel):
    rel = jnp.maximum(rel, 0)
    lf = (jnp.log(jnp.maximum(rel, 1).astype(F32) / MAX_EXACT)
          / math.log(MAX_DISTANCE / MAX_EXACT) * (N_BUCKETS - MAX_EXACT))
    large = jnp.minimum(MAX_EXACT + lf.astype(I32), N_BUCKETS - 1)
    return jnp.where(rel < MAX_EXACT, rel, large)


def _bias_lookup(rb_ref, bucket, h):
    val = jnp.zeros(bucket.shape, F32)
    for b in range(N_BUCKETS):
        val = jnp.where(bucket == b, rb_ref[b, h], val)
    return val


def _bias_tiles_kernel(rb_ref, o_ref, *, tq):
    h = pl.program_id(0)
    row = lax.broadcasted_iota(I32, (tq, tq), 0)
    col = lax.broadcasted_iota(I32, (tq, tq), 1)
    far = rb_ref[N_BUCKETS - 1, h]
    for tile in range(2):
        o_ref[tile, 0] = (_bias_lookup(rb_ref, _t5_bucket(row - col + tile * tq), h) - far) * LOG2E
    o_ref[2, 0] = jnp.zeros((tq, tq), F32)


def _bias_tiles(rel_bias, tq):
    assert tq >= MAX_DISTANCE
    return pl.pallas_call(
        functools.partial(_bias_tiles_kernel, tq=tq),
        grid=(N_HEADS,),
        in_specs=[pl.BlockSpec(memory_space=pltpu.SMEM)],
        out_specs=pl.BlockSpec((3, 1, tq, tq), lambda h: (0, h, 0, 0)),
        out_shape=SDS((3, N_HEADS, tq, tq), F32),
        compiler_params=_params("arbitrary"),
        name="bias_tiles",
    )(rel_bias)


def _pair_queries(q_ref, tq):
    lo_half = _half_masks((tq, LANES))
    out = []
    for p in range(N_PAIRS):
        qp = q_ref[0, :, p * LANES:(p + 1) * LANES].astype(F32)
        out.append((jnp.where(lo_half, qp, 0.0).astype(BF16), jnp.where(lo_half, 0.0, qp).astype(BF16)))
    return out, lo_half


def _dsa_kernel(q_ref, k_ref, v_ref, mask_ref, bias_ref, o_ref, logit_ref, madd_ref, m_ref, acc_ref, *, tq):
    i = pl.program_id(1)
    tk = 2 * tq
    qs, lo_half = _pair_queries(q_ref, tq)
    m_ref[...] = jnp.full(m_ref.shape, NEG, F32)
    acc_ref[...] = jnp.zeros(acc_ref.shape, F32)
    keep_lo = jnp.where(_half_masks((tk, LANES)), 1.0, 0.0).astype(BF16)
    keep_hi = 1.0 - keep_lo

    def pair_step(jj, near):
        base = pl.multiple_of(jj * tk, tk)
        madd_ref[...] = jnp.where(mask_ref[0, :, pl.ds(base, tk)].astype(I32) != 0, 0.0, NEG)
        for p in range(N_PAIRS):
            for u in range(2):
                kp = k_ref[0, pl.ds(base + u * tq, tq), p * LANES:(p + 1) * LANES]
                for hh in range(2):
                    h = 2 * p + hh
                    lm = _dot_nt(qs[p][hh], kp) + madd_ref[:, u * tq:(u + 1) * tq]
                    if near:
                        lm = lm + bias_ref[jnp.clip(i - (2 * jj + u), 0, 2), h]
                    logit_ref[h, :, u * tq:(u + 1) * tq] = lm
        for p in range(N_PAIRS):
            vp = v_ref[0, pl.ds(base, tk), p * LANES:(p + 1) * LANES]
            for hh in range(2):
                h = 2 * p + hh
                m_old = m_ref[h]
                m_new = jnp.maximum(m_old, jnp.max(logit_ref[h], axis=1, keepdims=True))
                m_ref[h] = m_new
                pexp = jnp.concatenate(
                    [jnp.exp2(logit_ref[h, :, c * LANES:(c + 1) * LANES] - m_new) for c in range(tk // LANES)],
                    axis=1)
                vaug = vp * keep_lo + keep_hi if hh == 0 else vp * keep_hi + keep_lo
                acc_ref[h] = jnp.exp2(m_old - m_new) * acc_ref[h] + _dot(pexp.astype(BF16), vaug)

    n_pairs = (i + 2) // 2
    n_far = jnp.maximum(n_pairs - 2, 0)

    def far_body(jj, _):
        pair_step(jj, False)
        return 0

    def near_body(jj, _):
        pair_step(jj, True)
        return 0

    lax.fori_loop(0, n_far, far_body, 0)
    lax.fori_loop(n_far, n_pairs, near_body, 0)
    for p in range(N_PAIRS):
        outs = []
        for hh in range(2):
            acc = acc_ref[2 * p + hh]
            outs.append(acc / pltpu.roll(acc, HEAD_DIM, axis=1))
        o_ref[0, :, p * LANES:(p + 1) * LANES] = jnp.where(lo_half, outs[0], outs[1])


def _prompt_dsa(q, k, v, mask, bias, tq):
    b, t, dh = q.shape
    assert t % (2 * tq) == 0
    once = pl.Buffered(1)
    seq = pl.BlockSpec((1, t, dh), lambda bi, i: (bi, 0, 0), pipeline_mode=once)
    return pl.pallas_call(
        functools.partial(_dsa_kernel, tq=tq),
        grid=(b, t // tq),
        in_specs=[pl.BlockSpec((1, tq, dh), lambda bi, i: (bi, i, 0)), seq, seq,
                  pl.BlockSpec((1, tq, t), lambda bi, i: (bi, i, 0)),
                  pl.BlockSpec(bias.shape, lambda bi, i: (0, 0, 0, 0), pipeline_mode=once)],
        out_specs=pl.BlockSpec((1, tq, dh), lambda bi, i: (bi, i, 0)),
        out_shape=SDS((b, t, dh), F32),
        scratch_shapes=[pltpu.VMEM((N_HEADS, tq, 2 * tq), F32), pltpu.VMEM((tq, 2 * tq), F32),
                        pltpu.VMEM((N_HEADS, tq, LANES), F32), pltpu.VMEM((N_HEADS, tq, LANES), F32)],
        compiler_params=_params("arbitrary", "arbitrary"),
        name="prompt_dsa",
    )(q, k, v, mask, bias)


def _log_one_minus_beta(z):
    return -(jnp.maximum(z, 0.0) + jnp.log(1.0 + jnp.exp(-jnp.abs(z))))


def _sb_kernel(q_ref, k_ref, v_ref, o_ref, z_ref, tail_ref, c_ref, acc_ref, *, tq):
    i = pl.program_id(1)
    qs, lo_half = _pair_queries(q_ref, tq)
    c_ref[...] = jnp.zeros(c_ref.shape, F32)
    acc_ref[...] = jnp.zeros(acc_ref.shape, F32)
    row = lax.broadcasted_iota(I32, (tq, tq), 0)
    col = lax.broadcasted_iota(I32, (tq, tq), 1)
    not_before = jnp.where(row >= col, 1.0, 0.0).astype(BF16)
    not_before = jnp.concatenate([not_before, not_before], axis=0)
    earlier = col < row

    def block(j, diag):
        base = pl.multiple_of(j * tq, tq)
        for p in range(N_PAIRS):
            kp = k_ref[0, pl.ds(base, tq), p * LANES:(p + 1) * LANES]
            for hh in range(2):
                z_ref[2 * p + hh] = _dot_nt(qs[p][hh], kp)
        for h in range(N_HEADS):
            z = z_ref[h]
            sp = jnp.maximum(z, 0.0) + jnp.log2(1.0 + jnp.exp2(-jnp.abs(z)))
            if diag:
                sp = jnp.where(earlier, sp, 0.0)
            hi, lo = _split2(sp)
            tail_ref[h] = _dot(jnp.concatenate([hi, lo], axis=1), not_before)
        for p in range(N_PAIRS):
            vp = v_ref[0, pl.ds(base, tq), p * LANES:(p + 1) * LANES]
            pvs = []
            for hh in range(2):
                h = 2 * p + hh
                c_old = c_ref[h]
                a = jnp.concatenate(
                    [jnp.exp2(z_ref[h, :, n * LANES:(n + 1) * LANES] - tail_ref[h, :, n * LANES:(n + 1) * LANES]
                             - c_old) for n in range(tq // LANES)], axis=1)
                if diag:
                    a = jnp.where(earlier, a, 0.0)
                c_ref[h] = c_old + tail_ref[h, :, :1]
                pvs.append(_dot(a.astype(BF16), vp))
            sl = slice(p * LANES, (p + 1) * LANES)
            acc_ref[:, sl] = acc_ref[:, sl] + jnp.where(lo_half, pvs[0], pvs[1])

    block(i, True)

    def body(step, _):
        block(i - 1 - step, False)
        return 0

    lax.fori_loop(0, i, body, 0)
    o_ref[0] = acc_ref[...]


def _prompt_sb(q, k, v, tq):
    b, t, dh = q.shape
    seq = pl.BlockSpec((1, t, dh), lambda bi, i: (bi, 0, 0), pipeline_mode=pl.Buffered(1))
    return pl.pallas_call(
        functools.partial(_sb_kernel, tq=tq),
        grid=(b, t // tq),
        in_specs=[pl.BlockSpec((1, tq, dh), lambda bi, i: (bi, i, 0)), seq, seq],
        out_specs=pl.BlockSpec((1, tq, dh), lambda bi, i: (bi, i, 0)),
        out_shape=SDS((b, t, dh), F32),
        scratch_shapes=[pltpu.VMEM((N_HEADS, tq, tq), F32), pltpu.VMEM((N_HEADS, tq, tq), F32),
                        pltpu.VMEM((N_HEADS, tq, LANES), F32),
                        pltpu.VMEM((tq, dh), F32)],
        compiler_params=_params("arbitrary", "arbitrary"),
        name="prompt_sb",
    )(q, k, v)


def _merge_kernel(ya_ref, yb_ref, gate_ref, x_ref, g1_ref, n_ref, wa_ref, wb_ref, wo_ref, o_ref):
    d = x_ref.shape[1]
    ma = _dot(ya_ref[...].astype(BF16), wa_ref[...])
    mb = _dot(yb_ref[...].astype(BF16), wb_ref[...])
    merged = gate_ref[:, :d].astype(F32) * ma + gate_ref[:, d:].astype(F32) * mb
    mix = _dot(merged.astype(BF16), wo_ref[...])
    o_ref[...] = x_ref[...] + g1_ref[0] * _rms(mix, n_ref[...])


def _mod_spec(mod, tm, rows_per_mod):
    d = mod.shape[2]
    if mod.shape[1] == 1:
        return pl.BlockSpec((1, 1, d), lambda i: ((i * tm) // rows_per_mod, 0, 0))
    return pl.BlockSpec((1, tm, d), lambda i: (0, i, 0))


def _merge(ya, yb, gate, x, g1, norm, wa, wb, wo, tm, rows_per_mod):
    m, d = x.shape
    row = lambda a: pl.BlockSpec((tm, a.shape[1]), lambda i: (i, 0))
    full = lambda a: pl.BlockSpec(a.shape, lambda i: (0, 0))
    return pl.pallas_call(
        _merge_kernel,
        grid=(m // tm,),
        in_specs=[row(ya), row(yb), row(gate), row(x), _mod_spec(g1, tm, rows_per_mod),
                  full(norm), full(wa), full(wb), full(wo)],
        out_specs=pl.BlockSpec((tm, d), lambda i: (i, 0)),
        out_shape=SDS((m, d), F32),
        compiler_params=_params("arbitrary"),
        name="merge",
    )(ya, yb, gate, x, g1, norm, wa, wb, wo)


def _ffn_kernel(x_ref, sc_ref, sh_ref, g2_ref, npre_ref, npost_ref, wg_ref, wu_ref, wd_ref, o_ref, *, chunk):
    x = x_ref[...]
    h = (_rms(x, npre_ref[...]) * (1.0 + sc_ref[0]) + sh_ref[0]).astype(BF16)
    f = jnp.zeros(x.shape, F32)
    for c0 in range(0, wg_ref.shape[1], chunk):
        gate = _dot(h, wg_ref[:, c0:c0 + chunk])
        up = _dot(h, wu_ref[:, c0:c0 + chunk])
        act = gate / (1.0 + jnp.exp(-gate)) * up
        f = f + _dot(act.astype(BF16), wd_ref[c0:c0 + chunk, :])
    o_ref[...] = x + g2_ref[0] * _rms(f, npost_ref[...])


def _ffn(x, sc, sh, g2, npre, npost, wg, wu, wd, tm, rows_per_mod):
    m, d = x.shape
    full = lambda a: pl.BlockSpec(a.shape, lambda i: (0, 0))
    mod = lambda a: _mod_spec(a, tm, rows_per_mod)
    return pl.pallas_call(
        functools.partial(_ffn_kernel, chunk=256),
        grid=(m // tm,),
        in_specs=[pl.BlockSpec((tm, d), lambda i: (i, 0)), mod(sc), mod(sh), mod(g2),
                  full(npre), full(npost), full(wg), full(wu), full(wd)],
        out_specs=pl.BlockSpec((tm, d), lambda i: (i, 0)),
        out_shape=SDS((m, d), F32),
        compiler_params=_params("arbitrary"),
        name="ffn",
    )(x, sc, sh, g2, npre, npost, wg, wu, wd)


def _sample_score_kernel(pt_ref, q_ref, w_ref, knew_ref, *rest, pages):
    page_refs = rest[:pages]
    past_ref, new_ref = rest[pages:]
    q_hi, q_lo = _split2(q_ref[0])
    w = w_ref[0]
    t = q_ref.shape[1] // N_IDX_HEADS

    def score(keys):
        k_hi, k_lo = _split2(keys)
        r = jnp.maximum(_dot3_nt(q_hi, q_lo, k_hi, k_lo), 0.0) * w
        s = r[0:t]
        for h in range(1, N_IDX_HEADS):
            s = s + r[h * t:(h + 1) * t]
        return s

    for n in range(pages):
        past_ref[0, :, n * LANES:(n + 1) * LANES] = score(page_refs[n][...])
    new_ref[0] = score(knew_ref[0])


def _sample_scores(page_table, q_rows, w_rows, k_new, cache_k_idx, pages):
    nb, n_pages = page_table.shape
    page = cache_k_idx.shape[2]
    assert page == LANES
    rows = q_rows.shape[1]
    t = rows // N_IDX_HEADS

    def page_spec(n):
        return pl.BlockSpec((None, None, page, IDX_DIM), lambda b, g, pt: (0, pt[b, g * pages + n], 0, 0))

    grid_spec = pltpu.PrefetchScalarGridSpec(
        num_scalar_prefetch=1,
        grid=(nb, n_pages // pages),
        in_specs=[pl.BlockSpec((1, rows, IDX_DIM), lambda b, g, pt: (b, 0, 0)),
                  pl.BlockSpec((1, rows, LANES), lambda b, g, pt: (b, 0, 0)),
                  pl.BlockSpec((1, page, IDX_DIM), lambda b, g, pt: (b, 0, 0))]
                 + [page_spec(n) for n in range(pages)],
        out_specs=[pl.BlockSpec((1, t, pages * page), lambda b, g, pt: (b, 0, g)),
                   pl.BlockSpec((1, t, page), lambda b, g, pt: (b, 0, 0))],
    )
    return pl.pallas_call(
        functools.partial(_sample_score_kernel, pages=pages),
        grid_spec=grid_spec,
        out_shape=[SDS((nb, t, n_pages * page), F32), SDS((nb, t, page), F32)],
        compiler_params=_params("arbitrary", "arbitrary"),
        name="sample_scores",
    )(page_table, q_rows, w_rows, k_new, *([cache_k_idx] * pages))


def _sample_select_kernel(past_ref, new_ref, mask_ref, score_ref, thr_ref, need_ref, *, t, topk):
    rows, past = past_ref.shape
    score_ref[:, :past] = past_ref[...]
    q_t = lax.broadcasted_iota(I32, (rows, LANES), 0) % t
    col = lax.broadcasted_iota(I32, (rows, LANES), 1)
    score_ref[:, past:] = jnp.where(col <= q_t, new_ref[...], -jnp.inf)

    def emit(j, sel):
        mask_ref[:, pl.ds(pl.multiple_of(j * LANES, LANES), LANES)] = (1.0 - sel) * NEG

    width = score_ref.shape[1]
    gw = _group_width(width)
    _emit_selection(score_ref, thr_ref, need_ref, width // gw, gw, width // LANES, LANES, topk, emit)


def _sample_select(s_past, s_new, t, topk, tr):
    rows, past = s_past.shape
    width = past + LANES
    return pl.pallas_call(
        functools.partial(_sample_select_kernel, t=t, topk=float(topk)),
        grid=(rows // tr,),
        in_specs=[pl.BlockSpec((tr, past), lambda i: (i, 0)), pl.BlockSpec((tr, LANES), lambda i: (i, 0))],
        out_specs=pl.BlockSpec((tr, width), lambda i: (i, 0)),
        out_shape=SDS((rows, width), F32),
        scratch_shapes=[pltpu.VMEM((tr, width), F32), pltpu.VMEM((tr, 1), F32), pltpu.VMEM((tr, 1), F32)],
        compiler_params=_params("arbitrary"),
        name="sample_select",
    )(s_past, s_new)


def _sample_bias_kernel(rb_ref, o_ref, *, t, page):
    key = lax.broadcasted_iota(I32, (page, N_HEADS * t), 0)
    col = lax.broadcasted_iota(I32, (page, N_HEADS * t), 1)
    q_t = col % t
    head = col // t
    for tile in range(2):
        bucket = _t5_bucket(q_t - key + tile * page)
        val = jnp.zeros(bucket.shape, F32)
        for h in range(N_HEADS):
            val = jnp.where(head == h, _bias_lookup(rb_ref, bucket, h), val)
        o_ref[tile] = val
    far = jnp.zeros(bucket.shape, F32)
    for h in range(N_HEADS):
        far = jnp.where(head == h, rb_ref[N_BUCKETS - 1, h], far)
    o_ref[2] = far


def _sample_bias(rel_bias, t, page):
    assert page >= MAX_DISTANCE
    return pl.pallas_call(
        functools.partial(_sample_bias_kernel, t=t, page=page),
        in_specs=[pl.BlockSpec(memory_space=pltpu.SMEM)],
        out_shape=SDS((3, page, N_HEADS * t), F32),
        name="sample_bias",
    )(rel_bias)


def _to_column(row_vec, n):
    eye = lax.broadcasted_iota(I32, (n, n), 0) == lax.broadcasted_iota(I32, (n, n), 1)
    return jnp.sum(jnp.where(eye, row_vec, 0.0), axis=1, keepdims=True)


def _sample_attn_kernel(pt_ref, qa_ref, qb_ref, knew_a_ref, vnew_a_ref, knew_b_ref, vnew_b_ref,
                        mnew_ref, mask_ref, bias_ref, *rest, pages, t):
    ka_refs = rest[0 * pages:1 * pages]
    va_refs = rest[1 * pages:2 * pages]
    kb_refs = rest[2 * pages:3 * pages]
    vb_refs = rest[3 * pages:4 * pages]
    ya_ref, yb_ref, m_ref, l_ref, acca_ref, c_ref, accb_ref = rest[4 * pages:]
    g = pl.program_id(1)
    n_groups = pl.num_programs(1)
    page = LANES
    cols = N_HEADS * t
    qa = qa_ref[0]
    qb = qb_ref[0]
    srow = lax.broadcasted_iota(I32, (page, page), 0)
    scol = lax.broadcasted_iota(I32, (page, page), 1)
    later = jnp.where(scol > srow, 1.0, 0.0).astype(BF16)

    def dsa_logits(k, bias):
        return _dot(k.astype(BF16), qa) + bias

    def dsa_update(blocks):
        m_old = m_ref[...]
        m_new = m_old
        for logits, valid, _ in blocks:
            m_new = jnp.maximum(m_new, jnp.max(jnp.where(valid, logits, NEG), axis=0, keepdims=True))
        alpha = jnp.exp(m_old - m_new)
        l_new = alpha * l_ref[...]
        contrib = jnp.zeros(acca_ref.shape, F32)
        for logits, valid, v in blocks:
            pexp = jnp.where(valid, jnp.exp(logits - m_new), 0.0)
            l_new = l_new + jnp.sum(pexp, axis=0, keepdims=True)
            contrib = contrib + _dot(pexp.T.astype(BF16), v.astype(BF16))
        m_ref[...] = m_new
        l_ref[...] = l_new
        acca_ref[...] = _to_column(alpha, cols) * acca_ref[...] + contrib

    def sb_block(k, v, earlier):
        z = _dot(k.astype(BF16), qb)
        lsm = _log_one_minus_beta(z)
        if earlier is not None:
            lsm = jnp.where(earlier, lsm, 0.0)
        hi, lo = _split2(lsm)
        between = _dot(later, hi) + _dot(later, lo)
        a = jnp.exp(z + lsm + between + c_ref[...])
        if earlier is not None:
            a = jnp.where(earlier, a, 0.0)
        c_ref[...] = c_ref[...] + between[:1, :] + lsm[:1, :]
        accb_ref[...] = accb_ref[...] + _dot(a.T.astype(BF16), v.astype(BF16))

    @pl.when(g == 0)
    def _():
        m_ref[...] = jnp.full(m_ref.shape, NEG, F32)
        l_ref[...] = jnp.zeros(l_ref.shape, F32)
        acca_ref[...] = jnp.zeros(acca_ref.shape, F32)
        c_ref[...] = jnp.zeros(c_ref.shape, F32)
        accb_ref[...] = jnp.zeros(accb_ref.shape, F32)
        key = lax.broadcasted_iota(I32, (page, cols), 0)
        q_t = lax.broadcasted_iota(I32, (page, cols), 1) % t
        valid = mnew_ref[0].astype(I32) != 0
        dsa_update([(dsa_logits(knew_a_ref[0], bias_ref[0]), valid, vnew_a_ref[0])])
        sb_block(knew_b_ref[0], vnew_b_ref[0], key < q_t)

    blocks = []
    for n in range(pages):
        valid = mask_ref[0, (pages - 1 - n) * page:(pages - n) * page, :].astype(I32) != 0
        last_page = jnp.logical_and(g == 0, n == 0)
        bias = jnp.where(last_page, bias_ref[1], bias_ref[2])
        blocks.append((dsa_logits(ka_refs[n][...], bias), valid, va_refs[n][...]))
    dsa_update(blocks)
    for n in range(pages):
        sb_block(kb_refs[n][...], vb_refs[n][...], None)

    @pl.when(g == n_groups - 1)
    def _():
        head_of_col = lax.broadcasted_iota(I32, (t, N_HEADS * HEAD_DIM), 1) // HEAD_DIM
        norm = acca_ref[...] / _to_column(l_ref[...], cols)
        ya = jnp.zeros((t, N_HEADS * HEAD_DIM), F32)
        yb = jnp.zeros((t, N_HEADS * HEAD_DIM), F32)
        for h in range(N_HEADS):
            ya = ya + jnp.where(head_of_col == h, norm[h * t:(h + 1) * t], 0.0)
            yb = yb + jnp.where(head_of_col == h, accb_ref[h * t:(h + 1) * t, :], 0.0)
        ya_ref[0] = ya
        yb_ref[0] = yb


def _sample_attn(page_table, qa_bd, qb_bd, knew_a, vnew_a, knew_b, vnew_b, mask_new, mask_past, bias,
                 cache_k_a, cache_v_a, cache_k_b, cache_v_b, pages, t):
    nb, n_pages = page_table.shape
    page = LANES
    dh = N_HEADS * HEAD_DIM
    cols = N_HEADS * t
    n_groups = n_pages // pages

    def page_spec(n):
        return pl.BlockSpec((None, None, page, dh),
                            lambda b, g, pt: (0, pt[b, n_pages - 1 - (g * pages + n)], 0, 0))

    per_seq = lambda shape: pl.BlockSpec((1,) + shape, lambda b, g, pt: (b, 0, 0))
    grid_spec = pltpu.PrefetchScalarGridSpec(
        num_scalar_prefetch=1,
        grid=(nb, n_groups),
        in_specs=[per_seq((dh, cols)), per_seq((dh, cols)),
                  per_seq((page, dh)), per_seq((page, dh)), per_seq((page, dh)), per_seq((page, dh)),
                  per_seq((page, cols)),
                  pl.BlockSpec((1, pages * page, cols), lambda b, g, pt: (b, n_groups - 1 - g, 0)),
                  pl.BlockSpec(bias.shape, lambda b, g, pt: (0, 0, 0))]
                 + [page_spec(n) for n in range(pages)] * 4,
        out_specs=[per_seq((t, dh)), per_seq((t, dh))],
        scratch_shapes=[pltpu.VMEM((1, cols), F32), pltpu.VMEM((1, cols), F32), pltpu.VMEM((cols, dh), F32),
                        pltpu.VMEM((1, cols), F32), pltpu.VMEM((cols, dh), F32)],
    )
    flat = lambda c: c.reshape(c.shape[0], c.shape[1], c.shape[2], dh)
    return pl.pallas_call(
        functools.partial(_sample_attn_kernel, pages=pages, t=t),
        grid_spec=grid_spec,
        out_shape=[SDS((nb, t, dh), F32), SDS((nb, t, dh), F32)],
        compiler_params=_params("arbitrary", "arbitrary"),
        name="sample_attn",
    )(page_table, qa_bd, qb_bd, knew_a, vnew_a, knew_b, vnew_b, mask_new, mask_past, bias,
      *([flat(cache_k_a)] * pages), *([flat(cache_v_a)] * pages),
      *([flat(cache_k_b)] * pages), *([flat(cache_v_b)] * pages))


def _sample_bias2_kernel(rb_ref, o_ref, *, t, page):
    rows, width = N_HEADS * t, page * N_HEADS
    row = lax.broadcasted_iota(I32, (rows, width), 0)
    lane = lax.broadcasted_iota(I32, (rows, width), 1)
    q_t, head = row % t, row // t
    key = lane // N_HEADS
    own = (lane % N_HEADS) == head
    for tile in range(3):
        bucket = _t5_bucket(q_t - key + tile * page)
        val = jnp.zeros((rows, width), F32)
        for h in range(N_HEADS):
            val = jnp.where(head == h, _bias_lookup(rb_ref, bucket, h), val)
        o_ref[tile] = jnp.where(own, val, NEG)


def _sample_bias2(rel_bias, t, page):
    assert page >= MAX_DISTANCE
    return pl.pallas_call(
        functools.partial(_sample_bias2_kernel, t=t, page=page),
        in_specs=[pl.BlockSpec(memory_space=pltpu.SMEM)],
        out_shape=SDS((3, N_HEADS * t, page * N_HEADS), F32),
        name="sample_bias",
    )(rel_bias)


def _sample_attn2_kernel(pt_ref, qa_ref, qb_ref, knew_a_ref, vnew_a_ref, knew_b_ref, vnew_b_ref,
                         mnew_ref, mask_ref, bias_ref, *rest, pages, t):
    ka_refs = rest[0 * pages:1 * pages]
    va_refs = rest[1 * pages:2 * pages]
    kb_refs = rest[2 * pages:3 * pages]
    vb_refs = rest[3 * pages:4 * pages]
    ya_ref, yb_ref, m_ref, l_ref, acca_ref, c_ref, accb_ref = rest[4 * pages:]
    g = pl.program_id(1)
    n_groups = pl.num_programs(1)
    rows = N_HEADS * t
    width = LANES * N_HEADS
    qa = qa_ref[0]
    qb = qb_ref[0]
    row = lax.broadcasted_iota(I32, (rows, width), 0)
    lane = lax.broadcasted_iota(I32, (rows, width), 1)
    own = (lane % N_HEADS) == (row // t)
    own_neg = jnp.where(own, 0.0, NEG)
    own_one = jnp.where(own, 1.0, 0.0)
    srow = lax.broadcasted_iota(I32, (LANES, LANES), 0)
    scol = lax.broadcasted_iota(I32, (LANES, LANES), 1)
    not_before = jnp.where(srow >= scol, 1.0, 0.0).astype(BF16)
    not_before = jnp.concatenate([not_before, not_before], axis=0)

    def dsa_logits(k, bias, madd):
        return _dot_nt(qa, k.astype(BF16)) + bias + jnp.concatenate([madd] * N_HEADS, axis=0)

    def dsa_update(blocks):
        m_old = m_ref[...]
        m_new = m_old
        for logits, _ in blocks:
            m_new = jnp.maximum(m_new, jnp.max(logits, axis=1, keepdims=True))
        alpha = jnp.exp(m_old - m_new)
        l_new = alpha * l_ref[...]
        contrib = jnp.zeros(acca_ref.shape, F32)
        for logits, v in blocks:
            pexp = jnp.exp(logits - m_new)
            l_new = l_new + jnp.sum(pexp, axis=1, keepdims=True)
            contrib = contrib + _dot(pexp.astype(BF16), v.astype(BF16))
        m_ref[...] = m_new
        l_ref[...] = l_new
        acca_ref[...] = alpha * acca_ref[...] + contrib

    def sb_update(blocks):
        n_lane_blocks = width // LANES
        zs = [_dot_nt(qb, k.astype(BF16)) for k, _, _, _ in blocks]
        inners = []
        for z, (_, _, live_one, _) in zip(zs, blocks):
            sp = (jnp.maximum(z, 0.0) + jnp.log(1.0 + jnp.exp(-jnp.abs(z)))) * live_one
            hi, lo = _split2(sp)
            inners.append([_dot(jnp.concatenate([hi[:, n * LANES:(n + 1) * LANES],
                                                 lo[:, n * LANES:(n + 1) * LANES]], axis=1), not_before)
                           for n in range(n_lane_blocks)])
        carry = c_ref[...]
        contrib = jnp.zeros(accb_ref.shape, F32)
        for z, inner, (_, v, _, live_neg) in zip(zs, inners, blocks):
            tails = [None] * n_lane_blocks
            for n in reversed(range(n_lane_blocks)):
                tails[n] = inner[n] + carry
                carry = carry + inner[n][:, :1]
            a = jnp.exp(z - jnp.concatenate(tails, axis=1) + live_neg)
            contrib = contrib + _dot(a.astype(BF16), v.astype(BF16))
        c_ref[...] = carry
        accb_ref[...] = accb_ref[...] + contrib

    @pl.when(g == 0)
    def _():
        m_ref[...] = jnp.full(m_ref.shape, NEG, F32)
        l_ref[...] = jnp.zeros(l_ref.shape, F32)
        acca_ref[...] = jnp.zeros(acca_ref.shape, F32)
        c_ref[...] = jnp.zeros(c_ref.shape, F32)
        accb_ref[...] = jnp.zeros(accb_ref.shape, F32)
        earlier = jnp.logical_and(own, lane // N_HEADS < row % t)
        dsa_update([(dsa_logits(knew_a_ref[0], bias_ref[0], mnew_ref[0]), vnew_a_ref[0])])
        sb_update([(knew_b_ref[0], vnew_b_ref[0], jnp.where(earlier, 1.0, 0.0), jnp.where(earlier, 0.0, NEG))])

    blocks = []
    for n in range(pages):
        madd = mask_ref[0, :, (pages - 1 - n) * width:(pages - n) * width]
        bias = bias_ref[2] if n else jnp.where(g == 0, bias_ref[1], bias_ref[2])
        blocks.append((dsa_logits(ka_refs[n][...], bias, madd), va_refs[n][...]))
    dsa_update(blocks)
    sb_update([(kb_refs[n][...], vb_refs[n][...], own_one, own_neg) for n in range(pages)])

    @pl.when(g == n_groups - 1)
    def _():
        ya_ref[0] = acca_ref[...] / l_ref[...]
        yb_ref[0] = accb_ref[...]


def _sample_attn2(page_table, qa_rows, qb_rows, knew_a, vnew_a, knew_b, vnew_b, madd_new, madd_past, bias,
                  cache_k_a, cache_v_a, cache_k_b, cache_v_b, pages, t):
    nb, n_pages = page_table.shape
    rows = N_HEADS * t
    width = LANES * N_HEADS
    n_groups = n_pages // pages

    def page_spec(n):
        return pl.BlockSpec((None, None, width, HEAD_DIM),
                            lambda b, g, pt: (0, pt[b, n_pages - 1 - (g * pages + n)], 0, 0))

    per_seq = lambda shape: pl.BlockSpec((1,) + shape, lambda b, g, pt: (b, 0, 0))
    grid_spec = pltpu.PrefetchScalarGridSpec(
        num_scalar_prefetch=1,
        grid=(nb, n_groups),
        in_specs=[per_seq((rows, HEAD_DIM)), per_seq((rows, HEAD_DIM)),
                  per_seq((width, HEAD_DIM)), per_seq((width, HEAD_DIM)),
                  per_seq((width, HEAD_DIM)), per_seq((width, HEAD_DIM)),
                  per_seq((t, width)),
                  pl.BlockSpec((1, t, pages * width), lambda b, g, pt: (b, 0, n_groups - 1 - g)),
                  pl.BlockSpec(bias.shape, lambda b, g, pt: (0, 0, 0))]
                 + [page_spec(n) for n in range(pages)] * 4,
        out_specs=[per_seq((rows, HEAD_DIM)), per_seq((rows, HEAD_DIM))],
        scratch_shapes=[pltpu.VMEM((rows, 1), F32), pltpu.VMEM((rows, 1), F32), pltpu.VMEM((rows, HEAD_DIM), F32),
                        pltpu.VMEM((rows, 1), F32), pltpu.VMEM((rows, HEAD_DIM), F32)],
    )
    as_rows = lambda c: c.reshape(c.shape[0], c.shape[1], width, HEAD_DIM)
    return pl.pallas_call(
        functools.partial(_sample_attn2_kernel, pages=pages, t=t),
        grid_spec=grid_spec,
        out_shape=[SDS((nb, rows, HEAD_DIM), F32), SDS((nb, rows, HEAD_DIM), F32)],
        compiler_params=_params("arbitrary", "arbitrary"),
        name="sample_attn",
    )(page_table, qa_rows, qb_rows, knew_a, vnew_a, knew_b, vnew_b, madd_new, madd_past, bias,
      *([as_rows(cache_k_a)] * pages), *([as_rows(cache_v_a)] * pages),
      *([as_rows(cache_k_b)] * pages), *([as_rows(cache_v_b)] * pages))


def _largest_divisor(n, cap):
    d = min(n, cap)
    while n % d:
        d -= 1
    return d


def _block_diag_queries(q, nb, t):
    q4 = q.astype(F32).reshape(nb, t, N_HEADS, HEAD_DIM)
    eye = jnp.eye(N_HEADS, dtype=F32)
    bd = jnp.einsum('bthd,hg->bhdgt', q4, eye)
    return bd.reshape(nb, N_HEADS * HEAD_DIM, N_HEADS * t).astype(BF16)


def kernel(x_prompt, x_sample, c_prompt, c_sample, cache_k_a, cache_v_a, cache_k_idx, cache_k_b, cache_v_b,
           page_table, rel_bias, w_ada, b_ada, norm_pre_mix, norm_post_mix, norm_pre_ffn, norm_post_ffn,
           w_in, w_br_a, w_br_b, w_out, w_ffn_gate, w_ffn_up, w_ffn_down):
    depth = w_in.shape[0]
    assert depth == 1
    nb_p, t_p, d = x_prompt.shape
    nb_s, t_s, _ = x_sample.shape
    n_pages = page_table.shape[1]
    page = cache_k_a.shape[2]
    past = n_pages * page
    dh = N_HEADS * HEAD_DIM
    nq = N_IDX_HEADS * IDX_DIM
    topk_p = min(INDEX_TOPK, t_p // 4)
    topk_s = min(INDEX_TOPK, (past + t_s) // 4)
    tq = 256
    assert t_p % tq == 0 and topk_p <= tq and t_s * N_HEADS <= LANES and page == LANES

    w = w_in[0]
    o = np.cumsum([0, dh, dh, dh, nq, IDX_DIM, N_IDX_HEADS, dh, dh, dh, d, d])
    col = lambda n: w[:, o[n]:o[n + 1]]
    w_main = jnp.concatenate([col(0), col(1), col(2), col(6), col(7), col(8)], axis=1).astype(BF16)
    w_gate = jnp.concatenate([col(9), col(10)], axis=1).astype(BF16)
    w_idx = jnp.concatenate([col(3), col(4), col(4), col(5),
                             jnp.zeros((d, LANES - N_IDX_HEADS), F32)], axis=1)
    w_idx_hi = w_idx.astype(BF16)
    w_idx_lo = (w_idx - w_idx_hi.astype(F32)).astype(BF16)
    wa, wb, wo = w_br_a[0].astype(BF16), w_br_b[0].astype(BF16), w_out[0].astype(BF16)
    wg, wu, wd = w_ffn_gate[0].astype(BF16), w_ffn_up[0].astype(BF16), w_ffn_down[0].astype(BF16)
    n_pre_mix, n_post_mix = norm_pre_mix[0][None], norm_post_mix[0][None]
    n_pre_ffn, n_post_ffn = norm_pre_ffn[0][None], norm_post_ffn[0][None]

    n_c = nb_p + nb_s
    c_all = jnp.concatenate([c_prompt, c_sample, jnp.zeros((-n_c % 8, d), F32)], axis=0)
    ada = _ada(c_all, w_ada[0], b_ada[0][None])
    mods_p = [m[:, None, :] for m in jnp.split(ada[:nb_p], 6, axis=-1)]
    mods_s = [jnp.repeat(m, t_s, axis=0)[None] for m in jnp.split(ada[nb_p:n_c], 6, axis=-1)]

    def dense_front(x2, mods, tm, rows_per_mod, q_scale):
        return _inproj(x2, mods[1], mods[0], n_pre_mix, w_main, w_gate, w_idx_hi, w_idx_lo, tm, rows_per_mod,
                       q_scale)

    def dense_back(ya, yb, gate, x2, mods, tm, rows_per_mod):
        x1 = _merge(ya, yb, gate, x2, mods[2], n_post_mix, wa, wb, wo, tm, rows_per_mod)
        return _ffn(x1, mods[4], mods[3], mods[5], n_pre_ffn, n_post_ffn, wg, wu, wd, tm, rows_per_mod)

    m_p = nb_p * t_p
    xp2 = x_prompt.reshape(m_p, d)
    tm_p = 256
    (qa, ka, va, qb, kb, vb, kab, vab, kbb, vbb, gate, iq, ik, wi, kidx, wit) = dense_front(
        xp2, mods_p, tm_p, t_p, LOG2E * HEAD_DIM ** -0.5)
    seq3 = lambda a: a.reshape(nb_p, t_p, a.shape[-1])
    mask = _prompt_index(seq3(iq), wit, seq3(ik), tq, topk_p)
    ya = _prompt_dsa(seq3(qa), seq3(kab), seq3(vab), mask, _bias_tiles(rel_bias, tq), tq)
    yb = _prompt_sb(seq3(qb), seq3(kbb), seq3(vbb), tq)
    y_p = dense_back(ya.reshape(m_p, dh), yb.reshape(m_p, dh), gate, xp2, mods_p, tm_p, t_p)
    heads_p = lambda a: a.reshape(depth, nb_p, t_p, N_HEADS, HEAD_DIM)
    outs_p = (heads_p(ka), heads_p(va), kidx.reshape(depth, nb_p, t_p, IDX_DIM), heads_p(kb), heads_p(vb))

    m_s = nb_s * t_s
    xs2 = x_sample.reshape(m_s, d)
    tm_s = _largest_divisor(m_s, 256)
    (qa, ka, va, qb, kb, vb, _, _, _, _, gate, iq, ik, wi, kidx, _) = dense_front(
        xs2, mods_s, tm_s, t_s, HEAD_DIM ** -0.5)
    pad_keys = lambda a: jnp.pad(a.reshape(nb_s, t_s, a.shape[-1]), ((0, 0), (0, page - t_s), (0, 0)))
    q_rows = iq.reshape(nb_s, t_s, N_IDX_HEADS, IDX_DIM).transpose(0, 2, 1, 3).reshape(nb_s, N_IDX_HEADS * t_s, IDX_DIM)
    w_rows = wi.reshape(nb_s, t_s, LANES)[:, :, :N_IDX_HEADS].transpose(0, 2, 1).reshape(nb_s, N_IDX_HEADS * t_s, 1)
    w_rows = jnp.broadcast_to(w_rows, (nb_s, N_IDX_HEADS * t_s, LANES))
    pages_idx = _largest_divisor(n_pages, 16)
    s_past, s_new = _sample_scores(page_table, q_rows, w_rows, pad_keys(kidx), cache_k_idx, pages_idx)
    madd = _sample_select(s_past.reshape(m_s, past), s_new.reshape(m_s, page), t_s, topk_s,
                          _largest_divisor(m_s, 128))
    madd = jnp.repeat(madd.reshape(nb_s, t_s, past + page), N_HEADS, axis=2)
    head_rows = lambda a: (a.reshape(nb_s, t_s, N_HEADS, HEAD_DIM).transpose(0, 2, 1, 3)
                           .reshape(nb_s, N_HEADS * t_s, HEAD_DIM))
    as_page = lambda a: jnp.pad(a.reshape(nb_s, t_s * N_HEADS, HEAD_DIM),
                                ((0, 0), (0, (page - t_s) * N_HEADS), (0, 0)))
    pages_attn = _largest_divisor(n_pages, 4)
    ya, yb = _sample_attn2(page_table, head_rows(qa), head_rows(qb),
                           as_page(ka), as_page(va), as_page(kb), as_page(vb),
                           madd[:, :, past * N_HEADS:], madd[:, :, :past * N_HEADS],
                           _sample_bias2(rel_bias, t_s, page),
                           cache_k_a, cache_v_a, cache_k_b, cache_v_b, pages_attn, t_s)
    token_rows = lambda y: (y.reshape(nb_s, N_HEADS, t_s, HEAD_DIM).transpose(0, 2, 1, 3).reshape(m_s, dh))
    y_s = dense_back(token_rows(ya), token_rows(yb), gate, xs2, mods_s, tm_s, t_s)
    heads_s = lambda a: a.reshape(depth, nb_s, t_s, N_HEADS, HEAD_DIM)
    outs_s = (heads_s(ka), heads_s(va), kidx.reshape(depth, nb_s, t_s, IDX_DIM), heads_s(kb), heads_s(vb))

    return (y_p.reshape(nb_p, t_p, d), y_s.reshape(nb_s, t_s, d)) + outs_p + outs_s
```

```python
import functools
import math

import numpy as np
import jax
import jax.numpy as jnp
from jax import lax
from jax.experimental import pallas as pl
from jax.experimental.pallas import tpu as pltpu

F32 = jnp.float32
BF16 = jnp.bfloat16
I32 = jnp.int32
SDS = jax.ShapeDtypeStruct

HEAD_DIM = 64
N_HEADS = 8
N_PAIRS = N_HEADS // 2
N_IDX_HEADS = 4
IDX_DIM = 64
INDEX_TOPK = 256
N_BUCKETS = 32
MAX_EXACT = N_BUCKETS // 2
MAX_DISTANCE = 128
EPS = 1e-6
LANES = 128
SUBLANES = 8
NEG = -1e30
LOG2E = 1.4426950408889634
VMEM_LIMIT = 56 * 1024 * 1024

_I32_MIN = np.int32(-2 ** 31)
_KEY_NEG_INF = np.int32(0x807FFFFF - 2 ** 32)


def _params(*sem):
    return pltpu.CompilerParams(dimension_semantics=sem, vmem_limit_bytes=VMEM_LIMIT)


def _dot(a, b):
    return jnp.dot(a, b, preferred_element_type=F32)


def _dot_nt(a, b):
    return lax.dot_general(a, b, (((1,), (1,)), ((), ())), preferred_element_type=F32)


def _split2(x):
    hi = x.astype(BF16)
    lo = (x - hi.astype(F32)).astype(BF16)
    return hi, lo


def _dot3(a_hi, a_lo, b_hi, b_lo):
    return _dot(a_hi, b_hi) + _dot(a_lo, b_hi) + _dot(a_hi, b_lo)


def _dot3_nt(a_hi, a_lo, b_hi, b_lo):
    return _dot_nt(a_hi, b_hi) + _dot_nt(a_lo, b_hi) + _dot_nt(a_hi, b_lo)


def _rms(x, g):
    var = jnp.mean(x * x, axis=-1, keepdims=True)
    return x * lax.rsqrt(var + EPS) * g


def _ada_kernel(c_ref, w_ref, b_ref, o_ref):
    c_hi, c_lo = _split2(c_ref[...])
    w_hi, w_lo = _split2(w_ref[...])
    o_ref[...] = _dot3(c_hi, c_lo, w_hi, w_lo) + b_ref[...]


def _ada(c, w, b):
    rows, d = c.shape
    n = w.shape[1]
    tn = 1024
    return pl.pallas_call(
        _ada_kernel,
        grid=(n // tn,),
        in_specs=[pl.BlockSpec((rows, d), lambda j: (0, 0)),
                  pl.BlockSpec((d, tn), lambda j: (0, j)),
                  pl.BlockSpec((1, tn), lambda j: (0, j))],
        out_specs=pl.BlockSpec((rows, tn), lambda j: (0, j)),
        out_shape=SDS((rows, n), F32),
        compiler_params=_params("arbitrary"),
        name="ada",
    )(c, w, b)


def _inproj_kernel(x_ref, sc_ref, sh_ref, g_ref, wm_ref, wg_ref, wih_ref, wil_ref,
                   qa_ref, ka_ref, va_ref, qb_ref, kb_ref, vb_ref,
                   kab_ref, vab_ref, kbb_ref, vbb_ref,
                   gate_ref, iq_ref, ik_ref, wi_ref, kidx_ref, wit_ref, *, q_scale):
    dh = N_HEADS * HEAD_DIM
    h = _rms(x_ref[...], g_ref[...]) * (1.0 + sc_ref[0]) + sh_ref[0]
    h_hi, h_lo = _split2(h)

    def seg(n):
        return _dot(h_hi, wm_ref[:, n * dh:(n + 1) * dh])

    qa_ref[...] = (seg(0) * q_scale).astype(BF16)
    ka = seg(1)
    ka_ref[...] = ka
    kab_ref[...] = ka.astype(BF16)
    va = seg(2)
    va_ref[...] = va
    vab_ref[...] = va.astype(BF16)
    qb_ref[...] = (seg(3) * q_scale).astype(BF16)
    kb = seg(4)
    kb_ref[...] = kb
    kbb_ref[...] = kb.astype(BF16)
    vb = seg(5)
    vb_ref[...] = vb
    vbb_ref[...] = vb.astype(BF16)

    d = x_ref.shape[1]
    for n in range(2):
        g = _dot(h_hi, wg_ref[:, n * d:(n + 1) * d])
        gate_ref[:, n * d:(n + 1) * d] = (1.0 / (1.0 + jnp.exp(-g))).astype(BF16)

    nq = N_IDX_HEADS * IDX_DIM
    idx = _dot3(h_hi, h_lo, wih_ref[...], wil_ref[...])
    iq_ref[...] = idx[:, :nq] * IDX_DIM ** -0.5
    ik_ref[...] = idx[:, nq:nq + LANES]
    kidx_ref[...] = idx[:, nq:nq + IDX_DIM]
    wi = idx[:, nq + LANES:nq + 2 * LANES] * N_IDX_HEADS ** -0.5
    wi_ref[...] = wi
    wit_ref[...] = wi.T[:SUBLANES, :]


def _inproj(x, sc, sh, g, wm, wg, wih, wil, tm, rows_per_mod, q_scale):
    m, d = x.shape
    dh = N_HEADS * HEAD_DIM
    r = sc.shape[1]
    if r == 1:
        mod_map = lambda i: ((i * tm) // rows_per_mod, 0, 0)
    else:
        mod_map = lambda i: (0, i, 0)
    row = lambda w: pl.BlockSpec((tm, w), lambda i: (i, 0))
    full = lambda a: pl.BlockSpec(a.shape, lambda i: (0, 0))
    nq = N_IDX_HEADS * IDX_DIM
    outs = [(dh, BF16), (dh, F32), (dh, F32), (dh, BF16), (dh, F32), (dh, F32),
            (dh, BF16), (dh, BF16), (dh, BF16), (dh, BF16),
            (2 * d, BF16), (nq, F32), (LANES, F32), (LANES, F32), (IDX_DIM, F32)]
    return pl.pallas_call(
        functools.partial(_inproj_kernel, q_scale=q_scale),
        grid=(m // tm,),
        in_specs=[row(d), pl.BlockSpec((1, r, d), mod_map), pl.BlockSpec((1, r, d), mod_map),
                  full(g), full(wm), full(wg), full(wih), full(wil)],
        out_specs=[row(w) for w, _ in outs] + [pl.BlockSpec((SUBLANES, tm), lambda i: (0, i))],
        out_shape=[SDS((m, w), dt) for w, dt in outs] + [SDS((SUBLANES, m), F32)],
        compiler_params=_params("arbitrary"),
        name="inproj",
    )(x, sc, sh, g, wm, wg, wih, wil)


ROW_CHUNK = 64
MAX_GROUPS = 256


def _key_to_float(key):
    bits = jnp.where(key < 0, key ^ jnp.int32(0x7FFFFFFF), key)
    f = lax.bitcast_convert_type(bits, F32)
    return jnp.where(key < _KEY_NEG_INF, -jnp.inf, f)


def _float_to_key(f):
    bits = lax.bitcast_convert_type(f, I32)
    return jnp.where(bits < 0, bits ^ jnp.int32(0x7FFFFFFF), bits)


def _group_width(width):
    gw = min(width, 8 * LANES)
    while width % gw:
        gw -= LANES
    return gw


def _count(score_ref, r0, ngroups, gw, cf, strict):
    cfb = jnp.broadcast_to(cf, (ROW_CHUNK, LANES))

    def body(g, acc):
        base = pl.multiple_of(g * gw, gw)
        for c in range(gw // LANES):
            s = score_ref[pl.ds(r0, ROW_CHUNK), pl.ds(base + c * LANES, LANES)]
            hit = (s > cfb) if strict else (s >= cfb)
            acc = acc + jnp.where(hit, 1.0, 0.0)
        return acc

    acc = lax.fori_loop(0, ngroups, body, jnp.zeros((ROW_CHUNK, LANES), F32))
    return jnp.sum(acc, axis=1, keepdims=True)


def _kth_largest(score_ref, r0, ngroups, gw, k):
    ninf = jnp.full((ROW_CHUNK, LANES), -jnp.inf, F32)

    def bounds(g, carry):
        base = pl.multiple_of(g * gw, gw)
        tops = list(carry)
        for c in range(gw // LANES):
            s = score_ref[pl.ds(r0, ROW_CHUNK), pl.ds(base + c * LANES, LANES)]
            tops[c % 2] = jnp.maximum(tops[c % 2], s)
        return tuple(tops)

    top_even, top_odd = lax.fori_loop(0, ngroups, bounds, (ninf, ninf))
    upper = jnp.max(jnp.maximum(top_even, top_odd), axis=1, keepdims=True)
    lower = jnp.min(jnp.minimum(top_even, top_odd), axis=1, keepdims=True)
    lo0 = _float_to_key(lower)
    hi0 = _float_to_key(upper) + 1

    def n_active(lo, hi):
        return jnp.sum(jnp.where(hi > lo + 1, 1.0, 0.0))

    def cond(state):
        return state[2] > 0.0

    def step(state):
        lo, hi, _ = state
        mid = lax.shift_right_arithmetic(lo, 1) + lax.shift_right_arithmetic(hi, 1) + (lo & hi & 1)
        cnt = _count(score_ref, r0, ngroups, gw, _key_to_float(mid), strict=False)
        keep = cnt >= k
        lo = jnp.where(keep, mid, lo)
        hi = jnp.where(cnt == k, mid + 1, jnp.where(keep, hi, mid))
        return lo, hi, n_active(lo, hi)

    lo, _, _ = lax.while_loop(cond, step, (lo0, hi0, n_active(lo0, hi0)))
    return _key_to_float(lo)


def _emit_selection(score_ref, thr_ref, need_ref, ngroups, gw, nblk, bw, k, emit, emit_last=None):
    rows = score_ref.shape[0]
    assert k <= MAX_GROUPS and rows % ROW_CHUNK == 0

    def chunk(n, _):
        r0 = pl.multiple_of(n * ROW_CHUNK, ROW_CHUNK)
        thr = _kth_largest(score_ref, r0, ngroups, gw, k)
        thr_ref[pl.ds(r0, ROW_CHUNK), :] = thr
        need_ref[pl.ds(r0, ROW_CHUNK), :] = k - _count(score_ref, r0, ngroups, gw, thr, strict=True)
        return 0

    lax.fori_loop(0, rows // ROW_CHUNK, chunk, 0)
    thr = thr_ref[...]
    need = need_ref[...]
    finite = jnp.where(thr > -jnp.inf, 1.0, 0.0)
    before = (lax.broadcasted_iota(I32, (bw, bw), 0) < lax.broadcasted_iota(I32, (bw, bw), 1))
    before = jnp.where(before, 1.0, 0.0).astype(BF16)

    def body(j, carry, emit=emit):
        s = score_ref[:, pl.ds(pl.multiple_of(j * bw, bw), bw)]
        tie = jnp.where(s == thr, finite, 0.0)
        rank = _dot(tie.astype(BF16), before) + carry
        sel = jnp.where(s > thr, 1.0, jnp.where(rank < need, tie, 0.0))
        emit(j, sel)
        return carry + jnp.sum(tie, axis=1, keepdims=True)

    n_loop = nblk if emit_last is None else nblk - 1
    carry = lax.fori_loop(0, n_loop, body, jnp.zeros((rows, 1), F32))
    if emit_last is not None:
        body(nblk - 1, carry, emit=emit_last)


def _half_masks(shape):
    lane = lax.broadcasted_iota(I32, shape, 1)
    return lane < HEAD_DIM


def _index_kernel(iq_ref, wit_ref, ik_ref, mask_ref, score_ref, top_ref, *, tq, topk):
    i = pl.program_id(1)
    nblk_total = mask_ref.shape[2] // tq
    assert topk <= tq
    lo_half = _half_masks((tq, LANES))
    q_parts = []
    for h in range(N_IDX_HEADS):
        qp = iq_ref[0, :, (h // 2) * LANES:(h // 2 + 1) * LANES]
        qh = jnp.where(lo_half, qp, 0.0) if h % 2 == 0 else jnp.where(lo_half, 0.0, qp)
        q_hi, q_lo = _split2(qh)
        q_parts.append(jnp.concatenate([q_hi, q_hi, q_lo], axis=1))
    w_rows = [wit_ref[h:h + 1, :] for h in range(N_IDX_HEADS)]
    key = lax.broadcasted_iota(I32, (tq, tq), 0)
    qry = lax.broadcasted_iota(I32, (tq, tq), 1)
    top_ref[...] = jnp.full((tq, tq), -jnp.inf, F32)

    def score_block(j, _):
        base = pl.multiple_of(j * tq, tq)
        k_hi, k_lo = _split2(ik_ref[0, pl.ds(base, tq), :])
        k_parts = jnp.concatenate([k_hi, k_lo, k_hi], axis=1)
        s = jnp.zeros((tq, tq), F32)
        for h in range(N_IDX_HEADS):
            d = _dot_nt(k_parts, q_parts[h])
            s = s + w_rows[h] * jnp.maximum(d, 0.0)
        s = jnp.where(key + (j - i) * tq <= qry, s, -jnp.inf)
        score_ref[pl.ds(base, tq), :] = s
        top_ref[...] = jnp.maximum(top_ref[...], s)
        return 0

    lax.fori_loop(0, i + 1, score_block, 0)

    def count(cf, strict):
        cfb = jnp.broadcast_to(cf, (SUBLANES, tq))

        def body(j, accs):
            blk = score_ref[pl.ds(pl.multiple_of(j * tq, tq), tq), :]
            accs = list(accs)
            for r in range(tq // SUBLANES):
                s = blk[r * SUBLANES:(r + 1) * SUBLANES]
                hit = (s > cfb) if strict else (s >= cfb)
                accs[r % len(accs)] = accs[r % len(accs)] + jnp.where(hit, 1.0, 0.0)
            return tuple(accs)

        zero = jnp.zeros((SUBLANES, tq), F32)
        accs = lax.fori_loop(0, i + 1, body, (zero, zero, zero, zero))
        return jnp.sum((accs[0] + accs[1]) + (accs[2] + accs[3]), axis=0, keepdims=True)

    tops = top_ref[...]
    lo0 = _float_to_key(jnp.min(tops, axis=0, keepdims=True))
    hi0 = _float_to_key(jnp.max(tops, axis=0, keepdims=True)) + 1

    def n_active(lo, hi):
        return jnp.sum(jnp.where(hi > lo + 1, 1.0, 0.0))

    def step(state):
        lo, hi, _ = state
        mid = lax.shift_right_arithmetic(lo, 1) + lax.shift_right_arithmetic(hi, 1) + (lo & hi & 1)
        cnt = count(_key_to_float(mid), strict=False)
        keep = cnt >= topk
        lo = jnp.where(keep, mid, lo)
        hi = jnp.where(cnt == topk, mid + 1, jnp.where(keep, hi, mid))
        return lo, hi, n_active(lo, hi)

    at_least_zero = count(jnp.zeros((1, tq), F32), strict=False)
    above_zero = count(jnp.zeros((1, tq), F32), strict=True)
    zero_kth = jnp.logical_and(above_zero < topk, at_least_zero >= topk)
    lo0 = jnp.where(zero_kth, 0, jnp.where(above_zero >= topk, jnp.maximum(lo0, 1), lo0))
    hi0 = jnp.where(zero_kth, 1, jnp.where(at_least_zero < topk, jnp.minimum(hi0, 0), hi0))

    lo, _, _ = lax.while_loop(lambda state: state[2] > 0.0, step, (lo0, hi0, n_active(lo0, hi0)))
    thr = _key_to_float(lo)
    need = topk - count(thr, strict=True)
    finite = jnp.where(thr > -jnp.inf, 1.0, 0.0)
    before = jnp.where(qry < key, 1.0, 0.0).astype(BF16)
    eye = jnp.where(qry == key, 1.0, 0.0).astype(BF16)

    def emit(j, carry):
        base = pl.multiple_of(j * tq, tq)
        s = score_ref[pl.ds(base, tq), :]
        tie = jnp.where(s == thr, finite, 0.0)
        rank = _dot(before, tie.astype(BF16)) + carry
        sel = jnp.where(s > thr, 1.0, jnp.where(rank < need, tie, 0.0))
        mask_ref[0, :, pl.ds(base, tq)] = _dot_nt(eye, sel.astype(BF16)).astype(jnp.int8)
        return carry + jnp.sum(tie, axis=0, keepdims=True)

    carry = lax.fori_loop(0, (i + 1) // 2, lambda jj, c: emit(2 * jj + 1, emit(2 * jj, c)), jnp.zeros((1, tq), F32))

    @pl.when((i + 1) % 2 == 1)
    def _():
        emit(i, carry)

    def clear(j, _):
        mask_ref[0, :, pl.ds(pl.multiple_of(j * tq, tq), tq)] = jnp.zeros((tq, tq), jnp.int8)
        return 0

    lax.fori_loop(i + 1, nblk_total, clear, 0)


def _prompt_index(iq, wit, ik, tq, topk):
    b, t, _ = iq.shape
    nq = t // tq
    return pl.pallas_call(
        functools.partial(_index_kernel, tq=tq, topk=float(topk)),
        grid=(b, nq),
        in_specs=[pl.BlockSpec((1, tq, iq.shape[2]), lambda bi, i: (bi, i, 0)),
                  pl.BlockSpec((SUBLANES, tq), lambda bi, i: (0, bi * nq + i)),
                  pl.BlockSpec((1, t, LANES), lambda bi, i: (bi, 0, 0))],
        out_specs=pl.BlockSpec((1, tq, t), lambda bi, i: (bi, i, 0)),
        out_shape=SDS((b, t, t), jnp.int8),
        scratch_shapes=[pltpu.VMEM((t, tq), F32), pltpu.VMEM((tq, tq), F32)],
        compiler_params=_params("arbitrary", "arbitrary"),
        name="prompt_index",
    )(iq, wit, ik)


def _t5_bucket(rel):
    rel = jnp.maximum(rel, 0)
    lf = (jnp.log(jnp.maximum(rel, 1).astype(F32) / MAX_EXACT)
          / math.log(MAX_DISTANCE / MAX_EXACT) * (N_BUCKETS - MAX_EXACT))
    large = jnp.minimum(MAX_EXACT + lf.astype(I32), N_BUCKETS - 1)
    return jnp.where(rel < MAX_EXACT, rel, large)


def _bias_lookup(rb_ref, bucket, h):
    val = jnp.zeros(bucket.shape, F32)
    for b in range(N_BUCKETS):
        val = jnp.where(bucket == b, rb_ref[b, h], val)
    return val


def _bias_tiles_kernel(rb_ref, o_ref, *, tq):
    h = pl.program_id(0)
    row = lax.broadcasted_iota(I32, (tq, tq), 0)
    col = lax.broadcasted_iota(I32, (tq, tq), 1)
    far = rb_ref[N_BUCKETS - 1, h]
    for tile in range(2):
        o_ref[tile, 0] = (_bias_lookup(rb_ref, _t5_bucket(row - col + tile * tq), h) - far) * LOG2E
    o_ref[2, 0] = jnp.zeros((tq, tq), F32)


def _bias_tiles(rel_bias, tq):
    assert tq >= MAX_DISTANCE
    return pl.pallas_call(
        functools.partial(_bias_tiles_kernel, tq=tq),
        grid=(N_HEADS,),
        in_specs=[pl.BlockSpec(memory_space=pltpu.SMEM)],
        out_specs=pl.BlockSpec((3, 1, tq, tq), lambda h: (0, h, 0, 0)),
        out_shape=SDS((3, N_HEADS, tq, tq), F32),
        compiler_params=_params("arbitrary"),
        name="bias_tiles",
    )(rel_bias)


def _pair_queries(q_ref, tq):
    lo_half = _half_masks((tq, LANES))
    out = []
    for p in range(N_PAIRS):
        qp = q_ref[0, :, p * LANES:(p + 1) * LANES].astype(F32)
        out.append((jnp.where(lo_half, qp, 0.0).astype(BF16), jnp.where(lo_half, 0.0, qp).astype(BF16)))
    return out, lo_half


def _dsa_kernel(q_ref, k_ref, v_ref, mask_ref, bias_ref, o_ref, logit_ref, madd_ref, m_ref, acc_ref, *, tq):
    i = pl.program_id(1)
    tk = 2 * tq
    qs, lo_half = _pair_queries(q_ref, tq)
    m_ref[...] = jnp.full(m_ref.shape, NEG, F32)
    acc_ref[...] = jnp.zeros(acc_ref.shape, F32)
    keep_lo = jnp.where(_half_masks((tk, LANES)), 1.0, 0.0).astype(BF16)
    keep_hi = 1.0 - keep_lo

    def pair_step(jj, near):
        base = pl.multiple_of(jj * tk, tk)
        madd_ref[...] = jnp.where(mask_ref[0, :, pl.ds(base, tk)].astype(I32) != 0, 0.0, NEG)
        for p in range(N_PAIRS):
            for u in range(2):
                kp = k_ref[0, pl.ds(base + u * tq, tq), p * LANES:(p + 1) * LANES]
                for hh in range(2):
                    h = 2 * p + hh
                    lm = _dot_nt(qs[p][hh], kp) + madd_ref[:, u * tq:(u + 1) * tq]
                    if near:
                        lm = lm + bias_ref[jnp.clip(i - (2 * jj + u), 0, 2), h]
                    logit_ref[h, :, u * tq:(u + 1) * tq] = lm
        for p in range(N_PAIRS):
            vp = v_ref[0, pl.ds(base, tk), p * LANES:(p + 1) * LANES]
            for hh in range(2):
                h = 2 * p + hh
                m_old = m_ref[h]
                m_new = jnp.maximum(m_old, jnp.max(logit_ref[h], axis=1, keepdims=True))
                m_ref[h] = m_new
                pexp = jnp.concatenate(
                    [jnp.exp2(logit_ref[h, :, c * LANES:(c + 1) * LANES] - m_new) for c in range(tk // LANES)],
                    axis=1)
                vaug = vp * keep_lo + keep_hi if hh == 0 else vp * keep_hi + keep_lo
                acc_ref[h] = jnp.exp2(m_old - m_new) * acc_ref[h] + _dot(pexp.astype(BF16), vaug)

    n_pairs = (i + 2) // 2
    n_far = jnp.maximum(n_pairs - 2, 0)

    def far_body(jj, _):
        pair_step(jj, False)
        return 0

    def near_body(jj, _):
        pair_step(jj, True)
        return 0

    lax.fori_loop(0, n_far, far_body, 0)
    lax.fori_loop(n_far, n_pairs, near_body, 0)
    for p in range(N_PAIRS):
        outs = []
        for hh in range(2):
            acc = acc_ref[2 * p + hh]
            outs.append(acc / pltpu.roll(acc, HEAD_DIM, axis=1))
        o_ref[0, :, p * LANES:(p + 1) * LANES] = jnp.where(lo_half, outs[0], outs[1])


def _prompt_dsa(q, k, v, mask, bias, tq):
    b, t, dh = q.shape
    assert t % (2 * tq) == 0
    once = pl.Buffered(1)
    seq = pl.BlockSpec((1, t, dh), lambda bi, i: (bi, 0, 0), pipeline_mode=once)
    return pl.pallas_call(
        functools.partial(_dsa_kernel, tq=tq),
        grid=(b, t // tq),
        in_specs=[pl.BlockSpec((1, tq, dh), lambda bi, i: (bi, i, 0)), seq, seq,
                  pl.BlockSpec((1, tq, t), lambda bi, i: (bi, i, 0)),
                  pl.BlockSpec(bias.shape, lambda bi, i: (0, 0, 0, 0), pipeline_mode=once)],
        out_specs=pl.BlockSpec((1, tq, dh), lambda bi, i: (bi, i, 0)),
        out_shape=SDS((b, t, dh), F32),
        scratch_shapes=[pltpu.VMEM((N_HEADS, tq, 2 * tq), F32), pltpu.VMEM((tq, 2 * tq), F32),
                        pltpu.VMEM((N_HEADS, tq, LANES), F32), pltpu.VMEM((N_HEADS, tq, LANES), F32)],
        compiler_params=_params("arbitrary", "arbitrary"),
        name="prompt_dsa",
    )(q, k, v, mask, bias)


def _log_one_minus_beta(z):
    return -(jnp.maximum(z, 0.0) + jnp.log(1.0 + jnp.exp(-jnp.abs(z))))


def _sb_kernel(q_ref, k_ref, v_ref, o_ref, z_ref, tail_ref, c_ref, acc_ref, *, tq):
    i = pl.program_id(1)
    qs, lo_half = _pair_queries(q_ref, tq)
    c_ref[...] = jnp.zeros(c_ref.shape, F32)
    acc_ref[...] = jnp.zeros(acc_ref.shape, F32)
    row = lax.broadcasted_iota(I32, (tq, tq), 0)
    col = lax.broadcasted_iota(I32, (tq, tq), 1)
    not_before = jnp.where(row >= col, 1.0, 0.0).astype(BF16)
    not_before = jnp.concatenate([not_before, not_before], axis=0)
    earlier = col < row

    def block(j, diag):
        base = pl.multiple_of(j * tq, tq)
        for p in range(N_PAIRS):
            kp = k_ref[0, pl.ds(base, tq), p * LANES:(p + 1) * LANES]
            for hh in range(2):
                z_ref[2 * p + hh] = _dot_nt(qs[p][hh], kp)
        for h in range(N_HEADS):
            z = z_ref[h]
            sp = jnp.maximum(z, 0.0) + jnp.log2(1.0 + jnp.exp2(-jnp.abs(z)))
            if diag:
                sp = jnp.where(earlier, sp, 0.0)
            hi, lo = _split2(sp)
            tail_ref[h] = _dot(jnp.concatenate([hi, lo], axis=1), not_before)
        for p in range(N_PAIRS):
            vp = v_ref[0, pl.ds(base, tq), p * LANES:(p + 1) * LANES]
            pvs = []
            for hh in range(2):
                h = 2 * p + hh
                c_old = c_ref[h]
                a = jnp.concatenate(
                    [jnp.exp2(z_ref[h, :, n * LANES:(n + 1) * LANES] - tail_ref[h, :, n * LANES:(n + 1) * LANES]
                             - c_old) for n in range(tq // LANES)], axis=1)
                if diag:
                    a = jnp.where(earlier, a, 0.0)
                c_ref[h] = c_old + tail_ref[h, :, :1]
                pvs.append(_dot(a.astype(BF16), vp))
            sl = slice(p * LANES, (p + 1) * LANES)
            acc_ref[:, sl] = acc_ref[:, sl] + jnp.where(lo_half, pvs[0], pvs[1])

    block(i, True)

    def body(step, _):
        block(i - 1 - step, False)
        return 0

    lax.fori_loop(0, i, body, 0)
    o_ref[0] = acc_ref[...]


def _prompt_sb(q, k, v, tq):
    b, t, dh = q.shape
    seq = pl.BlockSpec((1, t, dh), lambda bi, i: (bi, 0, 0), pipeline_mode=pl.Buffered(1))
    return pl.pallas_call(
        functools.partial(_sb_kernel, tq=tq),
        grid=(b, t // tq),
        in_specs=[pl.BlockSpec((1, tq, dh), lambda bi, i: (bi, i, 0)), seq, seq],
        out_specs=pl.BlockSpec((1, tq, dh), lambda bi, i: (bi, i, 0)),
        out_shape=SDS((b, t, dh), F32),
        scratch_shapes=[pltpu.VMEM((N_HEADS, tq, tq), F32), pltpu.VMEM((N_HEADS, tq, tq), F32),
                        pltpu.VMEM((N_HEADS, tq, LANES), F32),
                        pltpu.VMEM((tq, dh), F32)],
        compiler_params=_params("arbitrary", "arbitrary"),
        name="prompt_sb",
    )(q, k, v)


def _merge_kernel(ya_ref, yb_ref, gate_ref, x_ref, g1_ref, n_ref, wa_ref, wb_ref, wo_ref, o_ref):
    d = x_ref.shape[1]
    ma = _dot(ya_ref[...].astype(BF16), wa_ref[...])
    mb = _dot(yb_ref[...].astype(BF16), wb_ref[...])
    merged = gate_ref[:, :d].astype(F32) * ma + gate_ref[:, d:].astype(F32) * mb
    mix = _dot(merged.astype(BF16), wo_ref[...])
    o_ref[...] = x_ref[...] + g1_ref[0] * _rms(mix, n_ref[...])


def _mod_spec(mod, tm, rows_per_mod):
    d = mod.shape[2]
    if mod.shape[1] == 1:
        return pl.BlockSpec((1, 1, d), lambda i: ((i * tm) // rows_per_mod, 0, 0))
    return pl.BlockSpec((1, tm, d), lambda i: (0, i, 0))


def _merge(ya, yb, gate, x, g1, norm, wa, wb, wo, tm, rows_per_mod):
    m, d = x.shape
    row = lambda a: pl.BlockSpec((tm, a.shape[1]), lambda i: (i, 0))
    full = lambda a: pl.BlockSpec(a.shape, lambda i: (0, 0))
    return pl.pallas_call(
        _merge_kernel,
        grid=(m // tm,),
        in_specs=[row(ya), row(yb), row(gate), row(x), _mod_spec(g1, tm, rows_per_mod),
                  full(norm), full(wa), full(wb), full(wo)],
        out_specs=pl.BlockSpec((tm, d), lambda i: (i, 0)),
        out_shape=SDS((m, d), F32),
        compiler_params=_params("arbitrary"),
        name="merge",
    )(ya, yb, gate, x, g1, norm, wa, wb, wo)


def _ffn_kernel(x_ref, sc_ref, sh_ref, g2_ref, npre_ref, npost_ref, wg_ref, wu_ref, wd_ref, o_ref, *, chunk):
    x = x_ref[...]
    h = (_rms(x, npre_ref[...]) * (1.0 + sc_ref[0]) + sh_ref[0]).astype(BF16)
    f = jnp.zeros(x.shape, F32)
    for c0 in range(0, wg_ref.shape[1], chunk):
        gate = _dot(h, wg_ref[:, c0:c0 + chunk])
        up = _dot(h, wu_ref[:, c0:c0 + chunk])
        act = gate / (1.0 + jnp.exp(-gate)) * up
        f = f + _dot(act.astype(BF16), wd_ref[c0:c0 + chunk, :])
    o_ref[...] = x + g2_ref[0] * _rms(f, npost_ref[...])


def _ffn(x, sc, sh, g2, npre, npost, wg, wu, wd, tm, rows_per_mod):
    m, d = x.shape
    full = lambda a: pl.BlockSpec(a.shape, lambda i: (0, 0))
    mod = lambda a: _mod_spec(a, tm, rows_per_mod)
    return pl.pallas_call(
        functools.partial(_ffn_kernel, chunk=256),
        grid=(m // tm,),
        in_specs=[pl.BlockSpec((tm, d), lambda i: (i, 0)), mod(sc), mod(sh), mod(g2),
                  full(npre), full(npost), full(wg), full(wu), full(wd)],
        out_specs=pl.BlockSpec((tm, d), lambda i: (i, 0)),
        out_shape=SDS((m, d), F32),
        compiler_params=_params("arbitrary"),
        name="ffn",
    )(x, sc, sh, g2, npre, npost, wg, wu, wd)


def _sample_score_kernel(pt_ref, q_ref, w_ref, knew_ref, *rest, pages):
    page_refs = rest[:pages]
    past_ref, new_ref = rest[pages:]
    q_hi, q_lo = _split2(q_ref[0])
    w = w_ref[0]
    t = q_ref.shape[1] // N_IDX_HEADS

    def score(keys):
        k_hi, k_lo = _split2(keys)
        r = jnp.maximum(_dot3(q_hi, q_lo, k_hi, k_lo), 0.0) * w
        s = r[0:t]
        for h in range(1, N_IDX_HEADS):
            s = s + r[h * t:(h + 1) * t]
        return s

    for n in range(pages):
        past_ref[0, :, n * LANES:(n + 1) * LANES] = score(page_refs[n][...])
    new_ref[0] = score(knew_ref[0])


def _sample_scores(page_table, q_rows, w_rows, k_new, cache_k_idx, pages):
    nb, n_pages = page_table.shape
    page = cache_k_idx.shape[2]
    assert page == LANES
    rows = q_rows.shape[1]
    t = rows // N_IDX_HEADS

    def page_spec(n):
        return pl.BlockSpec((None, None, IDX_DIM, page), lambda b, g, pt: (0, pt[b, g * pages + n], 0, 0))

    grid_spec = pltpu.PrefetchScalarGridSpec(
        num_scalar_prefetch=1,
        grid=(nb, n_pages // pages),
        in_specs=[pl.BlockSpec((1, rows, IDX_DIM), lambda b, g, pt: (b, 0, 0)),
                  pl.BlockSpec((1, rows, LANES), lambda b, g, pt: (b, 0, 0)),
                  pl.BlockSpec((1, IDX_DIM, page), lambda b, g, pt: (b, 0, 0))]
                 + [page_spec(n) for n in range(pages)],
        out_specs=[pl.BlockSpec((1, t, pages * page), lambda b, g, pt: (b, 0, g)),
                   pl.BlockSpec((1, t, page), lambda b, g, pt: (b, 0, 0))],
    )
    return pl.pallas_call(
        functools.partial(_sample_score_kernel, pages=pages),
        grid_spec=grid_spec,
        out_shape=[SDS((nb, t, n_pages * page), F32), SDS((nb, t, page), F32)],
        compiler_params=_params("arbitrary", "arbitrary"),
        name="sample_scores",
    )(page_table, q_rows, w_rows, k_new, *([cache_k_idx.transpose(0, 1, 3, 2)] * pages))


def _sample_select_kernel(past_ref, new_ref, mpast_ref, mnew_ref, score_ref, thr_ref, need_ref, *, t, topk):
    rows, past = past_ref.shape
    score_ref[:, :past] = past_ref[...]
    q_t = lax.broadcasted_iota(I32, (rows, LANES), 0) % t
    col = lax.broadcasted_iota(I32, (rows, LANES), 1)
    score_ref[:, past:] = jnp.where(col <= q_t, new_ref[...], -jnp.inf)

    def emit(j, sel):
        mpast_ref[:, pl.ds(pl.multiple_of(j * LANES, LANES), LANES)] = sel

    def emit_new(j, sel):
        mnew_ref[...] = sel

    width = score_ref.shape[1]
    gw = _group_width(width)
    _emit_selection(score_ref, thr_ref, need_ref, width // gw, gw, width // LANES, LANES, topk, emit, emit_new)


def _sample_select(s_past, s_new, t, topk, tr):
    rows, past = s_past.shape
    width = past + LANES
    return pl.pallas_call(
        functools.partial(_sample_select_kernel, t=t, topk=float(topk)),
        grid=(rows // tr,),
        in_specs=[pl.BlockSpec((tr, past), lambda i: (i, 0)), pl.BlockSpec((tr, LANES), lambda i: (i, 0))],
        out_specs=[pl.BlockSpec((tr, past), lambda i: (i, 0)), pl.BlockSpec((tr, LANES), lambda i: (i, 0))],
        out_shape=[SDS((rows, past), F32), SDS((rows, LANES), F32)],
        scratch_shapes=[pltpu.VMEM((tr, width), F32), pltpu.VMEM((tr, 1), F32), pltpu.VMEM((tr, 1), F32)],
        compiler_params=_params("arbitrary"),
        name="sample_select",
    )(s_past, s_new)


def _sample_bias_kernel(rb_ref, o_ref, *, t, page):
    key = lax.broadcasted_iota(I32, (page, N_HEADS * t), 0)
    col = lax.broadcasted_iota(I32, (page, N_HEADS * t), 1)
    q_t = col % t
    head = col // t
    for tile in range(2):
        bucket = _t5_bucket(q_t - key + tile * page)
        val = jnp.zeros(bucket.shape, F32)
        for h in range(N_HEADS):
            val = jnp.where(head == h, _bias_lookup(rb_ref, bucket, h), val)
        o_ref[tile] = val
    far = jnp.zeros(bucket.shape, F32)
    for h in range(N_HEADS):
        far = jnp.where(head == h, rb_ref[N_BUCKETS - 1, h], far)
    o_ref[2] = far


def _sample_bias(rel_bias, t, page):
    assert page >= MAX_DISTANCE
    return pl.pallas_call(
        functools.partial(_sample_bias_kernel, t=t, page=page),
        in_specs=[pl.BlockSpec(memory_space=pltpu.SMEM)],
        out_shape=SDS((3, page, N_HEADS * t), F32),
        name="sample_bias",
    )(rel_bias)


def _to_column(row_vec, n):
    eye = lax.broadcasted_iota(I32, (n, n), 0) == lax.broadcasted_iota(I32, (n, n), 1)
    return jnp.sum(jnp.where(eye, row_vec, 0.0), axis=1, keepdims=True)


def _sample_attn_kernel(pt_ref, qa_ref, qb_ref, knew_a_ref, vnew_a_ref, knew_b_ref, vnew_b_ref,
                        mnew_ref, mask_ref, bias_ref, *rest, pages, t):
    ka_refs = rest[0 * pages:1 * pages]
    va_refs = rest[1 * pages:2 * pages]
    kb_refs = rest[2 * pages:3 * pages]
    vb_refs = rest[3 * pages:4 * pages]
    ya_ref, yb_ref, m_ref, l_ref, acca_ref, c_ref, accb_ref = rest[4 * pages:]
    g = pl.program_id(1)
    n_groups = pl.num_programs(1)
    page = LANES
    cols = N_HEADS * t
    qa = qa_ref[0]
    qb = qb_ref[0]
    srow = lax.broadcasted_iota(I32, (page, page), 0)
    scol = lax.broadcasted_iota(I32, (page, page), 1)
    later = jnp.where(scol > srow, 1.0, 0.0).astype(BF16)

    def dsa_logits(k, bias):
        return _dot(k.astype(BF16), qa) + bias

    def dsa_update(blocks):
        m_old = m_ref[...]
        m_new = m_old
        for logits, valid, _ in blocks:
            m_new = jnp.maximum(m_new, jnp.max(jnp.where(valid, logits, NEG), axis=0, keepdims=True))
        alpha = jnp.exp(m_old - m_new)
        l_new = alpha * l_ref[...]
        contrib = jnp.zeros(acca_ref.shape, F32)
        for logits, valid, v in blocks:
            pexp = jnp.where(valid, jnp.exp(logits - m_new), 0.0)
            l_new = l_new + jnp.sum(pexp, axis=0, keepdims=True)
            contrib = contrib + _dot(pexp.T.astype(BF16), v.astype(BF16))
        m_ref[...] = m_new
        l_ref[...] = l_new
        acca_ref[...] = _to_column(alpha, cols) * acca_ref[...] + contrib

    def sb_block(k, v, earlier):
        z = _dot(k.astype(BF16), qb)
        lsm = _log_one_minus_beta(z)
        if earlier is not None:
            lsm = jnp.where(earlier, lsm, 0.0)
        hi, lo = _split2(lsm)
        between = _dot(later, hi) + _dot(later, lo)
        a = jnp.exp(z + lsm + between + c_ref[...])
        if earlier is not None:
            a = jnp.where(earlier, a, 0.0)
        c_ref[...] = c_ref[...] + between[:1, :] + lsm[:1, :]
        accb_ref[...] = accb_ref[...] + _dot(a.T.astype(BF16), v.astype(BF16))

    @pl.when(g == 0)
    def _():
        m_ref[...] = jnp.full(m_ref.shape, NEG, F32)
        l_ref[...] = jnp.zeros(l_ref.shape, F32)
        acca_ref[...] = jnp.zeros(acca_ref.shape, F32)
        c_ref[...] = jnp.zeros(c_ref.shape, F32)
        accb_ref[...] = jnp.zeros(accb_ref.shape, F32)
        key = lax.broadcasted_iota(I32, (page, cols), 0)
        q_t = lax.broadcasted_iota(I32, (page, cols), 1) % t
        valid = mnew_ref[0].astype(I32) != 0
        dsa_update([(dsa_logits(knew_a_ref[0], bias_ref[0]), valid, vnew_a_ref[0])])
        sb_block(knew_b_ref[0], vnew_b_ref[0], key < q_t)

    blocks = []
    for n in range(pages):
        valid = mask_ref[0, (pages - 1 - n) * page:(pages - n) * page, :].astype(I32) != 0
        last_page = jnp.logical_and(g == 0, n == 0)
        bias = jnp.where(last_page, bias_ref[1], bias_ref[2])
        blocks.append((dsa_logits(ka_refs[n][...], bias), valid, va_refs[n][...]))
    dsa_update(blocks)
    for n in range(pages):
        sb_block(kb_refs[n][...], vb_refs[n][...], None)

    @pl.when(g == n_groups - 1)
    def _():
        head_of_col = lax.broadcasted_iota(I32, (t, N_HEADS * HEAD_DIM), 1) // HEAD_DIM
        norm = acca_ref[...] / _to_column(l_ref[...], cols)
        ya = jnp.zeros((t, N_HEADS * HEAD_DIM), F32)
        yb = jnp.zeros((t, N_HEADS * HEAD_DIM), F32)
        for h in range(N_HEADS):
            ya = ya + jnp.where(head_of_col == h, norm[h * t:(h + 1) * t], 0.0)
            yb = yb + jnp.where(head_of_col == h, accb_ref[h * t:(h + 1) * t, :], 0.0)
        ya_ref[0] = ya
        yb_ref[0] = yb


def _sample_attn(page_table, qa_bd, qb_bd, knew_a, vnew_a, knew_b, vnew_b, mask_new, mask_past, bias,
                 cache_k_a, cache_v_a, cache_k_b, cache_v_b, pages, t):
    nb, n_pages = page_table.shape
    page = LANES
    dh = N_HEADS * HEAD_DIM
    cols = N_HEADS * t
    n_groups = n_pages // pages

    def page_spec(n):
        return pl.BlockSpec((None, None, page, dh),
                            lambda b, g, pt: (0, pt[b, n_pages - 1 - (g * pages + n)], 0, 0))

    per_seq = lambda shape: pl.BlockSpec((1,) + shape, lambda b, g, pt: (b, 0, 0))
    grid_spec = pltpu.PrefetchScalarGridSpec(
        num_scalar_prefetch=1,
        grid=(nb, n_groups),
        in_specs=[per_seq((dh, cols)), per_seq((dh, cols)),
                  per_seq((page, dh)), per_seq((page, dh)), per_seq((page, dh)), per_seq((page, dh)),
                  per_seq((page, cols)),
                  pl.BlockSpec((1, pages * page, cols), lambda b, g, pt: (b, n_groups - 1 - g, 0)),
                  pl.BlockSpec(bias.shape, lambda b, g, pt: (0, 0, 0))]
                 + [page_spec(n) for n in range(pages)] * 4,
        out_specs=[per_seq((t, dh)), per_seq((t, dh))],
        scratch_shapes=[pltpu.VMEM((1, cols), F32), pltpu.VMEM((1, cols), F32), pltpu.VMEM((cols, dh), F32),
                        pltpu.VMEM((1, cols), F32), pltpu.VMEM((cols, dh), F32)],
    )
    flat = lambda c: c.reshape(c.shape[0], c.shape[1], c.shape[2], dh)
    return pl.pallas_call(
        functools.partial(_sample_attn_kernel, pages=pages, t=t),
        grid_spec=grid_spec,
        out_shape=[SDS((nb, t, dh), F32), SDS((nb, t, dh), F32)],
        compiler_params=_params("arbitrary", "arbitrary"),
        name="sample_attn",
    )(page_table, qa_bd, qb_bd, knew_a, vnew_a, knew_b, vnew_b, mask_new, mask_past, bias,
      *([flat(cache_k_a)] * pages), *([flat(cache_v_a)] * pages),
      *([flat(cache_k_b)] * pages), *([flat(cache_v_b)] * pages))


def _sample_bias2_kernel(rb_ref, o_ref, *, t, page):
    rows, width = N_HEADS * t, page * N_HEADS
    row = lax.broadcasted_iota(I32, (rows, width), 0)
    lane = lax.broadcasted_iota(I32, (rows, width), 1)
    q_t, head = row % t, row // t
    key = lane // N_HEADS
    own = (lane % N_HEADS) == head
    for tile in range(3):
        bucket = _t5_bucket(q_t - key + tile * page)
        val = jnp.zeros((rows, width), F32)
        for h in range(N_HEADS):
            val = jnp.where(head == h, _bias_lookup(rb_ref, bucket, h), val)
        o_ref[tile] = jnp.where(own, val, NEG)


def _sample_bias2(rel_bias, t, page):
    assert page >= MAX_DISTANCE
    return pl.pallas_call(
        functools.partial(_sample_bias2_kernel, t=t, page=page),
        in_specs=[pl.BlockSpec(memory_space=pltpu.SMEM)],
        out_shape=SDS((3, N_HEADS * t, page * N_HEADS), F32),
        name="sample_bias",
    )(rel_bias)


def _sample_attn2_kernel(pt_ref, qa_ref, qb_ref, knew_a_ref, vnew_a_ref, knew_b_ref, vnew_b_ref,
                         mnew_ref, mask_ref, bias_ref, *rest, pages, t):
    ka_refs = rest[0 * pages:1 * pages]
    va_refs = rest[1 * pages:2 * pages]
    kb_refs = rest[2 * pages:3 * pages]
    vb_refs = rest[3 * pages:4 * pages]
    ya_ref, yb_ref, m_ref, l_ref, acca_ref, c_ref, accb_ref = rest[4 * pages:]
    g = pl.program_id(1)
    n_groups = pl.num_programs(1)
    rows = N_HEADS * t
    width = LANES * N_HEADS
    qa = qa_ref[0]
    qb = qb_ref[0]
    row = lax.broadcasted_iota(I32, (rows, width), 0)
    lane = lax.broadcasted_iota(I32, (rows, width), 1)
    own = (lane % N_HEADS) == (row // t)
    own_neg = jnp.where(own, 0.0, NEG)
    own_one = jnp.where(own, 1.0, 0.0)
    srow = lax.broadcasted_iota(I32, (LANES, LANES), 0)
    scol = lax.broadcasted_iota(I32, (LANES, LANES), 1)
    not_before = jnp.where(srow >= scol, 1.0, 0.0).astype(BF16)
    not_before = jnp.concatenate([not_before, not_before], axis=0)

    spread = lax.broadcasted_iota(I32, (LANES, width), 1) // N_HEADS == lax.broadcasted_iota(I32, (LANES, width), 0)
    spread = jnp.where(spread, 1.0, 0.0).astype(BF16)

    def page_rows(ref):
        return ref[...].reshape(width, HEAD_DIM)

    def dsa_logits(k, bias, flags):
        chosen = _dot(jnp.concatenate([flags] * N_HEADS, axis=0).astype(BF16), spread)
        return _dot_nt(qa, k.astype(BF16)) + bias + (1.0 - chosen) * NEG

    def dsa_update(blocks):
        m_old = m_ref[...]
        m_new = m_old
        for logits, _ in blocks:
            m_new = jnp.maximum(m_new, jnp.max(logits, axis=1, keepdims=True))
        alpha = jnp.exp(m_old - m_new)
        l_new = alpha * l_ref[...]
        contrib = jnp.zeros(acca_ref.shape, F32)
        for logits, v in blocks:
            pexp = jnp.exp(logits - m_new)
            l_new = l_new + jnp.sum(pexp, axis=1, keepdims=True)
            contrib = contrib + _dot(pexp.astype(BF16), v.astype(BF16))
        m_ref[...] = m_new
        l_ref[...] = l_new
        acca_ref[...] = alpha * acca_ref[...] + contrib

    def sb_update(blocks):
        n_lane_blocks = width // LANES
        zs = [_dot_nt(qb, k.astype(BF16)) for k, _, _, _ in blocks]
        inners = []
        for z, (_, _, live_one, _) in zip(zs, blocks):
            sp = (jnp.maximum(z, 0.0) + jnp.log(1.0 + jnp.exp(-jnp.abs(z)))) * live_one
            hi, lo = _split2(sp)
            inners.append([_dot(jnp.concatenate([hi[:, n * LANES:(n + 1) * LANES],
                                                 lo[:, n * LANES:(n + 1) * LANES]], axis=1), not_before)
                           for n in range(n_lane_blocks)])
        carry = c_ref[...]
        contrib = jnp.zeros(accb_ref.shape, F32)
        for z, inner, (_, v, _, live_neg) in zip(zs, inners, blocks):
            tails = [None] * n_lane_blocks
            for n in reversed(range(n_lane_blocks)):
                tails[n] = inner[n] + carry
                carry = carry + inner[n][:, :1]
            a = jnp.exp(z - jnp.concatenate(tails, axis=1) + live_neg)
            contrib = contrib + _dot(a.astype(BF16), v.astype(BF16))
        c_ref[...] = carry
        accb_ref[...] = accb_ref[...] + contrib

    @pl.when(g == 0)
    def _():
        m_ref[...] = jnp.full(m_ref.shape, NEG, F32)
        l_ref[...] = jnp.zeros(l_ref.shape, F32)
        acca_ref[...] = jnp.zeros(acca_ref.shape, F32)
        c_ref[...] = jnp.zeros(c_ref.shape, F32)
        accb_ref[...] = jnp.zeros(accb_ref.shape, F32)
        earlier = jnp.logical_and(own, lane // N_HEADS < row % t)
        dsa_update([(dsa_logits(knew_a_ref[0], bias_ref[0], mnew_ref[0]), vnew_a_ref[0])])
        sb_update([(knew_b_ref[0], vnew_b_ref[0], jnp.where(earlier, 1.0, 0.0), jnp.where(earlier, 0.0, NEG))])

    blocks = []
    for n in range(pages):
        flags = mask_ref[0, :, (pages - 1 - n) * LANES:(pages - n) * LANES]
        bias = bias_ref[2] if n else jnp.where(g == 0, bias_ref[1], bias_ref[2])
        blocks.append((dsa_logits(page_rows(ka_refs[n]), bias, flags), page_rows(va_refs[n])))
    dsa_update(blocks)
    sb_update([(page_rows(kb_refs[n]), page_rows(vb_refs[n]), own_one, own_neg) for n in range(pages)])

    @pl.when(g == n_groups - 1)
    def _():
        ya_ref[0] = acca_ref[...] / l_ref[...]
        yb_ref[0] = accb_ref[...]


def _sample_attn2(page_table, qa_rows, qb_rows, knew_a, vnew_a, knew_b, vnew_b, madd_new, madd_past, bias,
                  cache_k_a, cache_v_a, cache_k_b, cache_v_b, pages, t):
    nb, n_pages = page_table.shape
    rows = N_HEADS * t
    width = LANES * N_HEADS
    n_groups = n_pages // pages

    def page_spec(n):
        return pl.BlockSpec((None, None, LANES, N_HEADS, HEAD_DIM),
                            lambda b, g, pt: (0, pt[b, n_pages - 1 - (g * pages + n)], 0, 0, 0))

    per_seq = lambda shape: pl.BlockSpec((1,) + shape, lambda b, g, pt: (b, 0, 0))
    grid_spec = pltpu.PrefetchScalarGridSpec(
        num_scalar_prefetch=1,
        grid=(nb, n_groups),
        in_specs=[per_seq((rows, HEAD_DIM)), per_seq((rows, HEAD_DIM)),
                  per_seq((width, HEAD_DIM)), per_seq((width, HEAD_DIM)),
                  per_seq((width, HEAD_DIM)), per_seq((width, HEAD_DIM)),
                  per_seq((t, LANES)),
                  pl.BlockSpec((1, t, pages * LANES), lambda b, g, pt: (b, 0, n_groups - 1 - g)),
                  pl.BlockSpec(bias.shape, lambda b, g, pt: (0, 0, 0))]
                 + [page_spec(n) for n in range(pages)] * 4,
        out_specs=[per_seq((rows, HEAD_DIM)), per_seq((rows, HEAD_DIM))],
        scratch_shapes=[pltpu.VMEM((rows, 1), F32), pltpu.VMEM((rows, 1), F32), pltpu.VMEM((rows, HEAD_DIM), F32),
                        pltpu.VMEM((rows, 1), F32), pltpu.VMEM((rows, HEAD_DIM), F32)],
    )
    return pl.pallas_call(
        functools.partial(_sample_attn2_kernel, pages=pages, t=t),
        grid_spec=grid_spec,
        out_shape=[SDS((nb, rows, HEAD_DIM), F32), SDS((nb, rows, HEAD_DIM), F32)],
        compiler_params=_params("arbitrary", "arbitrary"),
        name="sample_attn",
    )(page_table, qa_rows, qb_rows, knew_a, vnew_a, knew_b, vnew_b, madd_new, madd_past, bias,
      *([cache_k_a] * pages), *([cache_v_a] * pages), *([cache_k_b] * pages), *([cache_v_b] * pages))


def _sample_bias3_kernel(rb_ref, o_ref, *, t, page):
    rows = N_HEADS * t
    row = lax.broadcasted_iota(I32, (rows, page), 0)
    key = lax.broadcasted_iota(I32, (rows, page), 1)
    q_t, head = row % t, row // t
    for tile in range(3):
        bucket = _t5_bucket(q_t - key + tile * page)
        val = jnp.zeros((rows, page), F32)
        for h in range(N_HEADS):
            val = jnp.where(head == h, _bias_lookup(rb_ref, bucket, h), val)
        o_ref[tile] = val


def _sample_bias3(rel_bias, t, page):
    assert page >= MAX_DISTANCE
    return pl.pallas_call(
        functools.partial(_sample_bias3_kernel, t=t, page=page),
        in_specs=[pl.BlockSpec(memory_space=pltpu.SMEM)],
        out_shape=SDS((3, N_HEADS * t, page), F32),
        name="sample_bias",
    )(rel_bias)


def _sample_attn3_kernel(pt_ref, qa_ref, qb_ref, knew_a_ref, vnew_a_ref, knew_b_ref, vnew_b_ref,
                         fnew_ref, flags_ref, bias_ref, *rest, pages, t):
    ka_refs = rest[0 * pages:1 * pages]
    va_refs = rest[1 * pages:2 * pages]
    kb_refs = rest[2 * pages:3 * pages]
    vb_refs = rest[3 * pages:4 * pages]
    ya_ref, yb_ref, m_ref, l_ref, acca_ref, c_ref, accb_ref = rest[4 * pages:]
    g = pl.program_id(1)
    n_groups = pl.num_programs(1)
    rows = N_HEADS * t
    qa = qa_ref[0]
    qb = qb_ref[0]
    srow = lax.broadcasted_iota(I32, (LANES, LANES), 0)
    scol = lax.broadcasted_iota(I32, (LANES, LANES), 1)
    not_before = jnp.where(srow >= scol, 1.0, 0.0).astype(BF16)
    not_before = jnp.concatenate([not_before, not_before], axis=0)

    def scores(q, k):
        return jnp.concatenate([_dot(q[h * t:(h + 1) * t], k[h].astype(BF16)) for h in range(N_HEADS)], axis=0)

    def weighted(p, v):
        return jnp.concatenate([_dot_nt(p[h * t:(h + 1) * t].astype(BF16), v[h].astype(BF16))
                                for h in range(N_HEADS)], axis=0)

    def dsa_update(blocks):
        logits = [scores(qa, k) + bias + (1.0 - jnp.concatenate([flags] * N_HEADS, axis=0)) * NEG
                  for k, _, bias, flags in blocks]
        m_old = m_ref[...]
        m_new = m_old
        for lg in logits:
            m_new = jnp.maximum(m_new, jnp.max(lg, axis=1, keepdims=True))
        alpha = jnp.exp(m_old - m_new)
        l_new = alpha * l_ref[...]
        contrib = jnp.zeros(acca_ref.shape, F32)
        for lg, (_, v, _, _) in zip(logits, blocks):
            pexp = jnp.exp(lg - m_new)
            l_new = l_new + jnp.sum(pexp, axis=1, keepdims=True)
            contrib = contrib + weighted(pexp, v)
        m_ref[...] = m_new
        l_ref[...] = l_new
        acca_ref[...] = alpha * acca_ref[...] + contrib

    def sb_update(blocks):
        zs = [scores(qb, k) for k, _, _ in blocks]
        tails = []
        for z, (_, _, earlier) in zip(zs, blocks):
            sp = jnp.maximum(z, 0.0) + jnp.log(1.0 + jnp.exp(-jnp.abs(z)))
            if earlier is not None:
                sp = jnp.where(earlier, sp, 0.0)
            hi, lo = _split2(sp)
            tails.append(_dot(jnp.concatenate([hi, lo], axis=1), not_before))
        carry = c_ref[...]
        contrib = jnp.zeros(accb_ref.shape, F32)
        for z, tail, (_, v, earlier) in zip(zs, tails, blocks):
            a = jnp.exp(z - tail - carry)
            if earlier is not None:
                a = jnp.where(earlier, a, 0.0)
            carry = carry + tail[:, :1]
            contrib = contrib + weighted(a, v)
        c_ref[...] = carry
        accb_ref[...] = accb_ref[...] + contrib

    @pl.when(g == 0)
    def _():
        m_ref[...] = jnp.full(m_ref.shape, NEG, F32)
        l_ref[...] = jnp.zeros(l_ref.shape, F32)
        acca_ref[...] = jnp.zeros(acca_ref.shape, F32)
        c_ref[...] = jnp.zeros(c_ref.shape, F32)
        accb_ref[...] = jnp.zeros(accb_ref.shape, F32)
        key = lax.broadcasted_iota(I32, (rows, LANES), 1)
        q_t = lax.broadcasted_iota(I32, (rows, LANES), 0) % t
        dsa_update([(knew_a_ref[0], vnew_a_ref[0], bias_ref[0], fnew_ref[0])])
        sb_update([(knew_b_ref[0], vnew_b_ref[0], key < q_t)])

    blocks = []
    for n in range(pages):
        flags = flags_ref[0, :, (pages - 1 - n) * LANES:(pages - n) * LANES]
        bias = bias_ref[2] if n else jnp.where(g == 0, bias_ref[1], bias_ref[2])
        blocks.append((ka_refs[n][...], va_refs[n][...], bias, flags))
    dsa_update(blocks)
    sb_update([(kb_refs[n][...], vb_refs[n][...], None) for n in range(pages)])

    @pl.when(g == n_groups - 1)
    def _():
        ya_ref[0] = acca_ref[...] / l_ref[...]
        yb_ref[0] = accb_ref[...]


def _sample_attn3(page_table, qa_rows, qb_rows, knew_a, vnew_a, knew_b, vnew_b, flags_new, flags_past, bias,
                  cache_k_a, cache_v_a, cache_k_b, cache_v_b, pages, t):
    nb, n_pages = page_table.shape
    rows = N_HEADS * t
    n_groups = n_pages // pages
    page_shape = (N_HEADS, HEAD_DIM, LANES)

    def page_spec(n):
        return pl.BlockSpec((None, None) + page_shape,
                            lambda b, g, pt: (0, pt[b, n_pages - 1 - (g * pages + n)], 0, 0, 0))

    per_seq = lambda shape: pl.BlockSpec((1,) + shape, lambda b, g, pt: (b,) + (0,) * len(shape))
    grid_spec = pltpu.PrefetchScalarGridSpec(
        num_scalar_prefetch=1,
        grid=(nb, n_groups),
        in_specs=[per_seq((rows, HEAD_DIM)), per_seq((rows, HEAD_DIM)),
                  per_seq(page_shape), per_seq(page_shape), per_seq(page_shape), per_seq(page_shape),
                  per_seq((t, LANES)),
                  pl.BlockSpec((1, t, pages * LANES), lambda b, g, pt: (b, 0, n_groups - 1 - g)),
                  pl.BlockSpec(bias.shape, lambda b, g, pt: (0, 0, 0))]
                 + [page_spec(n) for n in range(pages)] * 4,
        out_specs=[per_seq((rows, HEAD_DIM)), per_seq((rows, HEAD_DIM))],
        scratch_shapes=[pltpu.VMEM((rows, 1), F32), pltpu.VMEM((rows, 1), F32), pltpu.VMEM((rows, HEAD_DIM), F32),
                        pltpu.VMEM((rows, 1), F32), pltpu.VMEM((rows, HEAD_DIM), F32)],
    )
    key_minor = lambda c: c.transpose(0, 1, 3, 4, 2)
    return pl.pallas_call(
        functools.partial(_sample_attn3_kernel, pages=pages, t=t),
        grid_spec=grid_spec,
        out_shape=[SDS((nb, rows, HEAD_DIM), F32), SDS((nb, rows, HEAD_DIM), F32)],
        compiler_params=_params("arbitrary", "arbitrary"),
        name="sample_attn",
    )(page_table, qa_rows, qb_rows, knew_a, vnew_a, knew_b, vnew_b, flags_new, flags_past, bias,
      *([key_minor(cache_k_a)] * pages), *([key_minor(cache_v_a)] * pages),
      *([key_minor(cache_k_b)] * pages), *([key_minor(cache_v_b)] * pages))


def _largest_divisor(n, cap):
    d = min(n, cap)
    while n % d:
        d -= 1
    return d


def _block_diag_queries(q, nb, t):
    q4 = q.astype(F32).reshape(nb, t, N_HEADS, HEAD_DIM)
    eye = jnp.eye(N_HEADS, dtype=F32)
    bd = jnp.einsum('bthd,hg->bhdgt', q4, eye)
    return bd.reshape(nb, N_HEADS * HEAD_DIM, N_HEADS * t).astype(BF16)


def kernel(x_prompt, x_sample, c_prompt, c_sample, cache_k_a, cache_v_a, cache_k_idx, cache_k_b, cache_v_b,
           page_table, rel_bias, w_ada, b_ada, norm_pre_mix, norm_post_mix, norm_pre_ffn, norm_post_ffn,
           w_in, w_br_a, w_br_b, w_out, w_ffn_gate, w_ffn_up, w_ffn_down):
    depth = w_in.shape[0]
    assert depth == 1
    nb_p, t_p, d = x_prompt.shape
    nb_s, t_s, _ = x_sample.shape
    n_pages = page_table.shape[1]
    page = cache_k_a.shape[2]
    past = n_pages * page
    dh = N_HEADS * HEAD_DIM
    nq = N_IDX_HEADS * IDX_DIM
    topk_p = min(INDEX_TOPK, t_p // 4)
    topk_s = min(INDEX_TOPK, (past + t_s) // 4)
    tq = 256
    assert t_p % tq == 0 and topk_p <= tq and t_s * N_HEADS <= LANES and page == LANES

    w = w_in[0]
    o = np.cumsum([0, dh, dh, dh, nq, IDX_DIM, N_IDX_HEADS, dh, dh, dh, d, d])
    col = lambda n: w[:, o[n]:o[n + 1]]
    w_main = jnp.concatenate([col(0), col(1), col(2), col(6), col(7), col(8)], axis=1).astype(BF16)
    w_gate = jnp.concatenate([col(9), col(10)], axis=1).astype(BF16)
    w_idx = jnp.concatenate([col(3), col(4), col(4), col(5),
                             jnp.zeros((d, LANES - N_IDX_HEADS), F32)], axis=1)
    w_idx_hi = w_idx.astype(BF16)
    w_idx_lo = (w_idx - w_idx_hi.astype(F32)).astype(BF16)
    wa, wb, wo = w_br_a[0].astype(BF16), w_br_b[0].astype(BF16), w_out[0].astype(BF16)
    wg, wu, wd = w_ffn_gate[0].astype(BF16), w_ffn_up[0].astype(BF16), w_ffn_down[0].astype(BF16)
    n_pre_mix, n_post_mix = norm_pre_mix[0][None], norm_post_mix[0][None]
    n_pre_ffn, n_post_ffn = norm_pre_ffn[0][None], norm_post_ffn[0][None]

    n_c = nb_p + nb_s
    c_all = jnp.concatenate([c_prompt, c_sample, jnp.zeros((-n_c % 8, d), F32)], axis=0)
    ada = _ada(c_all, w_ada[0], b_ada[0][None])
    mods_p = [m[:, None, :] for m in jnp.split(ada[:nb_p], 6, axis=-1)]
    mods_s = [jnp.repeat(m, t_s, axis=0)[None] for m in jnp.split(ada[nb_p:n_c], 6, axis=-1)]

    def dense_front(x2, mods, tm, rows_per_mod, q_scale):
        return _inproj(x2, mods[1], mods[0], n_pre_mix, w_main, w_gate, w_idx_hi, w_idx_lo, tm, rows_per_mod,
                       q_scale)

    def dense_back(ya, yb, gate, x2, mods, tm, rows_per_mod):
        x1 = _merge(ya, yb, gate, x2, mods[2], n_post_mix, wa, wb, wo, tm, rows_per_mod)
        return _ffn(x1, mods[4], mods[3], mods[5], n_pre_ffn, n_post_ffn, wg, wu, wd, tm, rows_per_mod)

    m_p = nb_p * t_p
    xp2 = x_prompt.reshape(m_p, d)
    tm_p = 256
    (qa, ka, va, qb, kb, vb, kab, vab, kbb, vbb, gate, iq, ik, wi, kidx, wit) = dense_front(
        xp2, mods_p, tm_p, t_p, LOG2E * HEAD_DIM ** -0.5)
    seq3 = lambda a: a.reshape(nb_p, t_p, a.shape[-1])
    mask = _prompt_index(seq3(iq), wit, seq3(ik), tq, topk_p)
    ya = _prompt_dsa(seq3(qa), seq3(kab), seq3(vab), mask, _bias_tiles(rel_bias, tq), tq)
    yb = _prompt_sb(seq3(qb), seq3(kbb), seq3(vbb), tq)
    y_p = dense_back(ya.reshape(m_p, dh), yb.reshape(m_p, dh), gate, xp2, mods_p, tm_p, t_p)
    heads_p = lambda a: a.reshape(depth, nb_p, t_p, N_HEADS, HEAD_DIM)
    outs_p = (heads_p(ka), heads_p(va), kidx.reshape(depth, nb_p, t_p, IDX_DIM), heads_p(kb), heads_p(vb))

    m_s = nb_s * t_s
    xs2 = x_sample.reshape(m_s, d)
    tm_s = _largest_divisor(m_s, 256)
    (qa, ka, va, qb, kb, vb, _, _, _, _, gate, iq, ik, wi, kidx, _) = dense_front(
        xs2, mods_s, tm_s, t_s, HEAD_DIM ** -0.5)
    pad_keys = lambda a: jnp.pad(a.reshape(nb_s, t_s, a.shape[-1]), ((0, 0), (0, page - t_s), (0, 0)))
    q_rows = iq.reshape(nb_s, t_s, N_IDX_HEADS, IDX_DIM).transpose(0, 2, 1, 3).reshape(nb_s, N_IDX_HEADS * t_s, IDX_DIM)
    w_rows = wi.reshape(nb_s, t_s, LANES)[:, :, :N_IDX_HEADS].transpose(0, 2, 1).reshape(nb_s, N_IDX_HEADS * t_s, 1)
    w_rows = jnp.broadcast_to(w_rows, (nb_s, N_IDX_HEADS * t_s, LANES))
    pages_idx = _largest_divisor(n_pages, 16)
    k_new = jnp.pad(kidx.reshape(nb_s, t_s, IDX_DIM).transpose(0, 2, 1), ((0, 0), (0, 0), (0, page - t_s)))
    s_past, s_new = _sample_scores(page_table, q_rows, w_rows, k_new, cache_k_idx, pages_idx)
    sel_past, sel_new = _sample_select(s_past.reshape(m_s, past), s_new.reshape(m_s, page), t_s, topk_s,
                                       _largest_divisor(m_s, 128))
    head_rows = lambda a: (a.reshape(nb_s, t_s, N_HEADS, HEAD_DIM).transpose(0, 2, 1, 3)
                           .reshape(nb_s, N_HEADS * t_s, HEAD_DIM))
    as_page = lambda a: jnp.pad(a.reshape(nb_s, t_s, N_HEADS, HEAD_DIM).transpose(0, 2, 3, 1),
                                ((0, 0), (0, 0), (0, 0), (0, page - t_s)))
    pages_attn = _largest_divisor(n_pages, 8)
    ya, yb = _sample_attn3(page_table, head_rows(qa), head_rows(qb),
                           as_page(ka), as_page(va), as_page(kb), as_page(vb),
                           sel_new.reshape(nb_s, t_s, page), sel_past.reshape(nb_s, t_s, past),
                           _sample_bias3(rel_bias, t_s, page),
                           cache_k_a, cache_v_a, cache_k_b, cache_v_b, pages_attn, t_s)
    token_rows = lambda y: (y.reshape(nb_s, N_HEADS, t_s, HEAD_DIM).transpose(0, 2, 1, 3).reshape(m_s, dh))
    y_s = dense_back(token_rows(ya), token_rows(yb), gate, xs2, mods_s, tm_s, t_s)
    heads_s = lambda a: a.reshape(depth, nb_s, t_s, N_HEADS, HEAD_DIM)
    outs_s = (heads_s(ka), heads_s(va), kidx.reshape(depth, nb_s, t_s, IDX_DIM), heads_s(kb), heads_s(vb))

    return (y_p.reshape(nb_p, t_p, d), y_s.reshape(nb_s, t_s, d)) + outs_p + outs_s
```

```python
import functools
import math

import numpy as np
import jax
import jax.numpy as jnp
from jax import lax
from jax.experimental import pallas as pl
from jax.experimental.pallas import tpu as pltpu

F32 = jnp.float32
BF16 = jnp.bfloat16
I32 = jnp.int32
SDS = jax.ShapeDtypeStruct

HEAD_DIM = 64
N_HEADS = 8
N_PAIRS = N_HEADS // 2
N_IDX_HEADS = 4
IDX_DIM = 64
INDEX_TOPK = 256
N_BUCKETS = 32
MAX_EXACT = N_BUCKETS // 2
MAX_DISTANCE = 128
EPS = 1e-6
LANES = 128
SUBLANES = 8
NEG = -1e30
LOG2E = 1.4426950408889634
EXP2_CLAMP = 126.0
VMEM_LIMIT = 56 * 1024 * 1024

_I32_MIN = np.int32(-2 ** 31)
_KEY_NEG_INF = np.int32(0x807FFFFF - 2 ** 32)


def _params(*sem):
    return pltpu.CompilerParams(dimension_semantics=sem, vmem_limit_bytes=VMEM_LIMIT)


def _dot(a, b):
    return jnp.dot(a, b, preferred_element_type=F32)


def _dot_nt(a, b):
    return lax.dot_general(a, b, (((1,), (1,)), ((), ())), preferred_element_type=F32)


def _split2(x):
    hi = x.astype(BF16)
    lo = (x - hi.astype(F32)).astype(BF16)
    return hi, lo


def _dot3(a_hi, a_lo, b_hi, b_lo):
    return _dot(a_hi, b_hi) + _dot(a_lo, b_hi) + _dot(a_hi, b_lo)


def _dot3_nt(a_hi, a_lo, b_hi, b_lo):
    return _dot_nt(a_hi, b_hi) + _dot_nt(a_lo, b_hi) + _dot_nt(a_hi, b_lo)


def _rms(x, g):
    var = jnp.mean(x * x, axis=-1, keepdims=True)
    return x * lax.rsqrt(var + EPS) * g


def _ada_kernel(c_ref, w_ref, b_ref, o_ref):
    c_hi, c_lo = _split2(c_ref[...])
    w_hi, w_lo = _split2(w_ref[...])
    o_ref[...] = _dot3(c_hi, c_lo, w_hi, w_lo) + b_ref[...]


def _ada(c, w, b):
    rows, d = c.shape
    n = w.shape[1]
    tn = 1024
    return pl.pallas_call(
        _ada_kernel,
        grid=(n // tn,),
        in_specs=[pl.BlockSpec((rows, d), lambda j: (0, 0)),
                  pl.BlockSpec((d, tn), lambda j: (0, j)),
                  pl.BlockSpec((1, tn), lambda j: (0, j))],
        out_specs=pl.BlockSpec((rows, tn), lambda j: (0, j)),
        out_shape=SDS((rows, n), F32),
        compiler_params=_params("arbitrary"),
        name="ada",
    )(c, w, b)


def _inproj_kernel(x_ref, sc_ref, sh_ref, g_ref, wm_ref, wg_ref, wih_ref, wil_ref,
                   qa_ref, ka_ref, va_ref, qb_ref, kb_ref, vb_ref,
                   kab_ref, vab_ref, kbb_ref, vbb_ref,
                   gate_ref, iq_ref, ik_ref, wi_ref, kidx_ref, wit_ref, *, q_scale):
    dh = N_HEADS * HEAD_DIM
    h = _rms(x_ref[...], g_ref[...]) * (1.0 + sc_ref[0]) + sh_ref[0]
    h_hi, h_lo = _split2(h)

    def seg(n):
        return _dot(h_hi, wm_ref[:, n * dh:(n + 1) * dh])

    qa_ref[...] = (seg(0) * q_scale).astype(BF16)
    ka = seg(1)
    ka_ref[0] = ka.T
    kab_ref[...] = ka.astype(BF16)
    va = seg(2)
    va_ref[0] = va.T
    vab_ref[...] = va.astype(BF16)
    qb_ref[...] = (seg(3) * q_scale).astype(BF16)
    kb = seg(4)
    kb_ref[0] = kb.T
    kbb_ref[...] = kb.astype(BF16)
    vb = seg(5)
    vb_ref[0] = vb.T
    vbb_ref[...] = vb.astype(BF16)

    d = x_ref.shape[1]
    for n in range(2):
        g = _dot(h_hi, wg_ref[:, n * d:(n + 1) * d])
        gate_ref[:, n * d:(n + 1) * d] = (1.0 / (1.0 + jnp.exp(-g))).astype(BF16)

    nq = N_IDX_HEADS * IDX_DIM
    idx = _dot3(h_hi, h_lo, wih_ref[...], wil_ref[...])
    iq_ref[...] = idx[:, :nq] * IDX_DIM ** -0.5
    ik_ref[...] = idx[:, nq:nq + LANES]
    kidx_ref[0] = idx[:, nq:nq + LANES].T[:IDX_DIM, :]
    wi = idx[:, nq + LANES:nq + 2 * LANES] * N_IDX_HEADS ** -0.5
    wi_ref[...] = wi
    wit_ref[...] = wi.T[:SUBLANES, :]


def _inproj(x, sc, sh, g, wm, wg, wih, wil, tm, rows_per_mod, seq_len, q_scale):
    m, d = x.shape
    dh = N_HEADS * HEAD_DIM
    r = sc.shape[1]
    if r == 1:
        mod_map = lambda i: ((i * tm) // rows_per_mod, 0, 0)
    else:
        mod_map = lambda i: (0, i, 0)
    row = lambda w: pl.BlockSpec((tm, w), lambda i: (i, 0))
    full = lambda a: pl.BlockSpec(a.shape, lambda i: (0, 0), pipeline_mode=pl.Buffered(1))
    nq = N_IDX_HEADS * IDX_DIM
    outs = [(dh, BF16), (dh, None), (dh, None), (dh, BF16), (dh, None), (dh, None),
            (dh, BF16), (dh, BF16), (dh, BF16), (dh, BF16),
            (2 * d, BF16), (nq, F32), (LANES, F32), (LANES, F32), (IDX_DIM, None)]
    tiles_per_seq = seq_len // tm
    token_minor = lambda w: pl.BlockSpec((1, w, tm), lambda i: (i // tiles_per_seq, 0, i % tiles_per_seq))
    return pl.pallas_call(
        functools.partial(_inproj_kernel, q_scale=q_scale),
        grid=(m // tm,),
        in_specs=[row(d), pl.BlockSpec((1, r, d), mod_map), pl.BlockSpec((1, r, d), mod_map),
                  full(g), full(wm), full(wg), full(wih), full(wil)],
        out_specs=[row(w) if dt else token_minor(w) for w, dt in outs]
                  + [pl.BlockSpec((SUBLANES, tm), lambda i: (0, i))],
        out_shape=[SDS((m, w), dt) if dt else SDS((m // seq_len, w, seq_len), F32) for w, dt in outs]
                  + [SDS((SUBLANES, m), F32)],
        compiler_params=_params("arbitrary"),
        name="inproj",
    )(x, sc, sh, g, wm, wg, wih, wil)


ROW_CHUNK = 64
MAX_GROUPS = 256


def _key_to_float(key):
    bits = jnp.where(key < 0, key ^ jnp.int32(0x7FFFFFFF), key)
    f = lax.bitcast_convert_type(bits, F32)
    return jnp.where(key < _KEY_NEG_INF, -jnp.inf, f)


def _float_to_key(f):
    bits = lax.bitcast_convert_type(f, I32)
    return jnp.where(bits < 0, bits ^ jnp.int32(0x7FFFFFFF), bits)


def _group_width(width):
    gw = min(width, 8 * LANES)
    while width % gw:
        gw -= LANES
    return gw


def _count(score_ref, r0, ngroups, gw, cf, strict):
    cfb = jnp.broadcast_to(cf, (ROW_CHUNK, LANES))

    def body(g, acc):
        base = pl.multiple_of(g * gw, gw)
        for c in range(gw // LANES):
            s = score_ref[pl.ds(r0, ROW_CHUNK), pl.ds(base + c * LANES, LANES)]
            hit = (s > cfb) if strict else (s >= cfb)
            acc = acc + jnp.where(hit, 1.0, 0.0)
        return acc

    acc = lax.fori_loop(0, ngroups, body, jnp.zeros((ROW_CHUNK, LANES), F32))
    return jnp.sum(acc, axis=1, keepdims=True)


def _kth_largest(score_ref, r0, ngroups, gw, k):
    ninf = jnp.full((ROW_CHUNK, LANES), -jnp.inf, F32)

    def bounds(g, carry):
        base = pl.multiple_of(g * gw, gw)
        tops = list(carry)
        for c in range(gw // LANES):
            s = score_ref[pl.ds(r0, ROW_CHUNK), pl.ds(base + c * LANES, LANES)]
            tops[c % 2] = jnp.maximum(tops[c % 2], s)
        return tuple(tops)

    top_even, top_odd = lax.fori_loop(0, ngroups, bounds, (ninf, ninf))
    upper = jnp.max(jnp.maximum(top_even, top_odd), axis=1, keepdims=True)
    lower = jnp.min(jnp.minimum(top_even, top_odd), axis=1, keepdims=True)
    lo0 = _float_to_key(lower)
    hi0 = _float_to_key(upper) + 1

    def n_active(lo, hi):
        return jnp.sum(jnp.where(hi > lo + 1, 1.0, 0.0))

    def cond(state):
        return state[2] > 0.0

    def step(state):
        lo, hi, _ = state
        mid = lax.shift_right_arithmetic(lo, 1) + lax.shift_right_arithmetic(hi, 1) + (lo & hi & 1)
        cnt = _count(score_ref, r0, ngroups, gw, _key_to_float(mid), strict=False)
        keep = cnt >= k
        lo = jnp.where(keep, mid, lo)
        hi = jnp.where(cnt == k, mid + 1, jnp.where(keep, hi, mid))
        return lo, hi, n_active(lo, hi)

    lo, _, _ = lax.while_loop(cond, step, (lo0, hi0, n_active(lo0, hi0)))
    return _key_to_float(lo)


def _emit_selection(score_ref, thr_ref, need_ref, ngroups, gw, nblk, bw, k, emit, emit_last=None):
    rows = score_ref.shape[0]
    assert k <= MAX_GROUPS and rows % ROW_CHUNK == 0

    def chunk(n, _):
        r0 = pl.multiple_of(n * ROW_CHUNK, ROW_CHUNK)
        thr = _kth_largest(score_ref, r0, ngroups, gw, k)
        thr_ref[pl.ds(r0, ROW_CHUNK), :] = thr
        need_ref[pl.ds(r0, ROW_CHUNK), :] = k - _count(score_ref, r0, ngroups, gw, thr, strict=True)
        return 0

    lax.fori_loop(0, rows // ROW_CHUNK, chunk, 0)
    thr = thr_ref[...]
    need = need_ref[...]
    finite = jnp.where(thr > -jnp.inf, 1.0, 0.0)
    before = (lax.broadcasted_iota(I32, (bw, bw), 0) < lax.broadcasted_iota(I32, (bw, bw), 1))
    before = jnp.where(before, 1.0, 0.0).astype(BF16)

    def body(j, carry, emit=emit):
        s = score_ref[:, pl.ds(pl.multiple_of(j * bw, bw), bw)]
        tie = jnp.where(s == thr, finite, 0.0)
        rank = _dot(tie.astype(BF16), before) + carry
        sel = jnp.where(s > thr, 1.0, jnp.where(rank < need, tie, 0.0))
        emit(j, sel)
        return carry + jnp.sum(tie, axis=1, keepdims=True)

    n_loop = nblk if emit_last is None else nblk - 1
    carry = lax.fori_loop(0, n_loop, body, jnp.zeros((rows, 1), F32))
    if emit_last is not None:
        body(nblk - 1, carry, emit=emit_last)


def _half_masks(shape):
    lane = lax.broadcasted_iota(I32, shape, 1)
    return lane < HEAD_DIM


def _index_kernel(iq_ref, wit_ref, ik_ref, mask_ref, score_ref, top_ref, *, tq, topk):
    i = pl.program_id(1)
    nblk_total = mask_ref.shape[2] // tq
    assert topk <= tq
    lo_half = _half_masks((tq, LANES))
    q_parts = []
    for h in range(N_IDX_HEADS):
        qp = iq_ref[0, :, (h // 2) * LANES:(h // 2 + 1) * LANES]
        qh = jnp.where(lo_half, qp, 0.0) if h % 2 == 0 else jnp.where(lo_half, 0.0, qp)
        q_hi, q_lo = _split2(qh)
        q_parts.append(jnp.concatenate([q_hi, q_hi, q_lo], axis=1))
    w_rows = [wit_ref[h:h + 1, :] for h in range(N_IDX_HEADS)]
    key = lax.broadcasted_iota(I32, (tq, tq), 0)
    qry = lax.broadcasted_iota(I32, (tq, tq), 1)
    top_ref[...] = jnp.full((tq, tq), -jnp.inf, F32)

    def score_block(j, _):
        base = pl.multiple_of(j * tq, tq)
        k_hi, k_lo = _split2(ik_ref[0, pl.ds(base, tq), :])
        k_parts = jnp.concatenate([k_hi, k_lo, k_hi], axis=1)
        s = jnp.zeros((tq, tq), F32)
        for h in range(N_IDX_HEADS):
            d = _dot_nt(k_parts, q_parts[h])
            s = s + w_rows[h] * jnp.maximum(d, 0.0)
        s = jnp.where(key + (j - i) * tq <= qry, s, -jnp.inf)
        score_ref[pl.ds(base, tq), :] = s
        top_ref[...] = jnp.maximum(top_ref[...], s)
        return 0

    lax.fori_loop(0, i + 1, score_block, 0)

    def count(cf, strict):
        cfb = jnp.broadcast_to(cf, (SUBLANES, tq))

        def body(j, accs):
            blk = score_ref[pl.ds(pl.multiple_of(j * tq, tq), tq), :]
            accs = list(accs)
            for r in range(tq // SUBLANES):
                s = blk[r * SUBLANES:(r + 1) * SUBLANES]
                hit = (s > cfb) if strict else (s >= cfb)
                accs[r % len(accs)] = accs[r % len(accs)] + jnp.where(hit, 1.0, 0.0)
            return tuple(accs)

        zero = jnp.zeros((SUBLANES, tq), F32)
        accs = lax.fori_loop(0, i + 1, body, (zero, zero, zero, zero))
        return jnp.sum((accs[0] + accs[1]) + (accs[2] + accs[3]), axis=0, keepdims=True)

    tops = top_ref[...]
    lo0 = _float_to_key(jnp.min(tops, axis=0, keepdims=True))
    hi0 = _float_to_key(jnp.max(tops, axis=0, keepdims=True)) + 1

    def n_active(lo, hi):
        return jnp.sum(jnp.where(hi > lo + 1, 1.0, 0.0))

    def step(state):
        lo, hi, _ = state
        mid = lax.shift_right_arithmetic(lo, 1) + lax.shift_right_arithmetic(hi, 1) + (lo & hi & 1)
        cnt = count(_key_to_float(mid), strict=False)
        keep = cnt >= topk
        lo = jnp.where(keep, mid, lo)
        hi = jnp.where(cnt == topk, mid + 1, jnp.where(keep, hi, mid))
        return lo, hi, n_active(lo, hi)

    at_least_zero = count(jnp.zeros((1, tq), F32), strict=False)
    above_zero = count(jnp.zeros((1, tq), F32), strict=True)
    zero_kth = jnp.logical_and(above_zero < topk, at_least_zero >= topk)
    lo0 = jnp.where(zero_kth, 0, jnp.where(above_zero >= topk, jnp.maximum(lo0, 1), lo0))
    hi0 = jnp.where(zero_kth, 1, jnp.where(at_least_zero < topk, jnp.minimum(hi0, 0), hi0))

    lo, _, _ = lax.while_loop(lambda state: state[2] > 0.0, step, (lo0, hi0, n_active(lo0, hi0)))
    thr = _key_to_float(lo)
    need = topk - count(thr, strict=True)
    finite = jnp.where(thr > -jnp.inf, 1.0, 0.0)
    before = jnp.where(qry < key, 1.0, 0.0).astype(BF16)
    eye = jnp.where(qry == key, 1.0, 0.0).astype(BF16)

    def emit(j, carry):
        base = pl.multiple_of(j * tq, tq)
        s = score_ref[pl.ds(base, tq), :]
        tie = jnp.where(s == thr, finite, 0.0)
        rank = _dot(before, tie.astype(BF16)) + carry
        sel = jnp.where(s > thr, 1.0, jnp.where(rank < need, tie, 0.0))
        mask_ref[0, :, pl.ds(base, tq)] = _dot_nt(eye, sel.astype(BF16)).astype(jnp.int8)
        return carry + jnp.sum(tie, axis=0, keepdims=True)

    carry = lax.fori_loop(0, (i + 1) // 2, lambda jj, c: emit(2 * jj + 1, emit(2 * jj, c)), jnp.zeros((1, tq), F32))

    @pl.when((i + 1) % 2 == 1)
    def _():
        emit(i, carry)

    def clear(j, _):
        mask_ref[0, :, pl.ds(pl.multiple_of(j * tq, tq), tq)] = jnp.zeros((tq, tq), jnp.int8)
        return 0

    lax.fori_loop(i + 1, nblk_total, clear, 0)


def _prompt_index(iq, wit, ik, tq, topk):
    b, t, _ = iq.shape
    nq = t // tq
    return pl.pallas_call(
        functools.partial(_index_kernel, tq=tq, topk=float(topk)),
        grid=(b, nq),
        in_specs=[pl.BlockSpec((1, tq, iq.shape[2]), lambda bi, i: (bi, i, 0)),
                  pl.BlockSpec((SUBLANES, tq), lambda bi, i: (0, bi * nq + i)),
                  pl.BlockSpec((1, t, LANES), lambda bi, i: (bi, 0, 0))],
        out_specs=pl.BlockSpec((1, tq, t), lambda bi, i: (bi, i, 0)),
        out_shape=SDS((b, t, t), jnp.int8),
        scratch_shapes=[pltpu.VMEM((t, tq), F32), pltpu.VMEM((tq, tq), F32)],
        compiler_params=_params("arbitrary", "arbitrary"),
        name="prompt_index",
    )(iq, wit, ik)


def _t5_bucket(rel):
    rel = jnp.maximum(rel, 0)
    lf = (jnp.log(jnp.maximum(rel, 1).astype(F32) / MAX_EXACT)
          / math.log(MAX_DISTANCE / MAX_EXACT) * (N_BUCKETS - MAX_EXACT))
    large = jnp.minimum(MAX_EXACT + lf.astype(I32), N_BUCKETS - 1)
    return jnp.where(rel < MAX_EXACT, rel, large)


def _bias_lookup(rb_ref, bucket, h):
    val = jnp.zeros(bucket.shape, F32)
    for b in range(N_BUCKETS):
        val = jnp.where(bucket == b, rb_ref[b, h], val)
    return val


def _bias_tiles_kernel(rb_ref, o_ref, *, tq):
    h = pl.program_id(0)
    row = lax.broadcasted_iota(I32, (tq, tq), 0)
    col = lax.broadcasted_iota(I32, (tq, tq), 1)
    far = rb_ref[N_BUCKETS - 1, h]
    for tile in range(2):
        o_ref[tile, 0] = (_bias_lookup(rb_ref, _t5_bucket(row - col + tile * tq), h) - far) * LOG2E
    o_ref[2, 0] = jnp.zeros((tq, tq), F32)


def _bias_tiles(rel_bias, tq):
    assert tq >= MAX_DISTANCE
    return pl.pallas_call(
        functools.partial(_bias_tiles_kernel, tq=tq),
        grid=(N_HEADS,),
        in_specs=[pl.BlockSpec(memory_space=pltpu.SMEM)],
        out_specs=pl.BlockSpec((3, 1, tq, tq), lambda h: (0, h, 0, 0)),
        out_shape=SDS((3, N_HEADS, tq, tq), F32),
        compiler_params=_params("arbitrary"),
        name="bias_tiles",
    )(rel_bias)


def _pair_queries(q_ref, tq):
    lo_half = _half_masks((tq, LANES))
    out = []
    for p in range(N_PAIRS):
        qp = q_ref[0, :, p * LANES:(p + 1) * LANES].astype(F32)
        out.append((jnp.where(lo_half, qp, 0.0).astype(BF16), jnp.where(lo_half, 0.0, qp).astype(BF16)))
    return out, lo_half


def _dsa_kernel(q_ref, k_ref, v_ref, mask_ref, bias_ref, o_ref, logit_ref, madd_ref, m_ref, acc_ref, *, tq):
    i = pl.program_id(1)
    tk = 2 * tq
    qs, lo_half = _pair_queries(q_ref, tq)
    m_ref[...] = jnp.full(m_ref.shape, NEG, F32)
    acc_ref[...] = jnp.zeros(acc_ref.shape, F32)
    keep_lo = jnp.where(_half_masks((tk, LANES)), 1.0, 0.0).astype(BF16)
    keep_hi = 1.0 - keep_lo

    def pair_step(jj, near):
        base = pl.multiple_of(jj * tk, tk)
        madd_ref[...] = jnp.where(mask_ref[0, :, pl.ds(base, tk)].astype(I32) != 0, 0.0, NEG)
        for p in range(N_PAIRS):
            for u in range(2):
                kp = k_ref[0, pl.ds(base + u * tq, tq), p * LANES:(p + 1) * LANES]
                for hh in range(2):
                    h = 2 * p + hh
                    lm = _dot_nt(qs[p][hh], kp) + madd_ref[:, u * tq:(u + 1) * tq]
                    if near:
                        lm = lm + bias_ref[jnp.clip(i - (2 * jj + u), 0, 2), h]
                    logit_ref[h, :, u * tq:(u + 1) * tq] = lm
        for p in range(N_PAIRS):
            vp = v_ref[0, pl.ds(base, tk), p * LANES:(p + 1) * LANES]
            for hh in range(2):
                h = 2 * p + hh
                m_old = m_ref[h]
                m_new = jnp.maximum(m_old, jnp.max(logit_ref[h], axis=1, keepdims=True))
                m_ref[h] = m_new
                pexp = jnp.concatenate(
                    [jnp.exp2(logit_ref[h, :, c * LANES:(c + 1) * LANES] - m_new) for c in range(tk // LANES)],
                    axis=1)
                vaug = vp * keep_lo + keep_hi if hh == 0 else vp * keep_hi + keep_lo
                acc_ref[h] = jnp.exp2(m_old - m_new) * acc_ref[h] + _dot(pexp.astype(BF16), vaug)

    n_pairs = (i + 2) // 2
    n_far = jnp.maximum(n_pairs - 2, 0)

    def far_body(jj, _):
        pair_step(jj, False)
        return 0

    def near_body(jj, _):
        pair_step(jj, True)
        return 0

    lax.fori_loop(0, n_far, far_body, 0)
    lax.fori_loop(n_far, n_pairs, near_body, 0)
    for p in range(N_PAIRS):
        outs = []
        for hh in range(2):
            acc = acc_ref[2 * p + hh]
            outs.append(acc / pltpu.roll(acc, HEAD_DIM, axis=1))
        o_ref[0, :, p * LANES:(p + 1) * LANES] = jnp.where(lo_half, outs[0], outs[1])


def _prompt_dsa(q, k, v, mask, bias, tq):
    b, t, dh = q.shape
    assert t % (2 * tq) == 0
    once = pl.Buffered(1)
    seq = pl.BlockSpec((1, t, dh), lambda bi, i: (bi, 0, 0), pipeline_mode=once)
    return pl.pallas_call(
        functools.partial(_dsa_kernel, tq=tq),
        grid=(b, t // tq),
        in_specs=[pl.BlockSpec((1, tq, dh), lambda bi, i: (bi, i, 0)), seq, seq,
                  pl.BlockSpec((1, tq, t), lambda bi, i: (bi, i, 0)),
                  pl.BlockSpec(bias.shape, lambda bi, i: (0, 0, 0, 0), pipeline_mode=once)],
        out_specs=pl.BlockSpec((1, tq, dh), lambda bi, i: (bi, i, 0)),
        out_shape=SDS((b, t, dh), F32),
        scratch_shapes=[pltpu.VMEM((N_HEADS, tq, 2 * tq), F32), pltpu.VMEM((tq, 2 * tq), F32),
                        pltpu.VMEM((N_HEADS, tq, LANES), F32), pltpu.VMEM((N_HEADS, tq, LANES), F32)],
        compiler_params=_params("arbitrary", "arbitrary"),
        name="prompt_dsa",
    )(q, k, v, mask, bias)


def _log_one_minus_beta(z):
    return -(jnp.maximum(z, 0.0) + jnp.log(1.0 + jnp.exp(-jnp.abs(z))))


def _sb_kernel(q_ref, k_ref, v_ref, o_ref, z_ref, tail_ref, c_ref, acc_ref, *, tq):
    i = pl.program_id(1)
    qs, lo_half = _pair_queries(q_ref, tq)
    c_ref[...] = jnp.zeros(c_ref.shape, F32)
    acc_ref[...] = jnp.zeros(acc_ref.shape, F32)
    row = lax.broadcasted_iota(I32, (tq, tq), 0)
    col = lax.broadcasted_iota(I32, (tq, tq), 1)
    not_before = jnp.where(row >= col, 1.0, 0.0).astype(BF16)
    earlier = col < row

    def block(j, diag):
        base = pl.multiple_of(j * tq, tq)
        for p in range(N_PAIRS):
            kp = k_ref[0, pl.ds(base, tq), p * LANES:(p + 1) * LANES]
            for hh in range(2):
                z_ref[2 * p + hh] = _dot_nt(qs[p][hh], kp)
        for h in range(N_HEADS):
            z = z_ref[h]
            sp = jnp.maximum(z, jnp.log2(1.0 + jnp.exp2(jnp.minimum(z, EXP2_CLAMP))))
            if diag:
                sp = jnp.where(earlier, sp, 0.0)
            tail_ref[h] = _dot(sp.astype(BF16), not_before)
        for p in range(N_PAIRS):
            vp = v_ref[0, pl.ds(base, tq), p * LANES:(p + 1) * LANES]
            pvs = []
            for hh in range(2):
                h = 2 * p + hh
                c_old = c_ref[h]
                a = jnp.concatenate(
                    [jnp.exp2(z_ref[h, :, n * LANES:(n + 1) * LANES] - tail_ref[h, :, n * LANES:(n + 1) * LANES]
                             - c_old) for n in range(tq // LANES)], axis=1)
                if diag:
                    a = jnp.where(earlier, a, 0.0)
                c_ref[h] = c_old + tail_ref[h, :, :1]
                pvs.append(_dot(a.astype(BF16), vp))
            sl = slice(p * LANES, (p + 1) * LANES)
            acc_ref[:, sl] = acc_ref[:, sl] + jnp.where(lo_half, pvs[0], pvs[1])

    block(i, True)

    def body(step, _):
        block(i - 1 - step, False)
        return 0

    lax.fori_loop(0, i, body, 0)
    o_ref[0] = acc_ref[...]


def _prompt_sb(q, k, v, tq):
    b, t, dh = q.shape
    seq = pl.BlockSpec((1, t, dh), lambda bi, i: (bi, 0, 0), pipeline_mode=pl.Buffered(1))
    return pl.pallas_call(
        functools.partial(_sb_kernel, tq=tq),
        grid=(b, t // tq),
        in_specs=[pl.BlockSpec((1, tq, dh), lambda bi, i: (bi, i, 0)), seq, seq],
        out_specs=pl.BlockSpec((1, tq, dh), lambda bi, i: (bi, i, 0)),
        out_shape=SDS((b, t, dh), F32),
        scratch_shapes=[pltpu.VMEM((N_HEADS, tq, tq), F32), pltpu.VMEM((N_HEADS, tq, tq), F32),
                        pltpu.VMEM((N_HEADS, tq, LANES), F32),
                        pltpu.VMEM((tq, dh), F32)],
        compiler_params=_params("arbitrary", "arbitrary"),
        name="prompt_sb",
    )(q, k, v)


def _merge_kernel(ya_ref, yb_ref, gate_ref, x_ref, g1_ref, n_ref, wa_ref, wb_ref, wo_ref, o_ref):
    d = x_ref.shape[1]
    ma = _dot(ya_ref[...].astype(BF16), wa_ref[...])
    mb = _dot(yb_ref[...].astype(BF16), wb_ref[...])
    merged = gate_ref[:, :d].astype(F32) * ma + gate_ref[:, d:].astype(F32) * mb
    mix = _dot(merged.astype(BF16), wo_ref[...])
    o_ref[...] = x_ref[...] + g1_ref[0] * _rms(mix, n_ref[...])


def _mod_spec(mod, tm, rows_per_mod):
    d = mod.shape[2]
    if mod.shape[1] == 1:
        return pl.BlockSpec((1, 1, d), lambda i: ((i * tm) // rows_per_mod, 0, 0))
    return pl.BlockSpec((1, tm, d), lambda i: (0, i, 0))


def _merge(ya, yb, gate, x, g1, norm, wa, wb, wo, tm, rows_per_mod):
    m, d = x.shape
    row = lambda a: pl.BlockSpec((tm, a.shape[1]), lambda i: (i, 0))
    full = lambda a: pl.BlockSpec(a.shape, lambda i: (0, 0), pipeline_mode=pl.Buffered(1))
    return pl.pallas_call(
        _merge_kernel,
        grid=(m // tm,),
        in_specs=[row(ya), row(yb), row(gate), row(x), _mod_spec(g1, tm, rows_per_mod),
                  full(norm), full(wa), full(wb), full(wo)],
        out_specs=pl.BlockSpec((tm, d), lambda i: (i, 0)),
        out_shape=SDS((m, d), F32),
        compiler_params=_params("arbitrary"),
        name="merge",
    )(ya, yb, gate, x, g1, norm, wa, wb, wo)


def _ffn_kernel(x_ref, sc_ref, sh_ref, g2_ref, npre_ref, npost_ref, wg_ref, wu_ref, wd_ref, o_ref, *, chunk):
    x = x_ref[...]
    h = (_rms(x, npre_ref[...]) * (1.0 + sc_ref[0]) + sh_ref[0]).astype(BF16)
    f = jnp.zeros(x.shape, F32)
    for c0 in range(0, wg_ref.shape[1], chunk):
        gate = _dot(h, wg_ref[:, c0:c0 + chunk])
        up = _dot(h, wu_ref[:, c0:c0 + chunk])
        act = gate / (1.0 + jnp.exp(-gate)) * up
        f = f + _dot(act.astype(BF16), wd_ref[c0:c0 + chunk, :])
    o_ref[...] = x + g2_ref[0] * _rms(f, npost_ref[...])


def _ffn(x, sc, sh, g2, npre, npost, wg, wu, wd, tm, rows_per_mod):
    m, d = x.shape
    full = lambda a: pl.BlockSpec(a.shape, lambda i: (0, 0), pipeline_mode=pl.Buffered(1))
    mod = lambda a: _mod_spec(a, tm, rows_per_mod)
    return pl.pallas_call(
        functools.partial(_ffn_kernel, chunk=256),
        grid=(m // tm,),
        in_specs=[pl.BlockSpec((tm, d), lambda i: (i, 0)), mod(sc), mod(sh), mod(g2),
                  full(npre), full(npost), full(wg), full(wu), full(wd)],
        out_specs=pl.BlockSpec((tm, d), lambda i: (i, 0)),
        out_shape=SDS((m, d), F32),
        compiler_params=_params("arbitrary"),
        name="ffn",
    )(x, sc, sh, g2, npre, npost, wg, wu, wd)


def _sample_score_kernel(pt_ref, q_ref, w_ref, knew_ref, *rest, pages):
    page_refs = rest[:pages]
    past_ref, new_ref = rest[pages:]
    q_hi, q_lo = _split2(q_ref[0])
    w = w_ref[0]
    t = q_ref.shape[1] // N_IDX_HEADS

    def score(keys):
        k_hi, k_lo = _split2(keys)
        r = jnp.maximum(_dot3(q_hi, q_lo, k_hi, k_lo), 0.0) * w
        s = r[0:t]
        for h in range(1, N_IDX_HEADS):
            s = s + r[h * t:(h + 1) * t]
        return s

    for n in range(pages):
        past_ref[0, :, n * LANES:(n + 1) * LANES] = score(page_refs[n][...])
    new_ref[0] = score(knew_ref[0])


def _sample_scores(page_table, q_rows, w_rows, k_new, cache_k_idx, pages):
    nb, n_pages = page_table.shape
    page = cache_k_idx.shape[2]
    assert page == LANES
    rows = q_rows.shape[1]
    t = rows // N_IDX_HEADS

    def page_spec(n):
        return pl.BlockSpec((None, None, IDX_DIM, page), lambda b, g, pt: (0, pt[b, g * pages + n], 0, 0))

    grid_spec = pltpu.PrefetchScalarGridSpec(
        num_scalar_prefetch=1,
        grid=(nb, n_pages // pages),
        in_specs=[pl.BlockSpec((1, rows, IDX_DIM), lambda b, g, pt: (b, 0, 0)),
                  pl.BlockSpec((1, rows, LANES), lambda b, g, pt: (b, 0, 0)),
                  pl.BlockSpec((1, IDX_DIM, page), lambda b, g, pt: (b, 0, 0))]
                 + [page_spec(n) for n in range(pages)],
        out_specs=[pl.BlockSpec((1, t, pages * page), lambda b, g, pt: (b, 0, g)),
                   pl.BlockSpec((1, t, page), lambda b, g, pt: (b, 0, 0))],
    )
    return pl.pallas_call(
        functools.partial(_sample_score_kernel, pages=pages),
        grid_spec=grid_spec,
        out_shape=[SDS((nb, t, n_pages * page), F32), SDS((nb, t, page), F32)],
        compiler_params=_params("arbitrary", "arbitrary"),
        name="sample_scores",
    )(page_table, q_rows, w_rows, k_new, *([cache_k_idx.transpose(0, 1, 3, 2)] * pages))


def _sample_select_kernel(past_ref, new_ref, mpast_ref, mnew_ref, score_ref, thr_ref, need_ref, *, t, topk):
    rows, past = past_ref.shape
    score_ref[:, :past] = past_ref[...]
    q_t = lax.broadcasted_iota(I32, (rows, LANES), 0) % t
    col = lax.broadcasted_iota(I32, (rows, LANES), 1)
    score_ref[:, past:] = jnp.where(col <= q_t, new_ref[...], -jnp.inf)

    def emit(j, sel):
        mpast_ref[:, pl.ds(pl.multiple_of(j * LANES, LANES), LANES)] = sel

    def emit_new(j, sel):
        mnew_ref[...] = sel

    width = score_ref.shape[1]
    gw = _group_width(width)
    _emit_selection(score_ref, thr_ref, need_ref, width // gw, gw, width // LANES, LANES, topk, emit, emit_new)


def _sample_select(s_past, s_new, t, topk, tr):
    rows, past = s_past.shape
    width = past + LANES
    return pl.pallas_call(
        functools.partial(_sample_select_kernel, t=t, topk=float(topk)),
        grid=(rows // tr,),
        in_specs=[pl.BlockSpec((tr, past), lambda i: (i, 0)), pl.BlockSpec((tr, LANES), lambda i: (i, 0))],
        out_specs=[pl.BlockSpec((tr, past), lambda i: (i, 0)), pl.BlockSpec((tr, LANES), lambda i: (i, 0))],
        out_shape=[SDS((rows, past), F32), SDS((rows, LANES), F32)],
        scratch_shapes=[pltpu.VMEM((tr, width), F32), pltpu.VMEM((tr, 1), F32), pltpu.VMEM((tr, 1), F32)],
        compiler_params=_params("arbitrary"),
        name="sample_select",
    )(s_past, s_new)


def _sample_bias_kernel(rb_ref, o_ref, *, t, page):
    key = lax.broadcasted_iota(I32, (page, N_HEADS * t), 0)
    col = lax.broadcasted_iota(I32, (page, N_HEADS * t), 1)
    q_t = col % t
    head = col // t
    for tile in range(2):
        bucket = _t5_bucket(q_t - key + tile * page)
        val = jnp.zeros(bucket.shape, F32)
        for h in range(N_HEADS):
            val = jnp.where(head == h, _bias_lookup(rb_ref, bucket, h), val)
        o_ref[tile] = val
    far = jnp.zeros(bucket.shape, F32)
    for h in range(N_HEADS):
        far = jnp.where(head == h, rb_ref[N_BUCKETS - 1, h], far)
    o_ref[2] = far


def _sample_bias(rel_bias, t, page):
    assert page >= MAX_DISTANCE
    return pl.pallas_call(
        functools.partial(_sample_bias_kernel, t=t, page=page),
        in_specs=[pl.BlockSpec(memory_space=pltpu.SMEM)],
        out_shape=SDS((3, page, N_HEADS * t), F32),
        name="sample_bias",
    )(rel_bias)


def _to_column(row_vec, n):
    eye = lax.broadcasted_iota(I32, (n, n), 0) == lax.broadcasted_iota(I32, (n, n), 1)
    return jnp.sum(jnp.where(eye, row_vec, 0.0), axis=1, keepdims=True)


def _sample_attn_kernel(pt_ref, qa_ref, qb_ref, knew_a_ref, vnew_a_ref, knew_b_ref, vnew_b_ref,
                        mnew_ref, mask_ref, bias_ref, *rest, pages, t):
    ka_refs = rest[0 * pages:1 * pages]
    va_refs = rest[1 * pages:2 * pages]
    kb_refs = rest[2 * pages:3 * pages]
    vb_refs = rest[3 * pages:4 * pages]
    ya_ref, yb_ref, m_ref, l_ref, acca_ref, c_ref, accb_ref = rest[4 * pages:]
    g = pl.program_id(1)
    n_groups = pl.num_programs(1)
    page = LANES
    cols = N_HEADS * t
    qa = qa_ref[0]
    qb = qb_ref[0]
    srow = lax.broadcasted_iota(I32, (page, page), 0)
    scol = lax.broadcasted_iota(I32, (page, page), 1)
    later = jnp.where(scol > srow, 1.0, 0.0).astype(BF16)

    def dsa_logits(k, bias):
        return _dot(k.astype(BF16), qa) + bias

    def dsa_update(blocks):
        m_old = m_ref[...]
        m_new = m_old
        for logits, valid, _ in blocks:
            m_new = jnp.maximum(m_new, jnp.max(jnp.where(valid, logits, NEG), axis=0, keepdims=True))
        alpha = jnp.exp(m_old - m_new)
        l_new = alpha * l_ref[...]
        contrib = jnp.zeros(acca_ref.shape, F32)
        for logits, valid, v in blocks:
            pexp = jnp.where(valid, jnp.exp(logits - m_new), 0.0)
            l_new = l_new + jnp.sum(pexp, axis=0, keepdims=True)
            contrib = contrib + _dot(pexp.T.astype(BF16), v.astype(BF16))
        m_ref[...] = m_new
        l_ref[...] = l_new
        acca_ref[...] = _to_column(alpha, cols) * acca_ref[...] + contrib

    def sb_block(k, v, earlier):
        z = _dot(k.astype(BF16), qb)
        lsm = _log_one_minus_beta(z)
        if earlier is not None:
            lsm = jnp.where(earlier, lsm, 0.0)
        hi, lo = _split2(lsm)
        between = _dot(later, hi) + _dot(later, lo)
        a = jnp.exp(z + lsm + between + c_ref[...])
        if earlier is not None:
            a = jnp.where(earlier, a, 0.0)
        c_ref[...] = c_ref[...] + between[:1, :] + lsm[:1, :]
        accb_ref[...] = accb_ref[...] + _dot(a.T.astype(BF16), v.astype(BF16))

    @pl.when(g == 0)
    def _():
        m_ref[...] = jnp.full(m_ref.shape, NEG, F32)
        l_ref[...] = jnp.zeros(l_ref.shape, F32)
        acca_ref[...] = jnp.zeros(acca_ref.shape, F32)
        c_ref[...] = jnp.zeros(c_ref.shape, F32)
        accb_ref[...] = jnp.zeros(accb_ref.shape, F32)
        key = lax.broadcasted_iota(I32, (page, cols), 0)
        q_t = lax.broadcasted_iota(I32, (page, cols), 1) % t
        valid = mnew_ref[0].astype(I32) != 0
        dsa_update([(dsa_logits(knew_a_ref[0], bias_ref[0]), valid, vnew_a_ref[0])])
        sb_block(knew_b_ref[0], vnew_b_ref[0], key < q_t)

    blocks = []
    for n in range(pages):
        valid = mask_ref[0, (pages - 1 - n) * page:(pages - n) * page, :].astype(I32) != 0
        last_page = jnp.logical_and(g == 0, n == 0)
        bias = jnp.where(last_page, bias_ref[1], bias_ref[2])
        blocks.append((dsa_logits(ka_refs[n][...], bias), valid, va_refs[n][...]))
    dsa_update(blocks)
    for n in range(pages):
        sb_block(kb_refs[n][...], vb_refs[n][...], None)

    @pl.when(g == n_groups - 1)
    def _():
        head_of_col = lax.broadcasted_iota(I32, (t, N_HEADS * HEAD_DIM), 1) // HEAD_DIM
        norm = acca_ref[...] / _to_column(l_ref[...], cols)
        ya = jnp.zeros((t, N_HEADS * HEAD_DIM), F32)
        yb = jnp.zeros((t, N_HEADS * HEAD_DIM), F32)
        for h in range(N_HEADS):
            ya = ya + jnp.where(head_of_col == h, norm[h * t:(h + 1) * t], 0.0)
            yb = yb + jnp.where(head_of_col == h, accb_ref[h * t:(h + 1) * t, :], 0.0)
        ya_ref[0] = ya
        yb_ref[0] = yb


def _sample_attn(page_table, qa_bd, qb_bd, knew_a, vnew_a, knew_b, vnew_b, mask_new, mask_past, bias,
                 cache_k_a, cache_v_a, cache_k_b, cache_v_b, pages, t):
    nb, n_pages = page_table.shape
    page = LANES
    dh = N_HEADS * HEAD_DIM
    cols = N_HEADS * t
    n_groups = n_pages // pages

    def page_spec(n):
        return pl.BlockSpec((None, None, page, dh),
                            lambda b, g, pt: (0, pt[b, n_pages - 1 - (g * pages + n)], 0, 0))

    per_seq = lambda shape: pl.BlockSpec((1,) + shape, lambda b, g, pt: (b, 0, 0))
    grid_spec = pltpu.PrefetchScalarGridSpec(
        num_scalar_prefetch=1,
        grid=(nb, n_groups),
        in_specs=[per_seq((dh, cols)), per_seq((dh, cols)),
                  per_seq((page, dh)), per_seq((page, dh)), per_seq((page, dh)), per_seq((page, dh)),
                  per_seq((page, cols)),
                  pl.BlockSpec((1, pages * page, cols), lambda b, g, pt: (b, n_groups - 1 - g, 0)),
                  pl.BlockSpec(bias.shape, lambda b, g, pt: (0, 0, 0))]
                 + [page_spec(n) for n in range(pages)] * 4,
        out_specs=[per_seq((t, dh)), per_seq((t, dh))],
        scratch_shapes=[pltpu.VMEM((1, cols), F32), pltpu.VMEM((1, cols), F32), pltpu.VMEM((cols, dh), F32),
                        pltpu.VMEM((1, cols), F32), pltpu.VMEM((cols, dh), F32)],
    )
    flat = lambda c: c.reshape(c.shape[0], c.shape[1], c.shape[2], dh)
    return pl.pallas_call(
        functools.partial(_sample_attn_kernel, pages=pages, t=t),
        grid_spec=grid_spec,
        out_shape=[SDS((nb, t, dh), F32), SDS((nb, t, dh), F32)],
        compiler_params=_params("arbitrary", "arbitrary"),
        name="sample_attn",
    )(page_table, qa_bd, qb_bd, knew_a, vnew_a, knew_b, vnew_b, mask_new, mask_past, bias,
      *([flat(cache_k_a)] * pages), *([flat(cache_v_a)] * pages),
      *([flat(cache_k_b)] * pages), *([flat(cache_v_b)] * pages))


def _sample_bias2_kernel(rb_ref, o_ref, *, t, page):
    rows, width = N_HEADS * t, page * N_HEADS
    row = lax.broadcasted_iota(I32, (rows, width), 0)
    lane = lax.broadcasted_iota(I32, (rows, width), 1)
    q_t, head = row % t, row // t
    key = lane // N_HEADS
    own = (lane % N_HEADS) == head
    for tile in range(3):
        bucket = _t5_bucket(q_t - key + tile * page)
        val = jnp.zeros((rows, width), F32)
        for h in range(N_HEADS):
            val = jnp.where(head == h, _bias_lookup(rb_ref, bucket, h), val)
        o_ref[tile] = jnp.where(own, val, NEG)


def _sample_bias2(rel_bias, t, page):
    assert page >= MAX_DISTANCE
    return pl.pallas_call(
        functools.partial(_sample_bias2_kernel, t=t, page=page),
        in_specs=[pl.BlockSpec(memory_space=pltpu.SMEM)],
        out_shape=SDS((3, N_HEADS * t, page * N_HEADS), F32),
        name="sample_bias",
    )(rel_bias)


def _sample_attn2_kernel(pt_ref, qa_ref, qb_ref, knew_a_ref, vnew_a_ref, knew_b_ref, vnew_b_ref,
                         mnew_ref, mask_ref, bias_ref, *rest, pages, t):
    ka_refs = rest[0 * pages:1 * pages]
    va_refs = rest[1 * pages:2 * pages]
    kb_refs = rest[2 * pages:3 * pages]
    vb_refs = rest[3 * pages:4 * pages]
    ya_ref, yb_ref, m_ref, l_ref, acca_ref, c_ref, accb_ref = rest[4 * pages:]
    g = pl.program_id(1)
    n_groups = pl.num_programs(1)
    rows = N_HEADS * t
    width = LANES * N_HEADS
    qa = qa_ref[0]
    qb = qb_ref[0]
    row = lax.broadcasted_iota(I32, (rows, width), 0)
    lane = lax.broadcasted_iota(I32, (rows, width), 1)
    own = (lane % N_HEADS) == (row // t)
    own_neg = jnp.where(own, 0.0, NEG)
    own_one = jnp.where(own, 1.0, 0.0)
    srow = lax.broadcasted_iota(I32, (LANES, LANES), 0)
    scol = lax.broadcasted_iota(I32, (LANES, LANES), 1)
    not_before = jnp.where(srow >= scol, 1.0, 0.0).astype(BF16)
    not_before = jnp.concatenate([not_before, not_before], axis=0)

    spread = lax.broadcasted_iota(I32, (LANES, width), 1) // N_HEADS == lax.broadcasted_iota(I32, (LANES, width), 0)
    spread = jnp.where(spread, 1.0, 0.0).astype(BF16)

    def page_rows(ref):
        return ref[...].reshape(width, HEAD_DIM)

    def dsa_logits(k, bias, flags):
        chosen = _dot(jnp.concatenate([flags] * N_HEADS, axis=0).astype(BF16), spread)
        return _dot_nt(qa, k.astype(BF16)) + bias + (1.0 - chosen) * NEG

    def dsa_update(blocks):
        m_old = m_ref[...]
        m_new = m_old
        for logits, _ in blocks:
            m_new = jnp.maximum(m_new, jnp.max(logits, axis=1, keepdims=True))
        alpha = jnp.exp(m_old - m_new)
        l_new = alpha * l_ref[...]
        contrib = jnp.zeros(acca_ref.shape, F32)
        for logits, v in blocks:
            pexp = jnp.exp(logits - m_new)
            l_new = l_new + jnp.sum(pexp, axis=1, keepdims=True)
            contrib = contrib + _dot(pexp.astype(BF16), v.astype(BF16))
        m_ref[...] = m_new
        l_ref[...] = l_new
        acca_ref[...] = alpha * acca_ref[...] + contrib

    def sb_update(blocks):
        n_lane_blocks = width // LANES
        zs = [_dot_nt(qb, k.astype(BF16)) for k, _, _, _ in blocks]
        inners = []
        for z, (_, _, live_one, _) in zip(zs, blocks):
            sp = (jnp.maximum(z, 0.0) + jnp.log(1.0 + jnp.exp(-jnp.abs(z)))) * live_one
            hi, lo = _split2(sp)
            inners.append([_dot(jnp.concatenate([hi[:, n * LANES:(n + 1) * LANES],
                                                 lo[:, n * LANES:(n + 1) * LANES]], axis=1), not_before)
                           for n in range(n_lane_blocks)])
        carry = c_ref[...]
        contrib = jnp.zeros(accb_ref.shape, F32)
        for z, inner, (_, v, _, live_neg) in zip(zs, inners, blocks):
            tails = [None] * n_lane_blocks
            for n in reversed(range(n_lane_blocks)):
                tails[n] = inner[n] + carry
                carry = carry + inner[n][:, :1]
            a = jnp.exp(z - jnp.concatenate(tails, axis=1) + live_neg)
            contrib = contrib + _dot(a.astype(BF16), v.astype(BF16))
        c_ref[...] = carry
        accb_ref[...] = accb_ref[...] + contrib

    @pl.when(g == 0)
    def _():
        m_ref[...] = jnp.full(m_ref.shape, NEG, F32)
        l_ref[...] = jnp.zeros(l_ref.shape, F32)
        acca_ref[...] = jnp.zeros(acca_ref.shape, F32)
        c_ref[...] = jnp.zeros(c_ref.shape, F32)
        accb_ref[...] = jnp.zeros(accb_ref.shape, F32)
        earlier = jnp.logical_and(own, lane // N_HEADS < row % t)
        dsa_update([(dsa_logits(knew_a_ref[0], bias_ref[0], mnew_ref[0]), vnew_a_ref[0])])
        sb_update([(knew_b_ref[0], vnew_b_ref[0], jnp.where(earlier, 1.0, 0.0), jnp.where(earlier, 0.0, NEG))])

    blocks = []
    for n in range(pages):
        flags = mask_ref[0, :, (pages - 1 - n) * LANES:(pages - n) * LANES]
        bias = bias_ref[2] if n else jnp.where(g == 0, bias_ref[1], bias_ref[2])
        blocks.append((dsa_logits(page_rows(ka_refs[n]), bias, flags), page_rows(va_refs[n])))
    dsa_update(blocks)
    sb_update([(page_rows(kb_refs[n]), page_rows(vb_refs[n]), own_one, own_neg) for n in range(pages)])

    @pl.when(g == n_groups - 1)
    def _():
        ya_ref[0] = acca_ref[...] / l_ref[...]
        yb_ref[0] = accb_ref[...]


def _sample_attn2(page_table, qa_rows, qb_rows, knew_a, vnew_a, knew_b, vnew_b, madd_new, madd_past, bias,
                  cache_k_a, cache_v_a, cache_k_b, cache_v_b, pages, t):
    nb, n_pages = page_table.shape
    rows = N_HEADS * t
    width = LANES * N_HEADS
    n_groups = n_pages // pages

    def page_spec(n):
        return pl.BlockSpec((None, None, LANES, N_HEADS, HEAD_DIM),
                            lambda b, g, pt: (0, pt[b, n_pages - 1 - (g * pages + n)], 0, 0, 0))

    per_seq = lambda shape: pl.BlockSpec((1,) + shape, lambda b, g, pt: (b, 0, 0))
    grid_spec = pltpu.PrefetchScalarGridSpec(
        num_scalar_prefetch=1,
        grid=(nb, n_groups),
        in_specs=[per_seq((rows, HEAD_DIM)), per_seq((rows, HEAD_DIM)),
                  per_seq((width, HEAD_DIM)), per_seq((width, HEAD_DIM)),
                  per_seq((width, HEAD_DIM)), per_seq((width, HEAD_DIM)),
                  per_seq((t, LANES)),
                  pl.BlockSpec((1, t, pages * LANES), lambda b, g, pt: (b, 0, n_groups - 1 - g)),
                  pl.BlockSpec(bias.shape, lambda b, g, pt: (0, 0, 0))]
                 + [page_spec(n) for n in range(pages)] * 4,
        out_specs=[per_seq((rows, HEAD_DIM)), per_seq((rows, HEAD_DIM))],
        scratch_shapes=[pltpu.VMEM((rows, 1), F32), pltpu.VMEM((rows, 1), F32), pltpu.VMEM((rows, HEAD_DIM), F32),
                        pltpu.VMEM((rows, 1), F32), pltpu.VMEM((rows, HEAD_DIM), F32)],
    )
    return pl.pallas_call(
        functools.partial(_sample_attn2_kernel, pages=pages, t=t),
        grid_spec=grid_spec,
        out_shape=[SDS((nb, rows, HEAD_DIM), F32), SDS((nb, rows, HEAD_DIM), F32)],
        compiler_params=_params("arbitrary", "arbitrary"),
        name="sample_attn",
    )(page_table, qa_rows, qb_rows, knew_a, vnew_a, knew_b, vnew_b, madd_new, madd_past, bias,
      *([cache_k_a] * pages), *([cache_v_a] * pages), *([cache_k_b] * pages), *([cache_v_b] * pages))


def _sample_bias3_kernel(rb_ref, o_ref, *, t, page):
    rows = N_HEADS * t
    row = lax.broadcasted_iota(I32, (rows, page), 0)
    key = lax.broadcasted_iota(I32, (rows, page), 1)
    q_t, head = row % t, row // t
    for tile in range(3):
        bucket = _t5_bucket(q_t - key + tile * page)
        val = jnp.zeros((rows, page), F32)
        for h in range(N_HEADS):
            val = jnp.where(head == h, _bias_lookup(rb_ref, bucket, h), val)
        o_ref[tile] = val


def _sample_bias3(rel_bias, t, page):
    assert page >= MAX_DISTANCE
    return pl.pallas_call(
        functools.partial(_sample_bias3_kernel, t=t, page=page),
        in_specs=[pl.BlockSpec(memory_space=pltpu.SMEM)],
        out_shape=SDS((3, N_HEADS * t, page), F32),
        name="sample_bias",
    )(rel_bias)


def _sample_attn3_kernel(pt_ref, qa_ref, qb_ref, knew_a_ref, vnew_a_ref, knew_b_ref, vnew_b_ref,
                         fnew_ref, flags_ref, bias_ref, *rest, pages, t):
    ka_refs = rest[0 * pages:1 * pages]
    va_refs = rest[1 * pages:2 * pages]
    kb_refs = rest[2 * pages:3 * pages]
    vb_refs = rest[3 * pages:4 * pages]
    ya_ref, yb_ref, m_ref, l_ref, acca_ref, c_ref, accb_ref = rest[4 * pages:]
    g = pl.program_id(1)
    n_groups = pl.num_programs(1)
    rows = N_HEADS * t
    qa = qa_ref[0]
    qb = qb_ref[0]
    srow = lax.broadcasted_iota(I32, (LANES, LANES), 0)
    scol = lax.broadcasted_iota(I32, (LANES, LANES), 1)
    not_before = jnp.where(srow >= scol, 1.0, 0.0).astype(BF16)
    not_before = jnp.concatenate([not_before, not_before], axis=0)

    def scores(q, k):
        return jnp.concatenate([_dot(q[h * t:(h + 1) * t], k[h].astype(BF16)) for h in range(N_HEADS)], axis=0)

    def weighted(p, v):
        return jnp.concatenate([_dot_nt(p[h * t:(h + 1) * t].astype(BF16), v[h].astype(BF16))
                                for h in range(N_HEADS)], axis=0)

    def dsa_update(blocks):
        logits = [scores(qa, k) + bias + (1.0 - jnp.concatenate([flags] * N_HEADS, axis=0)) * NEG
                  for k, _, bias, flags in blocks]
        m_old = m_ref[...]
        m_new = m_old
        for lg in logits:
            m_new = jnp.maximum(m_new, jnp.max(lg, axis=1, keepdims=True))
        alpha = jnp.exp(m_old - m_new)
        l_new = alpha * l_ref[...]
        contrib = jnp.zeros(acca_ref.shape, F32)
        for lg, (_, v, _, _) in zip(logits, blocks):
            pexp = jnp.exp(lg - m_new)
            l_new = l_new + jnp.sum(pexp, axis=1, keepdims=True)
            contrib = contrib + weighted(pexp, v)
        m_ref[...] = m_new
        l_ref[...] = l_new
        acca_ref[...] = alpha * acca_ref[...] + contrib

    def sb_update(blocks):
        zs = [scores(qb, k) for k, _, _ in blocks]
        tails = []
        for z, (_, _, earlier) in zip(zs, blocks):
            sp = jnp.maximum(z, 0.0) + jnp.log(1.0 + jnp.exp(-jnp.abs(z)))
            if earlier is not None:
                sp = jnp.where(earlier, sp, 0.0)
            hi, lo = _split2(sp)
            tails.append(_dot(jnp.concatenate([hi, lo], axis=1), not_before))
        carry = c_ref[...]
        contrib = jnp.zeros(accb_ref.shape, F32)
        for z, tail, (_, v, earlier) in zip(zs, tails, blocks):
            a = jnp.exp(z - tail - carry)
            if earlier is not None:
                a = jnp.where(earlier, a, 0.0)
            carry = carry + tail[:, :1]
            contrib = contrib + weighted(a, v)
        c_ref[...] = carry
        accb_ref[...] = accb_ref[...] + contrib

    @pl.when(g == 0)
    def _():
        m_ref[...] = jnp.full(m_ref.shape, NEG, F32)
        l_ref[...] = jnp.zeros(l_ref.shape, F32)
        acca_ref[...] = jnp.zeros(acca_ref.shape, F32)
        c_ref[...] = jnp.zeros(c_ref.shape, F32)
        accb_ref[...] = jnp.zeros(accb_ref.shape, F32)
        key = lax.broadcasted_iota(I32, (rows, LANES), 1)
        q_t = lax.broadcasted_iota(I32, (rows, LANES), 0) % t
        dsa_update([(knew_a_ref[0], vnew_a_ref[0], bias_ref[0], fnew_ref[0])])
        sb_update([(knew_b_ref[0], vnew_b_ref[0], key < q_t)])

    blocks = []
    for n in range(pages):
        flags = flags_ref[0, :, (pages - 1 - n) * LANES:(pages - n) * LANES]
        bias = bias_ref[2] if n else jnp.where(g == 0, bias_ref[1], bias_ref[2])
        blocks.append((ka_refs[n][...], va_refs[n][...], bias, flags))
    dsa_update(blocks)
    sb_update([(kb_refs[n][...], vb_refs[n][...], None) for n in range(pages)])

    @pl.when(g == n_groups - 1)
    def _():
        ya_ref[0] = acca_ref[...] / l_ref[...]
        yb_ref[0] = accb_ref[...]


def _sample_attn3(page_table, qa_rows, qb_rows, knew_a, vnew_a, knew_b, vnew_b, flags_new, flags_past, bias,
                  cache_k_a, cache_v_a, cache_k_b, cache_v_b, pages, t):
    nb, n_pages = page_table.shape
    rows = N_HEADS * t
    n_groups = n_pages // pages
    page_shape = (N_HEADS, HEAD_DIM, LANES)

    def page_spec(n):
        return pl.BlockSpec((None, None) + page_shape,
                            lambda b, g, pt: (0, pt[b, n_pages - 1 - (g * pages + n)], 0, 0, 0))

    per_seq = lambda shape: pl.BlockSpec((1,) + shape, lambda b, g, pt: (b,) + (0,) * len(shape))
    grid_spec = pltpu.PrefetchScalarGridSpec(
        num_scalar_prefetch=1,
        grid=(nb, n_groups),
        in_specs=[per_seq((rows, HEAD_DIM)), per_seq((rows, HEAD_DIM)),
                  per_seq(page_shape), per_seq(page_shape), per_seq(page_shape), per_seq(page_shape),
                  per_seq((t, LANES)),
                  pl.BlockSpec((1, t, pages * LANES), lambda b, g, pt: (b, 0, n_groups - 1 - g)),
                  pl.BlockSpec(bias.shape, lambda b, g, pt: (0, 0, 0))]
                 + [page_spec(n) for n in range(pages)] * 4,
        out_specs=[per_seq((rows, HEAD_DIM)), per_seq((rows, HEAD_DIM))],
        scratch_shapes=[pltpu.VMEM((rows, 1), F32), pltpu.VMEM((rows, 1), F32), pltpu.VMEM((rows, HEAD_DIM), F32),
                        pltpu.VMEM((rows, 1), F32), pltpu.VMEM((rows, HEAD_DIM), F32)],
    )
    key_minor = lambda c: c.transpose(0, 1, 3, 4, 2)
    return pl.pallas_call(
        functools.partial(_sample_attn3_kernel, pages=pages, t=t),
        grid_spec=grid_spec,
        out_shape=[SDS((nb, rows, HEAD_DIM), F32), SDS((nb, rows, HEAD_DIM), F32)],
        compiler_params=_params("arbitrary", "arbitrary"),
        name="sample_attn",
    )(page_table, qa_rows, qb_rows, knew_a, vnew_a, knew_b, vnew_b, flags_new, flags_past, bias,
      *([key_minor(cache_k_a)] * pages), *([key_minor(cache_v_a)] * pages),
      *([key_minor(cache_k_b)] * pages), *([key_minor(cache_v_b)] * pages))


def _largest_divisor(n, cap):
    d = min(n, cap)
    while n % d:
        d -= 1
    return d


def _block_diag_queries(q, nb, t):
    q4 = q.astype(F32).reshape(nb, t, N_HEADS, HEAD_DIM)
    eye = jnp.eye(N_HEADS, dtype=F32)
    bd = jnp.einsum('bthd,hg->bhdgt', q4, eye)
    return bd.reshape(nb, N_HEADS * HEAD_DIM, N_HEADS * t).astype(BF16)


def kernel(x_prompt, x_sample, c_prompt, c_sample, cache_k_a, cache_v_a, cache_k_idx, cache_k_b, cache_v_b,
           page_table, rel_bias, w_ada, b_ada, norm_pre_mix, norm_post_mix, norm_pre_ffn, norm_post_ffn,
           w_in, w_br_a, w_br_b, w_out, w_ffn_gate, w_ffn_up, w_ffn_down):
    depth = w_in.shape[0]
    assert depth == 1
    nb_p, t_p, d = x_prompt.shape
    nb_s, t_s, _ = x_sample.shape
    n_pages = page_table.shape[1]
    page = cache_k_a.shape[2]
    past = n_pages * page
    dh = N_HEADS * HEAD_DIM
    nq = N_IDX_HEADS * IDX_DIM
    topk_p = min(INDEX_TOPK, t_p // 4)
    topk_s = min(INDEX_TOPK, (past + t_s) // 4)
    tq = 256
    assert t_p % tq == 0 and topk_p <= tq and t_s * N_HEADS <= LANES and page == LANES

    w = w_in[0]
    o = np.cumsum([0, dh, dh, dh, nq, IDX_DIM, N_IDX_HEADS, dh, dh, dh, d, d])
    col = lambda n: w[:, o[n]:o[n + 1]]
    w_main = jnp.concatenate([col(0), col(1), col(2), col(6), col(7), col(8)], axis=1).astype(BF16)
    w_gate = jnp.concatenate([col(9), col(10)], axis=1).astype(BF16)
    w_idx = jnp.concatenate([col(3), col(4), col(4), col(5),
                             jnp.zeros((d, LANES - N_IDX_HEADS), F32)], axis=1)
    w_idx_hi = w_idx.astype(BF16)
    w_idx_lo = (w_idx - w_idx_hi.astype(F32)).astype(BF16)
    wa, wb, wo = w_br_a[0].astype(BF16), w_br_b[0].astype(BF16), w_out[0].astype(BF16)
    wg, wu, wd = w_ffn_gate[0].astype(BF16), w_ffn_up[0].astype(BF16), w_ffn_down[0].astype(BF16)
    n_pre_mix, n_post_mix = norm_pre_mix[0][None], norm_post_mix[0][None]
    n_pre_ffn, n_post_ffn = norm_pre_ffn[0][None], norm_post_ffn[0][None]

    n_c = nb_p + nb_s
    c_all = jnp.concatenate([c_prompt, c_sample, jnp.zeros((-n_c % 8, d), F32)], axis=0)
    ada = _ada(c_all, w_ada[0], b_ada[0][None])
    mods_p = [m[:, None, :] for m in jnp.split(ada[:nb_p], 6, axis=-1)]
    mods_s = [jnp.repeat(m, t_s, axis=0)[None] for m in jnp.split(ada[nb_p:n_c], 6, axis=-1)]

    def dense_front(x2, mods, tm, rows_per_mod, seq_len, q_scale):
        return _inproj(x2, mods[1], mods[0], n_pre_mix, w_main, w_gate, w_idx_hi, w_idx_lo, tm, rows_per_mod,
                       seq_len, q_scale)

    def dense_back(ya, yb, gate, x2, mods, tm, rows_per_mod):
        x1 = _merge(ya, yb, gate, x2, mods[2], n_post_mix, wa, wb, wo, tm, rows_per_mod)
        return _ffn(x1, mods[4], mods[3], mods[5], n_pre_ffn, n_post_ffn, wg, wu, wd, tm, rows_per_mod)

    m_p = nb_p * t_p
    xp2 = x_prompt.reshape(m_p, d)
    tm_p = 512
    (qa, ka, va, qb, kb, vb, kab, vab, kbb, vbb, gate, iq, ik, wi, kidx, wit) = dense_front(
        xp2, mods_p, tm_p, t_p, t_p, LOG2E * HEAD_DIM ** -0.5)
    seq3 = lambda a: a.reshape(nb_p, t_p, a.shape[-1])
    mask = _prompt_index(seq3(iq), wit, seq3(ik), tq, topk_p)
    ya = _prompt_dsa(seq3(qa), seq3(kab), seq3(vab), mask, _bias_tiles(rel_bias, tq), tq)
    yb = _prompt_sb(seq3(qb), seq3(kbb), seq3(vbb), tq)
    y_p = dense_back(ya.reshape(m_p, dh), yb.reshape(m_p, dh), gate, xp2, mods_p, tm_p, t_p)
    heads_p = lambda a: a.reshape(nb_p, N_HEADS, HEAD_DIM, t_p).transpose(0, 3, 1, 2)[None]
    outs_p = (heads_p(ka), heads_p(va), kidx.transpose(0, 2, 1)[None], heads_p(kb), heads_p(vb))

    m_s = nb_s * t_s
    xs2 = x_sample.reshape(m_s, d)
    tm_s = _largest_divisor(m_s, 256)
    (qa, ka, va, qb, kb, vb, _, _, _, _, gate, iq, ik, wi, kidx, _) = dense_front(
        xs2, mods_s, tm_s, t_s, m_s, HEAD_DIM ** -0.5)
    q_rows = iq.reshape(nb_s, t_s, N_IDX_HEADS, IDX_DIM).transpose(0, 2, 1, 3).reshape(nb_s, N_IDX_HEADS * t_s, IDX_DIM)
    w_rows = wi.reshape(nb_s, t_s, LANES)[:, :, :N_IDX_HEADS].transpose(0, 2, 1).reshape(nb_s, N_IDX_HEADS * t_s, 1)
    w_rows = jnp.broadcast_to(w_rows, (nb_s, N_IDX_HEADS * t_s, LANES))
    pages_idx = _largest_divisor(n_pages, 16)
    k_new = jnp.pad(kidx[0].reshape(IDX_DIM, nb_s, t_s).transpose(1, 0, 2), ((0, 0), (0, 0), (0, page - t_s)))
    s_past, s_new = _sample_scores(page_table, q_rows, w_rows, k_new, cache_k_idx, pages_idx)
    sel_past, sel_new = _sample_select(s_past.reshape(m_s, past), s_new.reshape(m_s, page), t_s, topk_s,
                                       _largest_divisor(m_s, 128))
    head_rows = lambda a: (a.reshape(nb_s, t_s, N_HEADS, HEAD_DIM).transpose(0, 2, 1, 3)
                           .reshape(nb_s, N_HEADS * t_s, HEAD_DIM))
    as_page = lambda a: jnp.pad(a[0].reshape(N_HEADS, HEAD_DIM, nb_s, t_s).transpose(2, 0, 1, 3),
                                ((0, 0), (0, 0), (0, 0), (0, page - t_s)))
    pages_attn = _largest_divisor(n_pages, 8)
    ya, yb = _sample_attn3(page_table, head_rows(qa), head_rows(qb),
                           as_page(ka), as_page(va), as_page(kb), as_page(vb),
                           sel_new.reshape(nb_s, t_s, page), sel_past.reshape(nb_s, t_s, past),
                           _sample_bias3(rel_bias, t_s, page),
                           cache_k_a, cache_v_a, cache_k_b, cache_v_b, pages_attn, t_s)
    token_rows = lambda y: (y.reshape(nb_s, N_HEADS, t_s, HEAD_DIM).transpose(0, 2, 1, 3).reshape(m_s, dh))
    y_s = dense_back(token_rows(ya), token_rows(yb), gate, xs2, mods_s, tm_s, t_s)
    heads_s = lambda a: a[0].T.reshape(depth, nb_s, t_s, N_HEADS, HEAD_DIM)
    outs_s = (heads_s(ka), heads_s(va), kidx[0].T.reshape(depth, nb_s, t_s, IDX_DIM), heads_s(kb), heads_s(vb))

    return (y_p.reshape(nb_p, t_p, d), y_s.reshape(nb_s, t_s, d)) + outs_p + outs_s
```

```python
import functools
import math

import numpy as np
import jax
import jax.numpy as jnp
from jax import lax
from jax.experimental import pallas as pl
from jax.experimental.pallas import tpu as pltpu

F32 = jnp.float32
BF16 = jnp.bfloat16
I32 = jnp.int32
SDS = jax.ShapeDtypeStruct

HEAD_DIM = 64
N_HEADS = 8
N_PAIRS = N_HEADS // 2
N_IDX_HEADS = 4
IDX_DIM = 64
INDEX_TOPK = 256
N_BUCKETS = 32
MAX_EXACT = N_BUCKETS // 2
MAX_DISTANCE = 128
EPS = 1e-6
LANES = 128
SUBLANES = 8
NEG = -1e30
LOG2E = 1.4426950408889634
EXP2_CLAMP = 126.0
VMEM_LIMIT = 56 * 1024 * 1024

_KEY_NEG_INF = np.int32(0x807FFFFF - 2 ** 32)


def _params(*sem):
    return pltpu.CompilerParams(dimension_semantics=sem, vmem_limit_bytes=VMEM_LIMIT)


def _dot(a, b):
    return jnp.dot(a, b, preferred_element_type=F32)


def _dot_nt(a, b):
    return lax.dot_general(a, b, (((1,), (1,)), ((), ())), preferred_element_type=F32)


def _split2(x):
    hi = x.astype(BF16)
    lo = (x - hi.astype(F32)).astype(BF16)
    return hi, lo


def _dot3(a_hi, a_lo, b_hi, b_lo):
    return _dot(a_hi, b_hi) + _dot(a_lo, b_hi) + _dot(a_hi, b_lo)


def _rms(x, g):
    var = jnp.mean(x * x, axis=-1, keepdims=True)
    return x * lax.rsqrt(var + EPS) * g


def _ada_kernel(c_ref, w_ref, b_ref, o_ref):
    c_hi, c_lo = _split2(c_ref[...])
    w_hi, w_lo = _split2(w_ref[...])
    o_ref[...] = _dot3(c_hi, c_lo, w_hi, w_lo) + b_ref[...]


def _ada(c, w, b):
    rows, d = c.shape
    n = w.shape[1]
    tn = 1024
    return pl.pallas_call(
        _ada_kernel,
        grid=(n // tn,),
        in_specs=[pl.BlockSpec((rows, d), lambda j: (0, 0)),
                  pl.BlockSpec((d, tn), lambda j: (0, j)),
                  pl.BlockSpec((1, tn), lambda j: (0, j))],
        out_specs=pl.BlockSpec((rows, tn), lambda j: (0, j)),
        out_shape=SDS((rows, n), F32),
        compiler_params=_params("arbitrary"),
        name="ada",
    )(c, w, b)


def _inproj_kernel(x_ref, sc_ref, sh_ref, g_ref, wm_ref, wg_ref, wih_ref, wil_ref,
                   qa_ref, ka_ref, va_ref, qb_ref, kb_ref, vb_ref,
                   kab_ref, vab_ref, kbb_ref, vbb_ref,
                   gate_ref, iq_ref, ik_ref, wi_ref, kidx_ref, wit_ref, *, q_scale):
    dh = N_HEADS * HEAD_DIM
    h = _rms(x_ref[...], g_ref[...]) * (1.0 + sc_ref[0]) + sh_ref[0]
    h_hi, h_lo = _split2(h)

    def seg(n):
        return _dot(h_hi, wm_ref[:, n * dh:(n + 1) * dh])

    qa_ref[...] = (seg(0) * q_scale).astype(BF16)
    ka = seg(1)
    ka_ref[0] = ka.T
    kab_ref[...] = ka.astype(BF16)
    va = seg(2)
    va_ref[0] = va.T
    vab_ref[...] = va.astype(BF16)
    qb_ref[...] = (seg(3) * q_scale).astype(BF16)
    kb = seg(4)
    kb_ref[0] = kb.T
    kbb_ref[...] = kb.astype(BF16)
    vb = seg(5)
    vb_ref[0] = vb.T
    vbb_ref[...] = vb.astype(BF16)

    d = x_ref.shape[1]
    for n in range(2):
        g = _dot(h_hi, wg_ref[:, n * d:(n + 1) * d])
        gate_ref[:, n * d:(n + 1) * d] = (1.0 / (1.0 + jnp.exp(-g))).astype(BF16)

    nq = N_IDX_HEADS * IDX_DIM
    idx = _dot3(h_hi, h_lo, wih_ref[...], wil_ref[...])
    iq_ref[...] = idx[:, :nq] * IDX_DIM ** -0.5
    ik_ref[...] = idx[:, nq:nq + LANES]
    kidx_ref[0] = idx[:, nq:nq + LANES].T[:IDX_DIM, :]
    wi = idx[:, nq + LANES:nq + 2 * LANES] * N_IDX_HEADS ** -0.5
    wi_ref[...] = wi
    wit_ref[...] = wi.T[:SUBLANES, :]


def _inproj(x, sc, sh, g, wm, wg, wih, wil, tm, rows_per_mod, seq_len, q_scale):
    m, d = x.shape
    dh = N_HEADS * HEAD_DIM
    r = sc.shape[1]
    if r == 1:
        mod_map = lambda i: ((i * tm) // rows_per_mod, 0, 0)
    else:
        mod_map = lambda i: (0, i, 0)
    row = lambda w: pl.BlockSpec((tm, w), lambda i: (i, 0))
    full = lambda a: pl.BlockSpec(a.shape, lambda i: (0, 0), pipeline_mode=pl.Buffered(1))
    nq = N_IDX_HEADS * IDX_DIM
    outs = [(dh, BF16), (dh, None), (dh, None), (dh, BF16), (dh, None), (dh, None),
            (dh, BF16), (dh, BF16), (dh, BF16), (dh, BF16),
            (2 * d, BF16), (nq, F32), (LANES, F32), (LANES, F32), (IDX_DIM, None)]
    tiles_per_seq = seq_len // tm
    token_minor = lambda w: pl.BlockSpec((1, w, tm), lambda i: (i // tiles_per_seq, 0, i % tiles_per_seq))
    return pl.pallas_call(
        functools.partial(_inproj_kernel, q_scale=q_scale),
        grid=(m // tm,),
        in_specs=[row(d), pl.BlockSpec((1, r, d), mod_map), pl.BlockSpec((1, r, d), mod_map),
                  full(g), full(wm), full(wg), full(wih), full(wil)],
        out_specs=[row(w) if dt else token_minor(w) for w, dt in outs]
                  + [pl.BlockSpec((SUBLANES, tm), lambda i: (0, i))],
        out_shape=[SDS((m, w), dt) if dt else SDS((m // seq_len, w, seq_len), F32) for w, dt in outs]
                  + [SDS((SUBLANES, m), F32)],
        compiler_params=_params("arbitrary"),
        name="inproj",
    )(x, sc, sh, g, wm, wg, wih, wil)


ROW_CHUNK = 64
MAX_GROUPS = 256


def _key_to_float(key):
    bits = jnp.where(key < 0, key ^ jnp.int32(0x7FFFFFFF), key)
    f = lax.bitcast_convert_type(bits, F32)
    return jnp.where(key < _KEY_NEG_INF, -jnp.inf, f)


def _float_to_key(f):
    bits = lax.bitcast_convert_type(f, I32)
    return jnp.where(bits < 0, bits ^ jnp.int32(0x7FFFFFFF), bits)


def _group_width(width):
    gw = min(width, 8 * LANES)
    while width % gw:
        gw -= LANES
    return gw


def _count(score_ref, r0, ngroups, gw, cf, strict):
    cfb = jnp.broadcast_to(cf, (ROW_CHUNK, LANES))

    def body(g, acc):
        base = pl.multiple_of(g * gw, gw)
        for c in range(gw // LANES):
            s = score_ref[pl.ds(r0, ROW_CHUNK), pl.ds(base + c * LANES, LANES)]
            hit = (s > cfb) if strict else (s >= cfb)
            acc = acc + jnp.where(hit, 1.0, 0.0)
        return acc

    acc = lax.fori_loop(0, ngroups, body, jnp.zeros((ROW_CHUNK, LANES), F32))
    return jnp.sum(acc, axis=1, keepdims=True)


def _kth_largest(score_ref, r0, ngroups, gw, k):
    ninf = jnp.full((ROW_CHUNK, LANES), -jnp.inf, F32)

    def bounds(g, carry):
        base = pl.multiple_of(g * gw, gw)
        tops = list(carry)
        for c in range(gw // LANES):
            s = score_ref[pl.ds(r0, ROW_CHUNK), pl.ds(base + c * LANES, LANES)]
            tops[c % 2] = jnp.maximum(tops[c % 2], s)
        return tuple(tops)

    top_even, top_odd = lax.fori_loop(0, ngroups, bounds, (ninf, ninf))
    upper = jnp.max(jnp.maximum(top_even, top_odd), axis=1, keepdims=True)
    lower = jnp.min(jnp.minimum(top_even, top_odd), axis=1, keepdims=True)
    lo0 = _float_to_key(lower)
    hi0 = _float_to_key(upper) + 1

    def n_active(lo, hi):
        return jnp.sum(jnp.where(hi > lo + 1, 1.0, 0.0))

    def cond(state):
        return state[2] > 0.0

    def step(state):
        lo, hi, _ = state
        mid = lax.shift_right_arithmetic(lo, 1) + lax.shift_right_arithmetic(hi, 1) + (lo & hi & 1)
        cnt = _count(score_ref, r0, ngroups, gw, _key_to_float(mid), strict=False)
        keep = cnt >= k
        lo = jnp.where(keep, mid, lo)
        hi = jnp.where(cnt == k, mid + 1, jnp.where(keep, hi, mid))
        return lo, hi, n_active(lo, hi)

    lo, _, _ = lax.while_loop(cond, step, (lo0, hi0, n_active(lo0, hi0)))
    return _key_to_float(lo)


def _emit_selection(score_ref, thr_ref, need_ref, ngroups, gw, nblk, bw, k, emit, emit_last=None):
    rows = score_ref.shape[0]
    assert k <= MAX_GROUPS and rows % ROW_CHUNK == 0

    def chunk(n, _):
        r0 = pl.multiple_of(n * ROW_CHUNK, ROW_CHUNK)
        thr = _kth_largest(score_ref, r0, ngroups, gw, k)
        thr_ref[pl.ds(r0, ROW_CHUNK), :] = thr
        need_ref[pl.ds(r0, ROW_CHUNK), :] = k - _count(score_ref, r0, ngroups, gw, thr, strict=True)
        return 0

    lax.fori_loop(0, rows // ROW_CHUNK, chunk, 0)
    thr = thr_ref[...]
    need = need_ref[...]
    finite = jnp.where(thr > -jnp.inf, 1.0, 0.0)
    before = (lax.broadcasted_iota(I32, (bw, bw), 0) < lax.broadcasted_iota(I32, (bw, bw), 1))
    before = jnp.where(before, 1.0, 0.0).astype(BF16)

    def body(j, carry, emit=emit):
        s = score_ref[:, pl.ds(pl.multiple_of(j * bw, bw), bw)]
        tie = jnp.where(s == thr, finite, 0.0)
        rank = _dot(tie.astype(BF16), before) + carry
        sel = jnp.where(s > thr, 1.0, jnp.where(rank < need, tie, 0.0))
        emit(j, sel)
        return carry + jnp.sum(tie, axis=1, keepdims=True)

    n_loop = nblk if emit_last is None else nblk - 1
    carry = lax.fori_loop(0, n_loop, body, jnp.zeros((rows, 1), F32))
    if emit_last is not None:
        body(nblk - 1, carry, emit=emit_last)


def _half_masks(shape):
    lane = lax.broadcasted_iota(I32, shape, 1)
    return lane < HEAD_DIM


def _index_kernel(iq_ref, wit_ref, ik_ref, mask_ref, score_ref, top_ref, *, tq, topk):
    i = pl.program_id(1)
    nblk_total = mask_ref.shape[2] // tq
    assert topk <= tq
    lo_half = _half_masks((tq, LANES))
    q_parts = []
    for h in range(N_IDX_HEADS):
        qp = iq_ref[0, :, (h // 2) * LANES:(h // 2 + 1) * LANES]
        swapped = pltpu.roll(qp, HEAD_DIM, axis=1)
        qh = jnp.where(lo_half, qp, swapped) if h % 2 == 0 else jnp.where(lo_half, swapped, qp)
        q_hi = qh.astype(BF16)
        q_lo = jnp.where(lo_half, qh - q_hi.astype(F32), 0.0).astype(BF16)
        q_parts.append(jnp.concatenate([q_hi, q_lo], axis=1))
    w_rows = [wit_ref[h:h + 1, :] for h in range(N_IDX_HEADS)]
    key = lax.broadcasted_iota(I32, (tq, tq), 0)
    qry = lax.broadcasted_iota(I32, (tq, tq), 1)
    top_ref[...] = jnp.full((tq, tq), -jnp.inf, F32)

    def score_block(j, _):
        base = pl.multiple_of(j * tq, tq)
        kk = ik_ref[0, pl.ds(base, tq), :]
        k_hi = kk.astype(BF16)
        k_hi_f32 = k_hi.astype(F32)
        k_parts = jnp.concatenate([jnp.where(lo_half, k_hi_f32, kk - k_hi_f32).astype(BF16), k_hi], axis=1)
        s = jnp.zeros((tq, tq), F32)
        for h in range(N_IDX_HEADS):
            d = _dot_nt(k_parts, q_parts[h])
            s = s + w_rows[h] * jnp.maximum(d, 0.0)
        s = jnp.where(key + (j - i) * tq <= qry, s, -jnp.inf)
        score_ref[pl.ds(base, tq), :] = s
        top_ref[...] = jnp.maximum(top_ref[...], s)
        return 0

    lax.fori_loop(0, i + 1, score_block, 0)

    def count(cf, strict):
        cfb = jnp.broadcast_to(cf, (SUBLANES, tq))

        def body(j, accs):
            blk = score_ref[pl.ds(pl.multiple_of(j * tq, tq), tq), :]
            accs = list(accs)
            for r in range(tq // SUBLANES):
                s = blk[r * SUBLANES:(r + 1) * SUBLANES]
                hit = (s > cfb) if strict else (s >= cfb)
                accs[r % len(accs)] = accs[r % len(accs)] + jnp.where(hit, 1.0, 0.0)
            return tuple(accs)

        zero = jnp.zeros((SUBLANES, tq), F32)
        accs = lax.fori_loop(0, i + 1, body, (zero, zero, zero, zero))
        return jnp.sum((accs[0] + accs[1]) + (accs[2] + accs[3]), axis=0, keepdims=True)

    tops = top_ref[...]
    lo0 = _float_to_key(jnp.min(tops, axis=0, keepdims=True))
    hi0 = _float_to_key(jnp.max(tops, axis=0, keepdims=True)) + 1

    def n_active(lo, hi):
        return jnp.sum(jnp.where(hi > lo + 1, 1.0, 0.0))

    def step(state):
        lo, hi, _ = state
        mid = lax.shift_right_arithmetic(lo, 1) + lax.shift_right_arithmetic(hi, 1) + (lo & hi & 1)
        cnt = count(_key_to_float(mid), strict=False)
        keep = cnt >= topk
        lo = jnp.where(keep, mid, lo)
        hi = jnp.where(cnt == topk, mid + 1, jnp.where(keep, hi, mid))
        return lo, hi, n_active(lo, hi)

    at_least_zero = count(jnp.zeros((1, tq), F32), strict=False)
    above_zero = count(jnp.zeros((1, tq), F32), strict=True)
    zero_kth = jnp.logical_and(above_zero < topk, at_least_zero >= topk)
    lo0 = jnp.where(zero_kth, 0, jnp.where(above_zero >= topk, jnp.maximum(lo0, 1), lo0))
    hi0 = jnp.where(zero_kth, 1, jnp.where(at_least_zero < topk, jnp.minimum(hi0, 0), hi0))

    lo, _, _ = lax.while_loop(lambda state: state[2] > 0.0, step, (lo0, hi0, n_active(lo0, hi0)))
    thr = _key_to_float(lo)
    need = topk - count(thr, strict=True)
    finite = jnp.where(thr > -jnp.inf, 1.0, 0.0)
    before = jnp.where(qry < key, 1.0, 0.0).astype(BF16)
    eye = jnp.where(qry == key, 1.0, 0.0).astype(BF16)

    def emit(j, carry):
        base = pl.multiple_of(j * tq, tq)
        s = score_ref[pl.ds(base, tq), :]
        tie = jnp.where(s == thr, finite, 0.0)
        rank = _dot(before, tie.astype(BF16)) + carry
        sel = jnp.where(s > thr, 1.0, jnp.where(rank < need, tie, 0.0))
        mask_ref[0, :, pl.ds(base, tq)] = _dot_nt(eye, sel.astype(BF16)).astype(jnp.int8)
        return carry + jnp.sum(tie, axis=0, keepdims=True)

    carry = lax.fori_loop(0, (i + 1) // 2, lambda jj, c: emit(2 * jj + 1, emit(2 * jj, c)), jnp.zeros((1, tq), F32))

    @pl.when((i + 1) % 2 == 1)
    def _():
        emit(i, carry)

    def clear(j, _):
        mask_ref[0, :, pl.ds(pl.multiple_of(j * tq, tq), tq)] = jnp.zeros((tq, tq), jnp.int8)
        return 0

    lax.fori_loop(i + 1, nblk_total, clear, 0)


def _prompt_index(iq, wit, ik, tq, topk):
    b, t, _ = iq.shape
    nq = t // tq
    return pl.pallas_call(
        functools.partial(_index_kernel, tq=tq, topk=float(topk)),
        grid=(b, nq),
        in_specs=[pl.BlockSpec((1, tq, iq.shape[2]), lambda bi, i: (bi, i, 0)),
                  pl.BlockSpec((SUBLANES, tq), lambda bi, i: (0, bi * nq + i)),
                  pl.BlockSpec((1, t, LANES), lambda bi, i: (bi, 0, 0))],
        out_specs=pl.BlockSpec((1, tq, t), lambda bi, i: (bi, i, 0)),
        out_shape=SDS((b, t, t), jnp.int8),
        scratch_shapes=[pltpu.VMEM((t, tq), F32), pltpu.VMEM((tq, tq), F32)],
        compiler_params=_params("arbitrary", "arbitrary"),
        name="prompt_index",
    )(iq, wit, ik)


def _t5_bucket(rel):
    rel = jnp.maximum(rel, 0)
    lf = (jnp.log(jnp.maximum(rel, 1).astype(F32) / MAX_EXACT)
          / math.log(MAX_DISTANCE / MAX_EXACT) * (N_BUCKETS - MAX_EXACT))
    large = jnp.minimum(MAX_EXACT + lf.astype(I32), N_BUCKETS - 1)
    return jnp.where(rel < MAX_EXACT, rel, large)


def _bias_lookup(rb_ref, bucket, h):
    val = jnp.zeros(bucket.shape, F32)
    for b in range(N_BUCKETS):
        val = jnp.where(bucket == b, rb_ref[b, h], val)
    return val


def _bias_tiles_kernel(rb_ref, o_ref, *, tq):
    h = pl.program_id(0)
    row = lax.broadcasted_iota(I32, (tq, tq), 0)
    col = lax.broadcasted_iota(I32, (tq, tq), 1)
    far = rb_ref[N_BUCKETS - 1, h]
    for tile in range(2):
        o_ref[tile, 0] = (_bias_lookup(rb_ref, _t5_bucket(row - col + tile * tq), h) - far) * LOG2E
    o_ref[2, 0] = jnp.zeros((tq, tq), F32)


def _bias_tiles(rel_bias, tq):
    assert tq >= MAX_DISTANCE
    return pl.pallas_call(
        functools.partial(_bias_tiles_kernel, tq=tq),
        grid=(N_HEADS,),
        in_specs=[pl.BlockSpec(memory_space=pltpu.SMEM)],
        out_specs=pl.BlockSpec((3, 1, tq, tq), lambda h: (0, h, 0, 0)),
        out_shape=SDS((3, N_HEADS, tq, tq), F32),
        compiler_params=_params("arbitrary"),
        name="bias_tiles",
    )(rel_bias)


def _pair_queries(q_ref, tq):
    lo_half = _half_masks((tq, LANES))
    out = []
    for p in range(N_PAIRS):
        qp = q_ref[0, :, p * LANES:(p + 1) * LANES].astype(F32)
        out.append((jnp.where(lo_half, qp, 0.0).astype(BF16), jnp.where(lo_half, 0.0, qp).astype(BF16)))
    return out, lo_half


def _dsa_kernel(q_ref, k_ref, v_ref, mask_ref, bias_ref, o_ref, logit_ref, madd_ref, m_ref, acc_ref, *, tq):
    i = pl.program_id(1)
    tk = 2 * tq
    qs, lo_half = _pair_queries(q_ref, tq)
    m_ref[...] = jnp.full(m_ref.shape, NEG, F32)
    acc_ref[...] = jnp.zeros(acc_ref.shape, F32)
    keep_lo = jnp.where(_half_masks((tk, LANES)), 1.0, 0.0).astype(BF16)
    keep_hi = 1.0 - keep_lo

    def pair_step(jj, near):
        base = pl.multiple_of(jj * tk, tk)
        madd_ref[...] = jnp.where(mask_ref[0, :, pl.ds(base, tk)].astype(I32) != 0, 0.0, NEG)
        for p in range(N_PAIRS):
            for u in range(2):
                kp = k_ref[0, pl.ds(base + u * tq, tq), p * LANES:(p + 1) * LANES]
                for hh in range(2):
                    h = 2 * p + hh
                    lm = _dot_nt(qs[p][hh], kp) + madd_ref[:, u * tq:(u + 1) * tq]
                    if near:
                        lm = lm + bias_ref[jnp.clip(i - (2 * jj + u), 0, 2), h]
                    logit_ref[h, :, u * tq:(u + 1) * tq] = lm
        for p in range(N_PAIRS):
            vp = v_ref[0, pl.ds(base, tk), p * LANES:(p + 1) * LANES]
            for hh in range(2):
                h = 2 * p + hh
                m_old = m_ref[h]
                m_new = jnp.maximum(m_old, jnp.max(logit_ref[h], axis=1, keepdims=True))
                m_ref[h] = m_new
                pexp = jnp.concatenate(
                    [jnp.exp2(logit_ref[h, :, c * LANES:(c + 1) * LANES] - m_new) for c in range(tk // LANES)],
                    axis=1)
                vaug = vp * keep_lo + keep_hi if hh == 0 else vp * keep_hi + keep_lo
                acc_ref[h] = jnp.exp2(m_old - m_new) * acc_ref[h] + _dot(pexp.astype(BF16), vaug)

    n_pairs = (i + 2) // 2
    n_far = jnp.maximum(n_pairs - 2, 0)

    def far_body(jj, _):
        pair_step(jj, False)
        return 0

    def near_body(jj, _):
        pair_step(jj, True)
        return 0

    lax.fori_loop(0, n_far, far_body, 0)
    lax.fori_loop(n_far, n_pairs, near_body, 0)
    for p in range(N_PAIRS):
        outs = []
        for hh in range(2):
            acc = acc_ref[2 * p + hh]
            outs.append(acc / pltpu.roll(acc, HEAD_DIM, axis=1))
        o_ref[0, :, p * LANES:(p + 1) * LANES] = jnp.where(lo_half, outs[0], outs[1])


def _prompt_dsa(q, k, v, mask, bias, tq):
    b, t, dh = q.shape
    assert t % (2 * tq) == 0
    once = pl.Buffered(1)
    seq = pl.BlockSpec((1, t, dh), lambda bi, i: (bi, 0, 0), pipeline_mode=once)
    return pl.pallas_call(
        functools.partial(_dsa_kernel, tq=tq),
        grid=(b, t // tq),
        in_specs=[pl.BlockSpec((1, tq, dh), lambda bi, i: (bi, i, 0)), seq, seq,
                  pl.BlockSpec((1, tq, t), lambda bi, i: (bi, i, 0)),
                  pl.BlockSpec(bias.shape, lambda bi, i: (0, 0, 0, 0), pipeline_mode=once)],
        out_specs=pl.BlockSpec((1, tq, dh), lambda bi, i: (bi, i, 0)),
        out_shape=SDS((b, t, dh), F32),
        scratch_shapes=[pltpu.VMEM((N_HEADS, tq, 2 * tq), F32), pltpu.VMEM((tq, 2 * tq), F32),
                        pltpu.VMEM((N_HEADS, tq, LANES), F32), pltpu.VMEM((N_HEADS, tq, LANES), F32)],
        compiler_params=_params("arbitrary", "arbitrary"),
        name="prompt_dsa",
    )(q, k, v, mask, bias)


def _sb_kernel(q_ref, k_ref, v_ref, o_ref, z_ref, tail_ref, c_ref, acc_ref, *, tq):
    i = pl.program_id(1)
    qs, lo_half = _pair_queries(q_ref, tq)
    c_ref[...] = jnp.zeros(c_ref.shape, F32)
    acc_ref[...] = jnp.zeros(acc_ref.shape, F32)
    row = lax.broadcasted_iota(I32, (tq, tq), 0)
    col = lax.broadcasted_iota(I32, (tq, tq), 1)
    not_before = jnp.where(row >= col, 1.0, 0.0).astype(BF16)
    earlier = col < row

    def block(j, diag):
        base = pl.multiple_of(j * tq, tq)
        for p in range(N_PAIRS):
            kp = k_ref[0, pl.ds(base, tq), p * LANES:(p + 1) * LANES]
            for hh in range(2):
                z_ref[2 * p + hh] = _dot_nt(qs[p][hh], kp)
        for h in range(N_HEADS):
            z = z_ref[h]
            sp = jnp.maximum(z, jnp.log2(1.0 + jnp.exp2(jnp.minimum(z, EXP2_CLAMP))))
            if diag:
                sp = jnp.where(earlier, sp, 0.0)
            tail_ref[h] = _dot(sp.astype(BF16), not_before)
        for p in range(N_PAIRS):
            vp = v_ref[0, pl.ds(base, tq), p * LANES:(p + 1) * LANES]
            pvs = []
            for hh in range(2):
                h = 2 * p + hh
                c_old = c_ref[h]
                a = jnp.concatenate(
                    [jnp.exp2(z_ref[h, :, n * LANES:(n + 1) * LANES] - tail_ref[h, :, n * LANES:(n + 1) * LANES]
                              - c_old) for n in range(tq // LANES)], axis=1)
                if diag:
                    a = jnp.where(earlier, a, 0.0)
                c_ref[h] = c_old + tail_ref[h, :, :1]
                pvs.append(_dot(a.astype(BF16), vp))
            sl = slice(p * LANES, (p + 1) * LANES)
            acc_ref[:, sl] = acc_ref[:, sl] + jnp.where(lo_half, pvs[0], pvs[1])

    block(i, True)

    def body(step, _):
        block(i - 1 - step, False)
        return 0

    lax.fori_loop(0, i, body, 0)
    o_ref[0] = acc_ref[...]


def _prompt_sb(q, k, v, tq):
    b, t, dh = q.shape
    seq = pl.BlockSpec((1, t, dh), lambda bi, i: (bi, 0, 0), pipeline_mode=pl.Buffered(1))
    return pl.pallas_call(
        functools.partial(_sb_kernel, tq=tq),
        grid=(b, t // tq),
        in_specs=[pl.BlockSpec((1, tq, dh), lambda bi, i: (bi, i, 0)), seq, seq],
        out_specs=pl.BlockSpec((1, tq, dh), lambda bi, i: (bi, i, 0)),
        out_shape=SDS((b, t, dh), F32),
        scratch_shapes=[pltpu.VMEM((N_HEADS, tq, tq), F32), pltpu.VMEM((N_HEADS, tq, tq), F32),
                        pltpu.VMEM((N_HEADS, tq, LANES), F32),
                        pltpu.VMEM((tq, dh), F32)],
        compiler_params=_params("arbitrary", "arbitrary"),
        name="prompt_sb",
    )(q, k, v)


def _merge_kernel(ya_ref, yb_ref, gate_ref, x_ref, g1_ref, n_ref, wa_ref, wb_ref, wo_ref, o_ref):
    d = x_ref.shape[1]
    ma = _dot(ya_ref[...].astype(BF16), wa_ref[...])
    mb = _dot(yb_ref[...].astype(BF16), wb_ref[...])
    merged = gate_ref[:, :d].astype(F32) * ma + gate_ref[:, d:].astype(F32) * mb
    mix = _dot(merged.astype(BF16), wo_ref[...])
    o_ref[...] = x_ref[...] + g1_ref[0] * _rms(mix, n_ref[...])


def _mod_spec(mod, tm, rows_per_mod):
    d = mod.shape[2]
    if mod.shape[1] == 1:
        return pl.BlockSpec((1, 1, d), lambda i: ((i * tm) // rows_per_mod, 0, 0))
    return pl.BlockSpec((1, tm, d), lambda i: (0, i, 0))


def _merge(ya, yb, gate, x, g1, norm, wa, wb, wo, tm, rows_per_mod):
    m, d = x.shape
    row = lambda a: pl.BlockSpec((tm, a.shape[1]), lambda i: (i, 0))
    full = lambda a: pl.BlockSpec(a.shape, lambda i: (0, 0), pipeline_mode=pl.Buffered(1))
    return pl.pallas_call(
        _merge_kernel,
        grid=(m // tm,),
        in_specs=[row(ya), row(yb), row(gate), row(x), _mod_spec(g1, tm, rows_per_mod),
                  full(norm), full(wa), full(wb), full(wo)],
        out_specs=pl.BlockSpec((tm, d), lambda i: (i, 0)),
        out_shape=SDS((m, d), F32),
        compiler_params=_params("arbitrary"),
        name="merge",
    )(ya, yb, gate, x, g1, norm, wa, wb, wo)


def _ffn_kernel(x_ref, sc_ref, sh_ref, g2_ref, npre_ref, npost_ref, wg_ref, wu_ref, wd_ref, o_ref, *, chunk):
    x = x_ref[...]
    h = (_rms(x, npre_ref[...]) * (1.0 + sc_ref[0]) + sh_ref[0]).astype(BF16)
    f = jnp.zeros(x.shape, F32)
    for c0 in range(0, wg_ref.shape[1], chunk):
        gate = _dot(h, wg_ref[:, c0:c0 + chunk])
        up = _dot(h, wu_ref[:, c0:c0 + chunk])
        act = gate / (1.0 + jnp.exp(-gate)) * up
        f = f + _dot(act.astype(BF16), wd_ref[c0:c0 + chunk, :])
    o_ref[...] = x + g2_ref[0] * _rms(f, npost_ref[...])


def _ffn(x, sc, sh, g2, npre, npost, wg, wu, wd, tm, rows_per_mod):
    m, d = x.shape
    full = lambda a: pl.BlockSpec(a.shape, lambda i: (0, 0), pipeline_mode=pl.Buffered(1))
    mod = lambda a: _mod_spec(a, tm, rows_per_mod)
    return pl.pallas_call(
        functools.partial(_ffn_kernel, chunk=256),
        grid=(m // tm,),
        in_specs=[pl.BlockSpec((tm, d), lambda i: (i, 0)), mod(sc), mod(sh), mod(g2),
                  full(npre), full(npost), full(wg), full(wu), full(wd)],
        out_specs=pl.BlockSpec((tm, d), lambda i: (i, 0)),
        out_shape=SDS((m, d), F32),
        compiler_params=_params("arbitrary"),
        name="ffn",
    )(x, sc, sh, g2, npre, npost, wg, wu, wd)


def _sample_score_kernel(pt_ref, q_ref, w_ref, knew_ref, *rest, pages):
    page_refs = rest[:pages]
    past_ref, new_ref = rest[pages:]
    q_hi, q_lo = _split2(q_ref[0])
    w = w_ref[0]
    t = q_ref.shape[1] // N_IDX_HEADS

    def score(keys):
        k_hi, k_lo = _split2(keys)
        r = jnp.maximum(_dot3(q_hi, q_lo, k_hi, k_lo), 0.0) * w
        s = r[0:t]
        for h in range(1, N_IDX_HEADS):
            s = s + r[h * t:(h + 1) * t]
        return s

    for n in range(pages):
        past_ref[0, :, n * LANES:(n + 1) * LANES] = score(page_refs[n][...])
    new_ref[0] = score(knew_ref[0])


def _sample_scores(page_table, q_rows, w_rows, k_new, cache_k_idx, pages):
    nb, n_pages = page_table.shape
    page = cache_k_idx.shape[2]
    assert page == LANES
    rows = q_rows.shape[1]
    t = rows // N_IDX_HEADS

    def page_spec(n):
        return pl.BlockSpec((None, None, IDX_DIM, page), lambda b, g, pt: (0, pt[b, g * pages + n], 0, 0))

    grid_spec = pltpu.PrefetchScalarGridSpec(
        num_scalar_prefetch=1,
        grid=(nb, n_pages // pages),
        in_specs=[pl.BlockSpec((1, rows, IDX_DIM), lambda b, g, pt: (b, 0, 0)),
                  pl.BlockSpec((1, rows, LANES), lambda b, g, pt: (b, 0, 0)),
                  pl.BlockSpec((1, IDX_DIM, page), lambda b, g, pt: (b, 0, 0))]
                 + [page_spec(n) for n in range(pages)],
        out_specs=[pl.BlockSpec((1, t, pages * page), lambda b, g, pt: (b, 0, g)),
                   pl.BlockSpec((1, t, page), lambda b, g, pt: (b, 0, 0))],
    )
    return pl.pallas_call(
        functools.partial(_sample_score_kernel, pages=pages),
        grid_spec=grid_spec,
        out_shape=[SDS((nb, t, n_pages * page), F32), SDS((nb, t, page), F32)],
        compiler_params=_params("arbitrary", "arbitrary"),
        name="sample_scores",
    )(page_table, q_rows, w_rows, k_new, *([cache_k_idx.transpose(0, 1, 3, 2)] * pages))


def _sample_select_kernel(past_ref, new_ref, mpast_ref, mnew_ref, score_ref, thr_ref, need_ref, *, t, topk):
    rows, past = past_ref.shape
    score_ref[:, :past] = past_ref[...]
    q_t = lax.broadcasted_iota(I32, (rows, LANES), 0) % t
    col = lax.broadcasted_iota(I32, (rows, LANES), 1)
    score_ref[:, past:] = jnp.where(col <= q_t, new_ref[...], -jnp.inf)

    def emit(j, sel):
        mpast_ref[:, pl.ds(pl.multiple_of(j * LANES, LANES), LANES)] = sel

    def emit_new(j, sel):
        mnew_ref[...] = sel

    width = score_ref.shape[1]
    gw = _group_width(width)
    _emit_selection(score_ref, thr_ref, need_ref, width // gw, gw, width // LANES, LANES, topk, emit, emit_new)


def _sample_select(s_past, s_new, t, topk, tr):
    rows, past = s_past.shape
    width = past + LANES
    return pl.pallas_call(
        functools.partial(_sample_select_kernel, t=t, topk=float(topk)),
        grid=(rows // tr,),
        in_specs=[pl.BlockSpec((tr, past), lambda i: (i, 0)), pl.BlockSpec((tr, LANES), lambda i: (i, 0))],
        out_specs=[pl.BlockSpec((tr, past), lambda i: (i, 0)), pl.BlockSpec((tr, LANES), lambda i: (i, 0))],
        out_shape=[SDS((rows, past), F32), SDS((rows, LANES), F32)],
        scratch_shapes=[pltpu.VMEM((tr, width), F32), pltpu.VMEM((tr, 1), F32), pltpu.VMEM((tr, 1), F32)],
        compiler_params=_params("arbitrary"),
        name="sample_select",
    )(s_past, s_new)


def _sample_bias_kernel(rb_ref, o_ref, *, t, page):
    rows = N_HEADS * t
    row = lax.broadcasted_iota(I32, (rows, page), 0)
    key = lax.broadcasted_iota(I32, (rows, page), 1)
    q_t, head = row % t, row // t
    for tile in range(3):
        bucket = _t5_bucket(q_t - key + tile * page)
        val = jnp.zeros((rows, page), F32)
        for h in range(N_HEADS):
            val = jnp.where(head == h, _bias_lookup(rb_ref, bucket, h), val)
        o_ref[tile] = val


def _sample_bias(rel_bias, t, page):
    assert page >= MAX_DISTANCE
    return pl.pallas_call(
        functools.partial(_sample_bias_kernel, t=t, page=page),
        in_specs=[pl.BlockSpec(memory_space=pltpu.SMEM)],
        out_shape=SDS((3, N_HEADS * t, page), F32),
        name="sample_bias",
    )(rel_bias)


def _sample_attn_kernel(pt_ref, qa_ref, qb_ref, knew_a_ref, vnew_a_ref, knew_b_ref, vnew_b_ref,
                        fnew_ref, flags_ref, bias_ref, *rest, pages, t):
    ka_refs = rest[0 * pages:1 * pages]
    va_refs = rest[1 * pages:2 * pages]
    kb_refs = rest[2 * pages:3 * pages]
    vb_refs = rest[3 * pages:4 * pages]
    ya_ref, yb_ref, m_ref, l_ref, acca_ref, c_ref, accb_ref = rest[4 * pages:]
    g = pl.program_id(1)
    n_groups = pl.num_programs(1)
    rows = N_HEADS * t
    qa = qa_ref[0]
    qb = qb_ref[0]
    srow = lax.broadcasted_iota(I32, (LANES, LANES), 0)
    scol = lax.broadcasted_iota(I32, (LANES, LANES), 1)
    not_before = jnp.where(srow >= scol, 1.0, 0.0).astype(BF16)
    not_before = jnp.concatenate([not_before, not_before], axis=0)

    def scores(q, k):
        return _dot(q, k.reshape(N_HEADS * HEAD_DIM, LANES).astype(BF16))

    def weighted(p, v):
        return _dot_nt(p.astype(BF16), v.reshape(N_HEADS * HEAD_DIM, LANES).astype(BF16))

    def dsa_update(blocks):
        logits = [scores(qa, k) + bias + (1.0 - jnp.concatenate([flags] * N_HEADS, axis=0)) * NEG
                  for k, _, bias, flags in blocks]
        m_old = m_ref[...]
        m_new = m_old
        for lg in logits:
            m_new = jnp.maximum(m_new, jnp.max(lg, axis=1, keepdims=True))
        alpha = jnp.exp(m_old - m_new)
        l_new = alpha * l_ref[...]
        contrib = jnp.zeros(acca_ref.shape, F32)
        for lg, (_, v, _, _) in zip(logits, blocks):
            pexp = jnp.exp(lg - m_new)
            l_new = l_new + jnp.sum(pexp, axis=1, keepdims=True)
            contrib = contrib + weighted(pexp, v)
        m_ref[...] = m_new
        l_ref[...] = l_new
        acca_ref[...] = alpha * acca_ref[...] + contrib

    def sb_update(blocks):
        zs = [scores(qb, k) for k, _, _ in blocks]
        tails = []
        for z, (_, _, earlier) in zip(zs, blocks):
            sp = jnp.maximum(z, 0.0) + jnp.log(1.0 + jnp.exp(-jnp.abs(z)))
            if earlier is not None:
                sp = jnp.where(earlier, sp, 0.0)
            hi, lo = _split2(sp)
            tails.append(_dot(jnp.concatenate([hi, lo], axis=1), not_before))
        carry = c_ref[...]
        contrib = jnp.zeros(accb_ref.shape, F32)
        for z, tail, (_, v, earlier) in zip(zs, tails, blocks):
            a = jnp.exp(z - tail - carry)
            if earlier is not None:
                a = jnp.where(earlier, a, 0.0)
            carry = carry + tail[:, :1]
            contrib = contrib + weighted(a, v)
        c_ref[...] = carry
        accb_ref[...] = accb_ref[...] + contrib

    @pl.when(g == 0)
    def _():
        m_ref[...] = jnp.full(m_ref.shape, NEG, F32)
        l_ref[...] = jnp.zeros(l_ref.shape, F32)
        acca_ref[...] = jnp.zeros(acca_ref.shape, F32)
        c_ref[...] = jnp.zeros(c_ref.shape, F32)
        accb_ref[...] = jnp.zeros(accb_ref.shape, F32)
        key = lax.broadcasted_iota(I32, (rows, LANES), 1)
        q_t = lax.broadcasted_iota(I32, (rows, LANES), 0) % t
        dsa_update([(knew_a_ref[0], vnew_a_ref[0], bias_ref[0], fnew_ref[0])])
        sb_update([(knew_b_ref[0], vnew_b_ref[0], key < q_t)])

    blocks = []
    for n in range(pages):
        flags = flags_ref[0, :, (pages - 1 - n) * LANES:(pages - n) * LANES]
        bias = bias_ref[2] if n else jnp.where(g == 0, bias_ref[1], bias_ref[2])
        blocks.append((ka_refs[n][...], va_refs[n][...], bias, flags))
    dsa_update(blocks)
    sb_update([(kb_refs[n][...], vb_refs[n][...], None) for n in range(pages)])

    @pl.when(g == n_groups - 1)
    def _():
        ya_ref[0] = acca_ref[...] / l_ref[...]
        yb_ref[0] = accb_ref[...]


def _sample_attn(page_table, qa_rows, qb_rows, knew_a, vnew_a, knew_b, vnew_b, flags_new, flags_past, bias,
                 cache_k_a, cache_v_a, cache_k_b, cache_v_b, pages, t):
    nb, n_pages = page_table.shape
    rows = N_HEADS * t
    n_groups = n_pages // pages
    dh = N_HEADS * HEAD_DIM
    page_shape = (N_HEADS, HEAD_DIM, LANES)

    def page_spec(n):
        return pl.BlockSpec((None, None) + page_shape,
                            lambda b, g, pt: (0, pt[b, n_pages - 1 - (g * pages + n)], 0, 0, 0))

    per_seq = lambda shape: pl.BlockSpec((1,) + shape, lambda b, g, pt: (b,) + (0,) * len(shape))
    grid_spec = pltpu.PrefetchScalarGridSpec(
        num_scalar_prefetch=1,
        grid=(nb, n_groups),
        in_specs=[per_seq((rows, dh)), per_seq((rows, dh)),
                  per_seq(page_shape), per_seq(page_shape), per_seq(page_shape), per_seq(page_shape),
                  per_seq((t, LANES)),
                  pl.BlockSpec((1, t, pages * LANES), lambda b, g, pt: (b, 0, n_groups - 1 - g)),
                  pl.BlockSpec(bias.shape, lambda b, g, pt: (0, 0, 0))]
                 + [page_spec(n) for n in range(pages)] * 4,
        out_specs=[per_seq((rows, dh)), per_seq((rows, dh))],
        scratch_shapes=[pltpu.VMEM((rows, 1), F32), pltpu.VMEM((rows, 1), F32), pltpu.VMEM((rows, dh), F32),
                        pltpu.VMEM((rows, 1), F32), pltpu.VMEM((rows, dh), F32)],
    )
    key_minor = lambda c: c.transpose(0, 1, 3, 4, 2)
    return pl.pallas_call(
        functools.partial(_sample_attn_kernel, pages=pages, t=t),
        grid_spec=grid_spec,
        out_shape=[SDS((nb, rows, dh), F32), SDS((nb, rows, dh), F32)],
        compiler_params=_params("arbitrary", "arbitrary"),
        name="sample_attn",
    )(page_table, qa_rows, qb_rows, knew_a, vnew_a, knew_b, vnew_b, flags_new, flags_past, bias,
      *([key_minor(cache_k_a)] * pages), *([key_minor(cache_v_a)] * pages),
      *([key_minor(cache_k_b)] * pages), *([key_minor(cache_v_b)] * pages))


def _largest_divisor(n, cap):
    d = min(n, cap)
    while n % d:
        d -= 1
    return d


def kernel(x_prompt, x_sample, c_prompt, c_sample, cache_k_a, cache_v_a, cache_k_idx, cache_k_b, cache_v_b,
           page_table, rel_bias, w_ada, b_ada, norm_pre_mix, norm_post_mix, norm_pre_ffn, norm_post_ffn,
           w_in, w_br_a, w_br_b, w_out, w_ffn_gate, w_ffn_up, w_ffn_down):
    depth = w_in.shape[0]
    assert depth == 1
    nb_p, t_p, d = x_prompt.shape
    nb_s, t_s, _ = x_sample.shape
    n_pages = page_table.shape[1]
    page = cache_k_a.shape[2]
    past = n_pages * page
    dh = N_HEADS * HEAD_DIM
    nq = N_IDX_HEADS * IDX_DIM
    topk_p = min(INDEX_TOPK, t_p // 4)
    topk_s = min(INDEX_TOPK, (past + t_s) // 4)
    tq = 256
    assert t_p % tq == 0 and topk_p <= tq and t_s * N_HEADS <= LANES and page == LANES

    w = w_in[0]
    o = np.cumsum([0, dh, dh, dh, nq, IDX_DIM, N_IDX_HEADS, dh, dh, dh, d, d])
    col = lambda n: w[:, o[n]:o[n + 1]]
    w_main = jnp.concatenate([col(0), col(1), col(2), col(6), col(7), col(8)], axis=1).astype(BF16)
    w_gate = jnp.concatenate([col(9), col(10)], axis=1).astype(BF16)
    w_idx = jnp.concatenate([col(3), col(4), col(4), col(5),
                             jnp.zeros((d, LANES - N_IDX_HEADS), F32)], axis=1)
    w_idx_hi = w_idx.astype(BF16)
    w_idx_lo = (w_idx - w_idx_hi.astype(F32)).astype(BF16)
    wa, wb, wo = w_br_a[0].astype(BF16), w_br_b[0].astype(BF16), w_out[0].astype(BF16)
    wg, wu, wd = w_ffn_gate[0].astype(BF16), w_ffn_up[0].astype(BF16), w_ffn_down[0].astype(BF16)
    n_pre_mix, n_post_mix = norm_pre_mix[0][None], norm_post_mix[0][None]
    n_pre_ffn, n_post_ffn = norm_pre_ffn[0][None], norm_post_ffn[0][None]

    n_c = nb_p + nb_s
    c_all = jnp.concatenate([c_prompt, c_sample, jnp.zeros((-n_c % 8, d), F32)], axis=0)
    ada = _ada(c_all, w_ada[0], b_ada[0][None])
    mods_p = [m[:, None, :] for m in jnp.split(ada[:nb_p], 6, axis=-1)]
    mods_s = [jnp.repeat(m, t_s, axis=0)[None] for m in jnp.split(ada[nb_p:n_c], 6, axis=-1)]

    def dense_front(x2, mods, tm, rows_per_mod, seq_len, q_scale):
        return _inproj(x2, mods[1], mods[0], n_pre_mix, w_main, w_gate, w_idx_hi, w_idx_lo, tm, rows_per_mod,
                       seq_len, q_scale)

    def dense_back(ya, yb, gate, x2, mods, tm, rows_per_mod):
        x1 = _merge(ya, yb, gate, x2, mods[2], n_post_mix, wa, wb, wo, tm, rows_per_mod)
        return _ffn(x1, mods[4], mods[3], mods[5], n_pre_ffn, n_post_ffn, wg, wu, wd, tm, rows_per_mod)

    m_p = nb_p * t_p
    xp2 = x_prompt.reshape(m_p, d)
    tm_p = 512
    (qa, ka, va, qb, kb, vb, kab, vab, kbb, vbb, gate, iq, ik, wi, kidx, wit) = dense_front(
        xp2, mods_p, tm_p, t_p, t_p, LOG2E * HEAD_DIM ** -0.5)
    seq3 = lambda a: a.reshape(nb_p, t_p, a.shape[-1])
    mask = _prompt_index(seq3(iq), wit, seq3(ik), tq, topk_p)
    ya = _prompt_dsa(seq3(qa), seq3(kab), seq3(vab), mask, _bias_tiles(rel_bias, tq), tq)
    yb = _prompt_sb(seq3(qb), seq3(kbb), seq3(vbb), tq)
    y_p = dense_back(ya.reshape(m_p, dh), yb.reshape(m_p, dh), gate, xp2, mods_p, tm_p, t_p)
    heads_p = lambda a: a.reshape(nb_p, N_HEADS, HEAD_DIM, t_p).transpose(0, 3, 1, 2)[None]
    outs_p = (heads_p(ka), heads_p(va), kidx.transpose(0, 2, 1)[None], heads_p(kb), heads_p(vb))

    m_s = nb_s * t_s
    xs2 = x_sample.reshape(m_s, d)
    tm_s = _largest_divisor(m_s, 256)
    (qa, ka, va, qb, kb, vb, _, _, _, _, gate, iq, ik, wi, kidx, _) = dense_front(
        xs2, mods_s, tm_s, t_s, m_s, HEAD_DIM ** -0.5)
    q_rows = iq.reshape(nb_s, t_s, N_IDX_HEADS, IDX_DIM).transpose(0, 2, 1, 3).reshape(nb_s, N_IDX_HEADS * t_s, IDX_DIM)
    w_rows = wi.reshape(nb_s, t_s, LANES)[:, :, :N_IDX_HEADS].transpose(0, 2, 1).reshape(nb_s, N_IDX_HEADS * t_s, 1)
    w_rows = jnp.broadcast_to(w_rows, (nb_s, N_IDX_HEADS * t_s, LANES))
    pages_idx = _largest_divisor(n_pages, 16)
    k_new = jnp.pad(kidx[0].reshape(IDX_DIM, nb_s, t_s).transpose(1, 0, 2), ((0, 0), (0, 0), (0, page - t_s)))
    s_past, s_new = _sample_scores(page_table, q_rows, w_rows, k_new, cache_k_idx, pages_idx)
    sel_past, sel_new = _sample_select(s_past.reshape(m_s, past), s_new.reshape(m_s, page), t_s, topk_s,
                                       _largest_divisor(m_s, 128))
    head_rows = lambda a: (jnp.einsum('bthd,hg->bhtgd', a.reshape(nb_s, t_s, N_HEADS, HEAD_DIM).astype(F32),
                                      jnp.eye(N_HEADS, dtype=F32))
                           .reshape(nb_s, N_HEADS * t_s, dh).astype(BF16))
    as_page = lambda a: jnp.pad(a[0].reshape(N_HEADS, HEAD_DIM, nb_s, t_s).transpose(2, 0, 1, 3),
                                ((0, 0), (0, 0), (0, 0), (0, page - t_s)))
    pages_attn = _largest_divisor(n_pages, 8)
    ya, yb = _sample_attn(page_table, head_rows(qa), head_rows(qb),
                          as_page(ka), as_page(va), as_page(kb), as_page(vb),
                          sel_new.reshape(nb_s, t_s, page), sel_past.reshape(nb_s, t_s, past),
                          _sample_bias(rel_bias, t_s, page),
                          cache_k_a, cache_v_a, cache_k_b, cache_v_b, pages_attn, t_s)
    token_rows = lambda y: (jnp.diagonal(y.reshape(nb_s, N_HEADS, t_s, N_HEADS, HEAD_DIM), axis1=1, axis2=3)
                            .transpose(0, 1, 3, 2).reshape(m_s, dh))
    y_s = dense_back(token_rows(ya), token_rows(yb), gate, xs2, mods_s, tm_s, t_s)
    heads_s = lambda a: a[0].T.reshape(depth, nb_s, t_s, N_HEADS, HEAD_DIM)
    outs_s = (heads_s(ka), heads_s(va), kidx[0].T.reshape(depth, nb_s, t_s, IDX_DIM), heads_s(kb), heads_s(vb))

    return (y_p.reshape(nb_p, t_p, d), y_s.reshape(nb_s, t_s, d)) + outs_p + outs_s
```

```python
import functools
import math

import numpy as np
import jax
import jax.numpy as jnp
from jax import lax
from jax.experimental import pallas as pl
from jax.experimental.pallas import tpu as pltpu

F32 = jnp.float32
BF16 = jnp.bfloat16
I32 = jnp.int32
SDS = jax.ShapeDtypeStruct

HEAD_DIM = 64
N_HEADS = 8
N_PAIRS = N_HEADS // 2
N_IDX_HEADS = 4
IDX_DIM = 64
INDEX_TOPK = 256
N_BUCKETS = 32
MAX_EXACT = N_BUCKETS // 2
MAX_DISTANCE = 128
EPS = 1e-6
LANES = 128
SUBLANES = 8
NEG = -1e30
LOG2E = 1.4426950408889634
EXP2_CLAMP = 126.0
VMEM_LIMIT = 56 * 1024 * 1024

_KEY_NEG_INF = np.int32(0x807FFFFF - 2 ** 32)


def _params(*sem):
    return pltpu.CompilerParams(dimension_semantics=sem, vmem_limit_bytes=VMEM_LIMIT)


def _dot(a, b):
    return jnp.dot(a, b, preferred_element_type=F32)


def _dot_nt(a, b):
    return lax.dot_general(a, b, (((1,), (1,)), ((), ())), preferred_element_type=F32)


def _split2(x):
    hi = x.astype(BF16)
    lo = (x - hi.astype(F32)).astype(BF16)
    return hi, lo


def _dot3(a_hi, a_lo, b_hi, b_lo):
    return _dot(a_hi, b_hi) + _dot(a_lo, b_hi) + _dot(a_hi, b_lo)


def _rms(x, g):
    var = jnp.mean(x * x, axis=-1, keepdims=True)
    return x * lax.rsqrt(var + EPS) * g


def _ada_kernel(c_ref, w_ref, b_ref, o_ref):
    c_hi, c_lo = _split2(c_ref[...])
    w_hi, w_lo = _split2(w_ref[...])
    o_ref[...] = _dot3(c_hi, c_lo, w_hi, w_lo) + b_ref[...]


def _ada(c, w, b):
    rows, d = c.shape
    n = w.shape[1]
    tn = 1024
    return pl.pallas_call(
        _ada_kernel,
        grid=(n // tn,),
        in_specs=[pl.BlockSpec((rows, d), lambda j: (0, 0)),
                  pl.BlockSpec((d, tn), lambda j: (0, j)),
                  pl.BlockSpec((1, tn), lambda j: (0, j))],
        out_specs=pl.BlockSpec((rows, tn), lambda j: (0, j)),
        out_shape=SDS((rows, n), F32),
        compiler_params=_params("arbitrary"),
        name="ada",
    )(c, w, b)


def _inproj_kernel(x_ref, sc_ref, sh_ref, g_ref, wm_ref, wg_ref, wih_ref, wil_ref,
                   qa_ref, ka_ref, va_ref, qb_ref, kb_ref, vb_ref,
                   kab_ref, vab_ref, kbb_ref, vbb_ref,
                   gate_ref, iq_ref, ik_ref, wi_ref, kidx_ref, wit_ref, *, q_scale):
    dh = N_HEADS * HEAD_DIM
    h = _rms(x_ref[...], g_ref[...]) * (1.0 + sc_ref[0]) + sh_ref[0]
    h_hi, h_lo = _split2(h)

    def seg(n):
        return _dot(h_hi, wm_ref[:, n * dh:(n + 1) * dh])

    qa_ref[...] = (seg(0) * q_scale).astype(BF16)
    ka = seg(1)
    ka_ref[0] = ka.T
    kab_ref[...] = ka.astype(BF16)
    va = seg(2)
    va_ref[0] = va.T
    vab_ref[...] = va.astype(BF16)
    qb_ref[...] = (seg(3) * q_scale).astype(BF16)
    kb = seg(4)
    kb_ref[0] = kb.T
    kbb_ref[...] = kb.astype(BF16)
    vb = seg(5)
    vb_ref[0] = vb.T
    vbb_ref[...] = vb.astype(BF16)

    d = x_ref.shape[1]
    for n in range(2):
        g = _dot(h_hi, wg_ref[:, n * d:(n + 1) * d])
        gate_ref[:, n * d:(n + 1) * d] = (1.0 / (1.0 + jnp.exp(-g))).astype(BF16)

    nq = N_IDX_HEADS * IDX_DIM
    idx = _dot3(h_hi, h_lo, wih_ref[...], wil_ref[...])
    iq_ref[...] = idx[:, :nq] * IDX_DIM ** -0.5
    ik_ref[...] = idx[:, nq:nq + LANES]
    kidx_ref[0] = idx[:, nq:nq + LANES].T[:IDX_DIM, :]
    wi = idx[:, nq + LANES:nq + 2 * LANES] * N_IDX_HEADS ** -0.5
    wi_ref[...] = wi
    wit_ref[...] = wi.T[:SUBLANES, :]


def _inproj(x, sc, sh, g, wm, wg, wih, wil, tm, rows_per_mod, seq_len, q_scale):
    m, d = x.shape
    dh = N_HEADS * HEAD_DIM
    r = sc.shape[1]
    if r == 1:
        mod_map = lambda i: ((i * tm) // rows_per_mod, 0, 0)
    else:
        mod_map = lambda i: (0, i, 0)
    row = lambda w: pl.BlockSpec((tm, w), lambda i: (i, 0))
    full = lambda a: pl.BlockSpec(a.shape, lambda i: (0, 0), pipeline_mode=pl.Buffered(1))
    nq = N_IDX_HEADS * IDX_DIM
    outs = [(dh, BF16), (dh, None), (dh, None), (dh, BF16), (dh, None), (dh, None),
            (dh, BF16), (dh, BF16), (dh, BF16), (dh, BF16),
            (2 * d, BF16), (nq, F32), (LANES, F32), (LANES, F32), (IDX_DIM, None)]
    tiles_per_seq = seq_len // tm
    token_minor = lambda w: pl.BlockSpec((1, w, tm), lambda i: (i // tiles_per_seq, 0, i % tiles_per_seq))
    return pl.pallas_call(
        functools.partial(_inproj_kernel, q_scale=q_scale),
        grid=(m // tm,),
        in_specs=[row(d), pl.BlockSpec((1, r, d), mod_map), pl.BlockSpec((1, r, d), mod_map),
                  full(g), full(wm), full(wg), full(wih), full(wil)],
        out_specs=[row(w) if dt else token_minor(w) for w, dt in outs]
                  + [pl.BlockSpec((SUBLANES, tm), lambda i: (0, i))],
        out_shape=[SDS((m, w), dt) if dt else SDS((m // seq_len, w, seq_len), F32) for w, dt in outs]
                  + [SDS((SUBLANES, m), F32)],
        compiler_params=_params("arbitrary"),
        name="inproj",
    )(x, sc, sh, g, wm, wg, wih, wil)


ROW_CHUNK = 64
MAX_GROUPS = 256


def _key_to_float(key):
    bits = jnp.where(key < 0, key ^ jnp.int32(0x7FFFFFFF), key)
    f = lax.bitcast_convert_type(bits, F32)
    return jnp.where(key < _KEY_NEG_INF, -jnp.inf, f)


def _float_to_key(f):
    bits = lax.bitcast_convert_type(f, I32)
    return jnp.where(bits < 0, bits ^ jnp.int32(0x7FFFFFFF), bits)


def _group_width(width):
    gw = min(width, 8 * LANES)
    while width % gw:
        gw -= LANES
    return gw


def _count(score_ref, r0, ngroups, gw, cf, strict):
    cfb = jnp.broadcast_to(cf, (ROW_CHUNK, LANES))

    def body(g, acc):
        base = pl.multiple_of(g * gw, gw)
        for c in range(gw // LANES):
            s = score_ref[pl.ds(r0, ROW_CHUNK), pl.ds(base + c * LANES, LANES)]
            hit = (s > cfb) if strict else (s >= cfb)
            acc = acc + jnp.where(hit, 1.0, 0.0)
        return acc

    acc = lax.fori_loop(0, ngroups, body, jnp.zeros((ROW_CHUNK, LANES), F32))
    return jnp.sum(acc, axis=1, keepdims=True)


def _kth_largest(score_ref, r0, ngroups, gw, k):
    ninf = jnp.full((ROW_CHUNK, LANES), -jnp.inf, F32)

    def bounds(g, carry):
        base = pl.multiple_of(g * gw, gw)
        tops = list(carry)
        for c in range(gw // LANES):
            s = score_ref[pl.ds(r0, ROW_CHUNK), pl.ds(base + c * LANES, LANES)]
            tops[c % 2] = jnp.maximum(tops[c % 2], s)
        return tuple(tops)

    top_even, top_odd = lax.fori_loop(0, ngroups, bounds, (ninf, ninf))
    upper = jnp.max(jnp.maximum(top_even, top_odd), axis=1, keepdims=True)
    lower = jnp.min(jnp.minimum(top_even, top_odd), axis=1, keepdims=True)
    lo0 = _float_to_key(lower)
    hi0 = _float_to_key(upper) + 1

    def n_active(lo, hi):
        return jnp.sum(jnp.where(hi > lo + 1, 1.0, 0.0))

    def cond(state):
        return state[2] > 0.0

    def step(state):
        lo, hi, _ = state
        mid = lax.shift_right_arithmetic(lo, 1) + lax.shift_right_arithmetic(hi, 1) + (lo & hi & 1)
        cnt = _count(score_ref, r0, ngroups, gw, _key_to_float(mid), strict=False)
        keep = cnt >= k
        lo = jnp.where(keep, mid, lo)
        hi = jnp.where(cnt == k, mid + 1, jnp.where(keep, hi, mid))
        return lo, hi, n_active(lo, hi)

    lo, _, _ = lax.while_loop(cond, step, (lo0, hi0, n_active(lo0, hi0)))
    return _key_to_float(lo)


def _emit_selection(score_ref, thr_ref, need_ref, ngroups, gw, nblk, bw, k, emit, emit_last=None):
    rows = score_ref.shape[0]
    assert k <= MAX_GROUPS and rows % ROW_CHUNK == 0

    def chunk(n, _):
        r0 = pl.multiple_of(n * ROW_CHUNK, ROW_CHUNK)
        thr = _kth_largest(score_ref, r0, ngroups, gw, k)
        thr_ref[pl.ds(r0, ROW_CHUNK), :] = thr
        need_ref[pl.ds(r0, ROW_CHUNK), :] = k - _count(score_ref, r0, ngroups, gw, thr, strict=True)
        return 0

    lax.fori_loop(0, rows // ROW_CHUNK, chunk, 0)
    thr = thr_ref[...]
    need = need_ref[...]
    finite = jnp.where(thr > -jnp.inf, 1.0, 0.0)
    before = (lax.broadcasted_iota(I32, (bw, bw), 0) < lax.broadcasted_iota(I32, (bw, bw), 1))
    before = jnp.where(before, 1.0, 0.0).astype(BF16)

    def body(j, carry, emit=emit):
        s = score_ref[:, pl.ds(pl.multiple_of(j * bw, bw), bw)]
        tie = jnp.where(s == thr, finite, 0.0)
        rank = _dot(tie.astype(BF16), before) + carry
        sel = jnp.where(s > thr, 1.0, jnp.where(rank < need, tie, 0.0))
        emit(j, sel)
        return carry + jnp.sum(tie, axis=1, keepdims=True)

    n_loop = nblk if emit_last is None else nblk - 1
    carry = lax.fori_loop(0, n_loop, body, jnp.zeros((rows, 1), F32))
    if emit_last is not None:
        body(nblk - 1, carry, emit=emit_last)


def _half_masks(shape):
    lane = lax.broadcasted_iota(I32, shape, 1)
    return lane < HEAD_DIM


def _index_kernel(iq_ref, wit_ref, ik_ref, mask_ref, score_ref, top_ref, *, tq, topk):
    i = pl.program_id(1)
    nblk_total = mask_ref.shape[2] // tq
    assert topk <= tq
    lo_half = _half_masks((tq, LANES))
    q_parts = []
    for h in range(N_IDX_HEADS):
        qp = iq_ref[0, :, (h // 2) * LANES:(h // 2 + 1) * LANES]
        swapped = pltpu.roll(qp, HEAD_DIM, axis=1)
        qh = jnp.where(lo_half, qp, swapped) if h % 2 == 0 else jnp.where(lo_half, swapped, qp)
        q_hi = qh.astype(BF16)
        q_lo = jnp.where(lo_half, qh - q_hi.astype(F32), 0.0).astype(BF16)
        q_parts.append(jnp.concatenate([q_hi, q_lo], axis=1))
    w_rows = [wit_ref[h:h + 1, :] for h in range(N_IDX_HEADS)]
    key = lax.broadcasted_iota(I32, (tq, tq), 0)
    qry = lax.broadcasted_iota(I32, (tq, tq), 1)
    top_ref[...] = jnp.full((tq, tq), -jnp.inf, F32)

    def score_block(j, _):
        base = pl.multiple_of(j * tq, tq)
        kk = ik_ref[0, pl.ds(base, tq), :]
        k_hi = kk.astype(BF16)
        k_hi_f32 = k_hi.astype(F32)
        k_parts = jnp.concatenate([jnp.where(lo_half, k_hi_f32, kk - k_hi_f32).astype(BF16), k_hi], axis=1)
        s = jnp.zeros((tq, tq), F32)
        for h in range(N_IDX_HEADS):
            d = _dot_nt(k_parts, q_parts[h])
            s = s + w_rows[h] * jnp.maximum(d, 0.0)
        s = jnp.where(key + (j - i) * tq <= qry, s, -jnp.inf)
        score_ref[pl.ds(base, tq), :] = s
        return s

    def score_pair(jj, _):
        top_ref[...] = jnp.maximum(top_ref[...], jnp.maximum(score_block(2 * jj, 0), score_block(2 * jj + 1, 0)))
        return 0

    lax.fori_loop(0, (i + 1) // 2, score_pair, 0)

    @pl.when((i + 1) % 2 == 1)
    def _():
        top_ref[...] = jnp.maximum(top_ref[...], score_block(i, 0))

    def count(cf, strict):
        cfb = jnp.broadcast_to(cf, (SUBLANES, tq))

        def body(j, accs):
            blk = score_ref[pl.ds(pl.multiple_of(j * tq, tq), tq), :]
            accs = list(accs)
            for r in range(tq // SUBLANES):
                s = blk[r * SUBLANES:(r + 1) * SUBLANES]
                hit = (s > cfb) if strict else (s >= cfb)
                accs[r % len(accs)] = accs[r % len(accs)] + jnp.where(hit, 1.0, 0.0)
            return tuple(accs)

        zero = jnp.zeros((SUBLANES, tq), F32)
        accs = lax.fori_loop(0, i + 1, body, (zero, zero, zero, zero))
        return jnp.sum((accs[0] + accs[1]) + (accs[2] + accs[3]), axis=0, keepdims=True)

    tops = top_ref[...]
    lo0 = _float_to_key(jnp.min(tops, axis=0, keepdims=True))
    hi0 = _float_to_key(jnp.max(tops, axis=0, keepdims=True)) + 1

    def n_active(lo, hi):
        return jnp.sum(jnp.where(hi > lo + 1, 1.0, 0.0))

    def step(state):
        lo, hi, _ = state
        mid = lax.shift_right_arithmetic(lo, 1) + lax.shift_right_arithmetic(hi, 1) + (lo & hi & 1)
        cnt = count(_key_to_float(mid), strict=False)
        keep = cnt >= topk
        lo = jnp.where(keep, mid, lo)
        hi = jnp.where(cnt == topk, mid + 1, jnp.where(keep, hi, mid))
        return lo, hi, n_active(lo, hi)

    at_least_zero = count(jnp.zeros((1, tq), F32), strict=False)
    above_zero = count(jnp.zeros((1, tq), F32), strict=True)
    zero_kth = jnp.logical_and(above_zero < topk, at_least_zero >= topk)
    lo0 = jnp.where(zero_kth, 0, jnp.where(above_zero >= topk, jnp.maximum(lo0, 1), lo0))
    hi0 = jnp.where(zero_kth, 1, jnp.where(at_least_zero < topk, jnp.minimum(hi0, 0), hi0))

    lo, _, _ = lax.while_loop(lambda state: state[2] > 0.0, step, (lo0, hi0, n_active(lo0, hi0)))
    thr = _key_to_float(lo)
    need = topk - count(thr, strict=True)
    finite = jnp.where(thr > -jnp.inf, 1.0, 0.0)
    before = jnp.where(qry < key, 1.0, 0.0).astype(BF16)
    eye = jnp.where(qry == key, 1.0, 0.0).astype(BF16)

    def emit(j, carry):
        base = pl.multiple_of(j * tq, tq)
        s = score_ref[pl.ds(base, tq), :]
        tie = jnp.where(s == thr, finite, 0.0)
        rank = _dot(before, tie.astype(BF16)) + carry
        sel = jnp.where(s > thr, 1.0, jnp.where(rank < need, tie, 0.0))
        mask_ref[0, :, pl.ds(base, tq)] = _dot_nt(eye, sel.astype(BF16)).astype(jnp.int8)
        return carry + jnp.sum(tie, axis=0, keepdims=True)

    carry = lax.fori_loop(0, (i + 1) // 2, lambda jj, c: emit(2 * jj + 1, emit(2 * jj, c)), jnp.zeros((1, tq), F32))

    @pl.when((i + 1) % 2 == 1)
    def _():
        emit(i, carry)

    def clear(j, _):
        mask_ref[0, :, pl.ds(pl.multiple_of(j * tq, tq), tq)] = jnp.zeros((tq, tq), jnp.int8)
        return 0

    lax.fori_loop(i + 1, nblk_total, clear, 0)


def _prompt_index(iq, wit, ik, tq, topk):
    b, t, _ = iq.shape
    nq = t // tq
    return pl.pallas_call(
        functools.partial(_index_kernel, tq=tq, topk=float(topk)),
        grid=(b, nq),
        in_specs=[pl.BlockSpec((1, tq, iq.shape[2]), lambda bi, i: (bi, i, 0)),
                  pl.BlockSpec((SUBLANES, tq), lambda bi, i: (0, bi * nq + i)),
                  pl.BlockSpec((1, t, LANES), lambda bi, i: (bi, 0, 0))],
        out_specs=pl.BlockSpec((1, tq, t), lambda bi, i: (bi, i, 0)),
        out_shape=SDS((b, t, t), jnp.int8),
        scratch_shapes=[pltpu.VMEM((t, tq), F32), pltpu.VMEM((tq, tq), F32)],
        compiler_params=_params("arbitrary", "arbitrary"),
        name="prompt_index",
    )(iq, wit, ik)


def _t5_bucket(rel):
    rel = jnp.maximum(rel, 0)
    lf = (jnp.log(jnp.maximum(rel, 1).astype(F32) / MAX_EXACT)
          / math.log(MAX_DISTANCE / MAX_EXACT) * (N_BUCKETS - MAX_EXACT))
    large = jnp.minimum(MAX_EXACT + lf.astype(I32), N_BUCKETS - 1)
    return jnp.where(rel < MAX_EXACT, rel, large)


def _bias_lookup(rb_ref, bucket, h):
    val = jnp.zeros(bucket.shape, F32)
    for b in range(N_BUCKETS):
        val = jnp.where(bucket == b, rb_ref[b, h], val)
    return val


def _bias_tiles_kernel(rb_ref, o_ref, *, tq):
    h = pl.program_id(0)
    row = lax.broadcasted_iota(I32, (tq, tq), 0)
    col = lax.broadcasted_iota(I32, (tq, tq), 1)
    far = rb_ref[N_BUCKETS - 1, h]
    for tile in range(2):
        o_ref[tile, 0] = (_bias_lookup(rb_ref, _t5_bucket(row - col + tile * tq), h) - far) * LOG2E
    o_ref[2, 0] = jnp.zeros((tq, tq), F32)


def _bias_tiles(rel_bias, tq):
    assert tq >= MAX_DISTANCE
    return pl.pallas_call(
        functools.partial(_bias_tiles_kernel, tq=tq),
        grid=(N_HEADS,),
        in_specs=[pl.BlockSpec(memory_space=pltpu.SMEM)],
        out_specs=pl.BlockSpec((3, 1, tq, tq), lambda h: (0, h, 0, 0)),
        out_shape=SDS((3, N_HEADS, tq, tq), F32),
        compiler_params=_params("arbitrary"),
        name="bias_tiles",
    )(rel_bias)


def _pair_queries(q_ref, tq):
    lo_half = _half_masks((tq, LANES))
    out = []
    for p in range(N_PAIRS):
        qp = q_ref[0, :, p * LANES:(p + 1) * LANES].astype(F32)
        out.append((jnp.where(lo_half, qp, 0.0).astype(BF16), jnp.where(lo_half, 0.0, qp).astype(BF16)))
    return out, lo_half


def _dsa_kernel(q_ref, k_ref, v_ref, mask_ref, bias_ref, o_ref, logit_ref, madd_ref, m_ref, acc_ref, *, tq):
    i = pl.program_id(1)
    tk = 2 * tq
    qs, lo_half = _pair_queries(q_ref, tq)
    m_ref[...] = jnp.full(m_ref.shape, NEG, F32)
    acc_ref[...] = jnp.zeros(acc_ref.shape, F32)
    keep_lo = jnp.where(_half_masks((tk, LANES)), 1.0, 0.0).astype(BF16)
    keep_hi = 1.0 - keep_lo

    def pair_step(jj, near):
        base = pl.multiple_of(jj * tk, tk)
        madd_ref[...] = jnp.where(mask_ref[0, :, pl.ds(base, tk)].astype(I32) != 0, 0.0, NEG)
        for p in range(N_PAIRS):
            for u in range(2):
                kp = k_ref[0, pl.ds(base + u * tq, tq), p * LANES:(p + 1) * LANES]
                for hh in range(2):
                    h = 2 * p + hh
                    lm = _dot_nt(qs[p][hh], kp) + madd_ref[:, u * tq:(u + 1) * tq]
                    if near:
                        lm = lm + bias_ref[jnp.clip(i - (2 * jj + u), 0, 2), h]
                    logit_ref[h, :, u * tq:(u + 1) * tq] = lm
        for p in range(N_PAIRS):
            vp = v_ref[0, pl.ds(base, tk), p * LANES:(p + 1) * LANES]
            for hh in range(2):
                h = 2 * p + hh
                m_old = m_ref[h]
                m_new = jnp.maximum(m_old, jnp.max(logit_ref[h], axis=1, keepdims=True))
                m_ref[h] = m_new
                pexp = jnp.concatenate(
                    [jnp.exp2(logit_ref[h, :, c * LANES:(c + 1) * LANES] - m_new).astype(BF16)
                     for c in range(tk // LANES)], axis=1)
                vaug = vp * keep_lo + keep_hi if hh == 0 else vp * keep_hi + keep_lo
                acc_ref[h] = jnp.exp2(m_old - m_new) * acc_ref[h] + _dot(pexp, vaug)

    n_pairs = (i + 2) // 2
    n_far = jnp.maximum(n_pairs - 2, 0)

    def far_body(jj, _):
        pair_step(jj, False)
        return 0

    def near_body(jj, _):
        pair_step(jj, True)
        return 0

    lax.fori_loop(0, n_far, far_body, 0)
    lax.fori_loop(n_far, n_pairs, near_body, 0)
    for p in range(N_PAIRS):
        outs = []
        for hh in range(2):
            acc = acc_ref[2 * p + hh]
            outs.append(acc / pltpu.roll(acc, HEAD_DIM, axis=1))
        o_ref[0, :, p * LANES:(p + 1) * LANES] = jnp.where(lo_half, outs[0], outs[1])


def _prompt_dsa(q, k, v, mask, bias, tq):
    b, t, dh = q.shape
    assert t % (2 * tq) == 0
    once = pl.Buffered(1)
    seq = pl.BlockSpec((1, t, dh), lambda bi, i: (bi, 0, 0), pipeline_mode=once)
    return pl.pallas_call(
        functools.partial(_dsa_kernel, tq=tq),
        grid=(b, t // tq),
        in_specs=[pl.BlockSpec((1, tq, dh), lambda bi, i: (bi, i, 0)), seq, seq,
                  pl.BlockSpec((1, tq, t), lambda bi, i: (bi, i, 0)),
                  pl.BlockSpec(bias.shape, lambda bi, i: (0, 0, 0, 0), pipeline_mode=once)],
        out_specs=pl.BlockSpec((1, tq, dh), lambda bi, i: (bi, i, 0)),
        out_shape=SDS((b, t, dh), F32),
        scratch_shapes=[pltpu.VMEM((N_HEADS, tq, 2 * tq), F32), pltpu.VMEM((tq, 2 * tq), F32),
                        pltpu.VMEM((N_HEADS, tq, LANES), F32), pltpu.VMEM((N_HEADS, tq, LANES), F32)],
        compiler_params=_params("arbitrary", "arbitrary"),
        name="prompt_dsa",
    )(q, k, v, mask, bias)


def _sb_kernel(q_ref, k_ref, v_ref, o_ref, z_ref, tail_ref, c_ref, acc_ref, *, tq):
    i = pl.program_id(1)
    qs, lo_half = _pair_queries(q_ref, tq)
    c_ref[...] = jnp.zeros(c_ref.shape, F32)
    acc_ref[...] = jnp.zeros(acc_ref.shape, F32)
    row = lax.broadcasted_iota(I32, (tq, tq), 0)
    col = lax.broadcasted_iota(I32, (tq, tq), 1)
    not_before = jnp.where(row >= col, 1.0, 0.0).astype(BF16)
    earlier = col < row

    def block(j, diag):
        base = pl.multiple_of(j * tq, tq)
        for p in range(N_PAIRS):
            kp = k_ref[0, pl.ds(base, tq), p * LANES:(p + 1) * LANES]
            for hh in range(2):
                z_ref[2 * p + hh] = _dot_nt(qs[p][hh], kp)
        for h in range(N_HEADS):
            z = z_ref[h]
            sp = jnp.maximum(z, jnp.log2(1.0 + jnp.exp2(jnp.minimum(z, EXP2_CLAMP))))
            if diag:
                sp = jnp.where(earlier, sp, 0.0)
            tail_ref[h] = _dot(sp.astype(BF16), not_before)
        for p in range(N_PAIRS):
            vp = v_ref[0, pl.ds(base, tq), p * LANES:(p + 1) * LANES]
            pvs = []
            for hh in range(2):
                h = 2 * p + hh
                c_old = c_ref[h]
                chunks = []
                for n in range(tq // LANES):
                    sl = slice(n * LANES, (n + 1) * LANES)
                    a = jnp.exp2(z_ref[h, :, sl] - tail_ref[h, :, sl] - c_old)
                    if diag:
                        a = jnp.where(earlier[:, sl], a, 0.0)
                    chunks.append(a.astype(BF16))
                c_ref[h] = c_old + tail_ref[h, :, :1]
                pvs.append(_dot(jnp.concatenate(chunks, axis=1), vp))
            sl = slice(p * LANES, (p + 1) * LANES)
            acc_ref[:, sl] = acc_ref[:, sl] + jnp.where(lo_half, pvs[0], pvs[1])

    block(i, True)

    def body(step, _):
        block(i - 1 - step, False)
        return 0

    lax.fori_loop(0, i, body, 0)
    o_ref[0] = acc_ref[...]


def _prompt_sb(q, k, v, tq):
    b, t, dh = q.shape
    seq = pl.BlockSpec((1, t, dh), lambda bi, i: (bi, 0, 0), pipeline_mode=pl.Buffered(1))
    return pl.pallas_call(
        functools.partial(_sb_kernel, tq=tq),
        grid=(b, t // tq),
        in_specs=[pl.BlockSpec((1, tq, dh), lambda bi, i: (bi, i, 0)), seq, seq],
        out_specs=pl.BlockSpec((1, tq, dh), lambda bi, i: (bi, i, 0)),
        out_shape=SDS((b, t, dh), F32),
        scratch_shapes=[pltpu.VMEM((N_HEADS, tq, tq), F32), pltpu.VMEM((N_HEADS, tq, tq), F32),
                        pltpu.VMEM((N_HEADS, tq, LANES), F32),
                        pltpu.VMEM((tq, dh), F32)],
        compiler_params=_params("arbitrary", "arbitrary"),
        name="prompt_sb",
    )(q, k, v)


def _merge_kernel(ya_ref, yb_ref, gate_ref, x_ref, g1_ref, n_ref, wa_ref, wb_ref, wo_ref, o_ref):
    d = x_ref.shape[1]
    ma = _dot(ya_ref[...].astype(BF16), wa_ref[...])
    mb = _dot(yb_ref[...].astype(BF16), wb_ref[...])
    merged = gate_ref[:, :d].astype(F32) * ma + gate_ref[:, d:].astype(F32) * mb
    mix = _dot(merged.astype(BF16), wo_ref[...])
    o_ref[...] = x_ref[...] + g1_ref[0] * _rms(mix, n_ref[...])


def _mod_spec(mod, tm, rows_per_mod):
    d = mod.shape[2]
    if mod.shape[1] == 1:
        return pl.BlockSpec((1, 1, d), lambda i: ((i * tm) // rows_per_mod, 0, 0))
    return pl.BlockSpec((1, tm, d), lambda i: (0, i, 0))


def _merge(ya, yb, gate, x, g1, norm, wa, wb, wo, tm, rows_per_mod):
    m, d = x.shape
    row = lambda a: pl.BlockSpec((tm, a.shape[1]), lambda i: (i, 0))
    full = lambda a: pl.BlockSpec(a.shape, lambda i: (0, 0), pipeline_mode=pl.Buffered(1))
    return pl.pallas_call(
        _merge_kernel,
        grid=(m // tm,),
        in_specs=[row(ya), row(yb), row(gate), row(x), _mod_spec(g1, tm, rows_per_mod),
                  full(norm), full(wa), full(wb), full(wo)],
        out_specs=pl.BlockSpec((tm, d), lambda i: (i, 0)),
        out_shape=SDS((m, d), F32),
        compiler_params=_params("arbitrary"),
        name="merge",
    )(ya, yb, gate, x, g1, norm, wa, wb, wo)


def _ffn_kernel(x_ref, sc_ref, sh_ref, g2_ref, npre_ref, npost_ref, wg_ref, wu_ref, wd_ref, o_ref, *, chunk):
    x = x_ref[...]
    h = (_rms(x, npre_ref[...]) * (1.0 + sc_ref[0]) + sh_ref[0]).astype(BF16)
    f = jnp.zeros(x.shape, F32)
    for c0 in range(0, wg_ref.shape[1], chunk):
        gate = _dot(h, wg_ref[:, c0:c0 + chunk])
        up = _dot(h, wu_ref[:, c0:c0 + chunk])
        act = gate / (1.0 + jnp.exp(-gate)) * up
        f = f + _dot(act.astype(BF16), wd_ref[c0:c0 + chunk, :])
    o_ref[...] = x + g2_ref[0] * _rms(f, npost_ref[...])


def _ffn(x, sc, sh, g2, npre, npost, wg, wu, wd, tm, rows_per_mod):
    m, d = x.shape
    full = lambda a: pl.BlockSpec(a.shape, lambda i: (0, 0), pipeline_mode=pl.Buffered(1))
    mod = lambda a: _mod_spec(a, tm, rows_per_mod)
    return pl.pallas_call(
        functools.partial(_ffn_kernel, chunk=256),
        grid=(m // tm,),
        in_specs=[pl.BlockSpec((tm, d), lambda i: (i, 0)), mod(sc), mod(sh), mod(g2),
                  full(npre), full(npost), full(wg), full(wu), full(wd)],
        out_specs=pl.BlockSpec((tm, d), lambda i: (i, 0)),
        out_shape=SDS((m, d), F32),
        compiler_params=_params("arbitrary"),
        name="ffn",
    )(x, sc, sh, g2, npre, npost, wg, wu, wd)


def _sample_score_kernel(pt_ref, q_ref, w_ref, knew_ref, *rest, pages):
    page_refs = rest[:pages]
    past_ref, new_ref = rest[pages:]
    q_hi, q_lo = _split2(q_ref[0])
    w = w_ref[0]
    t = q_ref.shape[1] // N_IDX_HEADS

    def score(keys):
        k_hi, k_lo = _split2(keys)
        r = jnp.maximum(_dot3(q_hi, q_lo, k_hi, k_lo), 0.0) * w
        s = r[0:t]
        for h in range(1, N_IDX_HEADS):
            s = s + r[h * t:(h + 1) * t]
        return s

    for n in range(pages):
        past_ref[0, :, n * LANES:(n + 1) * LANES] = score(page_refs[n][...])
    new_ref[0] = score(knew_ref[0])


def _sample_scores(page_table, q_rows, w_rows, k_new, cache_k_idx, pages):
    nb, n_pages = page_table.shape
    page = cache_k_idx.shape[2]
    assert page == LANES
    rows = q_rows.shape[1]
    t = rows // N_IDX_HEADS

    def page_spec(n):
        return pl.BlockSpec((None, None, IDX_DIM, page), lambda b, g, pt: (0, pt[b, g * pages + n], 0, 0))

    grid_spec = pltpu.PrefetchScalarGridSpec(
        num_scalar_prefetch=1,
        grid=(nb, n_pages // pages),
        in_specs=[pl.BlockSpec((1, rows, IDX_DIM), lambda b, g, pt: (b, 0, 0)),
                  pl.BlockSpec((1, rows, LANES), lambda b, g, pt: (b, 0, 0)),
                  pl.BlockSpec((1, IDX_DIM, page), lambda b, g, pt: (b, 0, 0))]
                 + [page_spec(n) for n in range(pages)],
        out_specs=[pl.BlockSpec((1, t, pages * page), lambda b, g, pt: (b, 0, g)),
                   pl.BlockSpec((1, t, page), lambda b, g, pt: (b, 0, 0))],
    )
    return pl.pallas_call(
        functools.partial(_sample_score_kernel, pages=pages),
        grid_spec=grid_spec,
        out_shape=[SDS((nb, t, n_pages * page), F32), SDS((nb, t, page), F32)],
        compiler_params=_params("arbitrary", "arbitrary"),
        name="sample_scores",
    )(page_table, q_rows, w_rows, k_new, *([cache_k_idx.transpose(0, 1, 3, 2)] * pages))


def _sample_select_kernel(past_ref, new_ref, mpast_ref, mnew_ref, score_ref, thr_ref, need_ref, *, t, topk):
    rows, past = past_ref.shape
    score_ref[:, :past] = past_ref[...]
    q_t = lax.broadcasted_iota(I32, (rows, LANES), 0) % t
    col = lax.broadcasted_iota(I32, (rows, LANES), 1)
    score_ref[:, past:] = jnp.where(col <= q_t, new_ref[...], -jnp.inf)

    def emit(j, sel):
        mpast_ref[:, pl.ds(pl.multiple_of(j * LANES, LANES), LANES)] = sel

    def emit_new(j, sel):
        mnew_ref[...] = sel

    width = score_ref.shape[1]
    gw = _group_width(width)
    _emit_selection(score_ref, thr_ref, need_ref, width // gw, gw, width // LANES, LANES, topk, emit, emit_new)


def _sample_select(s_past, s_new, t, topk, tr):
    rows, past = s_past.shape
    width = past + LANES
    return pl.pallas_call(
        functools.partial(_sample_select_kernel, t=t, topk=float(topk)),
        grid=(rows // tr,),
        in_specs=[pl.BlockSpec((tr, past), lambda i: (i, 0)), pl.BlockSpec((tr, LANES), lambda i: (i, 0))],
        out_specs=[pl.BlockSpec((tr, past), lambda i: (i, 0)), pl.BlockSpec((tr, LANES), lambda i: (i, 0))],
        out_shape=[SDS((rows, past), F32), SDS((rows, LANES), F32)],
        scratch_shapes=[pltpu.VMEM((tr, width), F32), pltpu.VMEM((tr, 1), F32), pltpu.VMEM((tr, 1), F32)],
        compiler_params=_params("arbitrary"),
        name="sample_select",
    )(s_past, s_new)


def _sample_bias_kernel(rb_ref, o_ref, *, t, page):
    rows = N_HEADS * t
    row = lax.broadcasted_iota(I32, (rows, page), 0)
    key = lax.broadcasted_iota(I32, (rows, page), 1)
    q_t, head = row % t, row // t
    for tile in range(3):
        bucket = _t5_bucket(q_t - key + tile * page)
        val = jnp.zeros((rows, page), F32)
        for h in range(N_HEADS):
            val = jnp.where(head == h, _bias_lookup(rb_ref, bucket, h), val)
        o_ref[tile] = val


def _sample_bias(rel_bias, t, page):
    assert page >= MAX_DISTANCE
    return pl.pallas_call(
        functools.partial(_sample_bias_kernel, t=t, page=page),
        in_specs=[pl.BlockSpec(memory_space=pltpu.SMEM)],
        out_shape=SDS((3, N_HEADS * t, page), F32),
        name="sample_bias",
    )(rel_bias)


def _sample_attn_kernel(pt_ref, qa_ref, qb_ref, knew_a_ref, vnew_a_ref, knew_b_ref, vnew_b_ref,
                        fnew_ref, flags_ref, bias_ref, *rest, pages, t):
    ka_refs = rest[0 * pages:1 * pages]
    va_refs = rest[1 * pages:2 * pages]
    kb_refs = rest[2 * pages:3 * pages]
    vb_refs = rest[3 * pages:4 * pages]
    ya_ref, yb_ref, m_ref, l_ref, acca_ref, c_ref, accb_ref = rest[4 * pages:]
    g = pl.program_id(1)
    n_groups = pl.num_programs(1)
    rows = N_HEADS * t
    qa = qa_ref[0]
    qb = qb_ref[0]
    srow = lax.broadcasted_iota(I32, (LANES, LANES), 0)
    scol = lax.broadcasted_iota(I32, (LANES, LANES), 1)
    not_before = jnp.where(srow >= scol, 1.0, 0.0).astype(BF16)
    not_before = jnp.concatenate([not_before, not_before], axis=0)

    def scores(q, k):
        return _dot(q, k.reshape(N_HEADS * HEAD_DIM, LANES).astype(BF16))

    def weighted(p, v):
        return _dot_nt(p.astype(BF16), v.reshape(N_HEADS * HEAD_DIM, LANES).astype(BF16))

    def dsa_update(blocks):
        logits = [scores(qa, k) + bias + (1.0 - jnp.concatenate([flags] * N_HEADS, axis=0)) * NEG
                  for k, _, bias, flags in blocks]
        m_old = m_ref[...]
        m_new = m_old
        for lg in logits:
            m_new = jnp.maximum(m_new, jnp.max(lg, axis=1, keepdims=True))
        alpha = jnp.exp(m_old - m_new)
        l_new = alpha * l_ref[...]
        contrib = jnp.zeros(acca_ref.shape, F32)
        for lg, (_, v, _, _) in zip(logits, blocks):
            pexp = jnp.exp(lg - m_new)
            l_new = l_new + jnp.sum(pexp, axis=1, keepdims=True)
            contrib = contrib + weighted(pexp, v)
        m_ref[...] = m_new
        l_ref[...] = l_new
        acca_ref[...] = alpha * acca_ref[...] + contrib

    def sb_update(blocks):
        zs = [scores(qb, k) for k, _, _ in blocks]
        tails = []
        for z, (_, _, earlier) in zip(zs, blocks):
            sp = jnp.maximum(z, 0.0) + jnp.log(1.0 + jnp.exp(-jnp.abs(z)))
            if earlier is not None:
                sp = jnp.where(earlier, sp, 0.0)
            hi, lo = _split2(sp)
            tails.append(_dot(jnp.concatenate([hi, lo], axis=1), not_before))
        carry = c_ref[...]
        contrib = jnp.zeros(accb_ref.shape, F32)
        for z, tail, (_, v, earlier) in zip(zs, tails, blocks):
            a = jnp.exp(z - tail - carry)
            if earlier is not None:
                a = jnp.where(earlier, a, 0.0)
            carry = carry + tail[:, :1]
            contrib = contrib + weighted(a, v)
        c_ref[...] = carry
        accb_ref[...] = accb_ref[...] + contrib

    @pl.when(g == 0)
    def _():
        m_ref[...] = jnp.full(m_ref.shape, NEG, F32)
        l_ref[...] = jnp.zeros(l_ref.shape, F32)
        acca_ref[...] = jnp.zeros(acca_ref.shape, F32)
        c_ref[...] = jnp.zeros(c_ref.shape, F32)
        accb_ref[...] = jnp.zeros(accb_ref.shape, F32)
        key = lax.broadcasted_iota(I32, (rows, LANES), 1)
        q_t = lax.broadcasted_iota(I32, (rows, LANES), 0) % t
        dsa_update([(knew_a_ref[0], vnew_a_ref[0], bias_ref[0], fnew_ref[0])])
        sb_update([(knew_b_ref[0], vnew_b_ref[0], key < q_t)])

    blocks = []
    for n in range(pages):
        flags = flags_ref[0, :, (pages - 1 - n) * LANES:(pages - n) * LANES]
        bias = bias_ref[2] if n else jnp.where(g == 0, bias_ref[1], bias_ref[2])
        blocks.append((ka_refs[n][...], va_refs[n][...], bias, flags))
    dsa_update(blocks)
    sb_update([(kb_refs[n][...], vb_refs[n][...], None) for n in range(pages)])

    @pl.when(g == n_groups - 1)
    def _():
        ya_ref[0] = acca_ref[...] / l_ref[...]
        yb_ref[0] = accb_ref[...]


def _sample_attn(page_table, qa_rows, qb_rows, knew_a, vnew_a, knew_b, vnew_b, flags_new, flags_past, bias,
                 cache_k_a, cache_v_a, cache_k_b, cache_v_b, pages, t):
    nb, n_pages = page_table.shape
    rows = N_HEADS * t
    n_groups = n_pages // pages
    dh = N_HEADS * HEAD_DIM
    page_shape = (N_HEADS, HEAD_DIM, LANES)

    def page_spec(n):
        return pl.BlockSpec((None, None) + page_shape,
                            lambda b, g, pt: (0, pt[b, n_pages - 1 - (g * pages + n)], 0, 0, 0))

    per_seq = lambda shape: pl.BlockSpec((1,) + shape, lambda b, g, pt: (b,) + (0,) * len(shape))
    grid_spec = pltpu.PrefetchScalarGridSpec(
        num_scalar_prefetch=1,
        grid=(nb, n_groups),
        in_specs=[per_seq((rows, dh)), per_seq((rows, dh)),
                  per_seq(page_shape), per_seq(page_shape), per_seq(page_shape), per_seq(page_shape),
                  per_seq((t, LANES)),
                  pl.BlockSpec((1, t, pages * LANES), lambda b, g, pt: (b, 0, n_groups - 1 - g)),
                  pl.BlockSpec(bias.shape, lambda b, g, pt: (0, 0, 0))]
                 + [page_spec(n) for n in range(pages)] * 4,
        out_specs=[per_seq((rows, dh)), per_seq((rows, dh))],
        scratch_shapes=[pltpu.VMEM((rows, 1), F32), pltpu.VMEM((rows, 1), F32), pltpu.VMEM((rows, dh), F32),
                        pltpu.VMEM((rows, 1), F32), pltpu.VMEM((rows, dh), F32)],
    )
    key_minor = lambda c: c.transpose(0, 1, 3, 4, 2)
    return pl.pallas_call(
        functools.partial(_sample_attn_kernel, pages=pages, t=t),
        grid_spec=grid_spec,
        out_shape=[SDS((nb, rows, dh), F32), SDS((nb, rows, dh), F32)],
        compiler_params=_params("arbitrary", "arbitrary"),
        name="sample_attn",
    )(page_table, qa_rows, qb_rows, knew_a, vnew_a, knew_b, vnew_b, flags_new, flags_past, bias,
      *([key_minor(cache_k_a)] * pages), *([key_minor(cache_v_a)] * pages),
      *([key_minor(cache_k_b)] * pages), *([key_minor(cache_v_b)] * pages))


def _largest_divisor(n, cap):
    d = min(n, cap)
    while n % d:
        d -= 1
    return d


def kernel(x_prompt, x_sample, c_prompt, c_sample, cache_k_a, cache_v_a, cache_k_idx, cache_k_b, cache_v_b,
           page_table, rel_bias, w_ada, b_ada, norm_pre_mix, norm_post_mix, norm_pre_ffn, norm_post_ffn,
           w_in, w_br_a, w_br_b, w_out, w_ffn_gate, w_ffn_up, w_ffn_down):
    depth = w_in.shape[0]
    assert depth == 1
    nb_p, t_p, d = x_prompt.shape
    nb_s, t_s, _ = x_sample.shape
    n_pages = page_table.shape[1]
    page = cache_k_a.shape[2]
    past = n_pages * page
    dh = N_HEADS * HEAD_DIM
    nq = N_IDX_HEADS * IDX_DIM
    topk_p = min(INDEX_TOPK, t_p // 4)
    topk_s = min(INDEX_TOPK, (past + t_s) // 4)
    tq = 256
    assert t_p % tq == 0 and topk_p <= tq and t_s * N_HEADS <= LANES and page == LANES

    w = w_in[0]
    o = np.cumsum([0, dh, dh, dh, nq, IDX_DIM, N_IDX_HEADS, dh, dh, dh, d, d])
    col = lambda n: w[:, o[n]:o[n + 1]]
    w_main = jnp.concatenate([col(0), col(1), col(2), col(6), col(7), col(8)], axis=1).astype(BF16)
    w_gate = jnp.concatenate([col(9), col(10)], axis=1).astype(BF16)
    w_idx = jnp.concatenate([col(3), col(4), col(4), col(5),
                             jnp.zeros((d, LANES - N_IDX_HEADS), F32)], axis=1)
    w_idx_hi = w_idx.astype(BF16)
    w_idx_lo = (w_idx - w_idx_hi.astype(F32)).astype(BF16)
    wa, wb, wo = w_br_a[0].astype(BF16), w_br_b[0].astype(BF16), w_out[0].astype(BF16)
    wg, wu, wd = w_ffn_gate[0].astype(BF16), w_ffn_up[0].astype(BF16), w_ffn_down[0].astype(BF16)
    n_pre_mix, n_post_mix = norm_pre_mix[0][None], norm_post_mix[0][None]
    n_pre_ffn, n_post_ffn = norm_pre_ffn[0][None], norm_post_ffn[0][None]

    n_c = nb_p + nb_s
    c_all = jnp.concatenate([c_prompt, c_sample, jnp.zeros((-n_c % 8, d), F32)], axis=0)
    ada = _ada(c_all, w_ada[0], b_ada[0][None])
    mods_p = [m[:, None, :] for m in jnp.split(ada[:nb_p], 6, axis=-1)]
    mods_s = [jnp.repeat(m, t_s, axis=0)[None] for m in jnp.split(ada[nb_p:n_c], 6, axis=-1)]

    def dense_front(x2, mods, tm, rows_per_mod, seq_len, q_scale):
        return _inproj(x2, mods[1], mods[0], n_pre_mix, w_main, w_gate, w_idx_hi, w_idx_lo, tm, rows_per_mod,
                       seq_len, q_scale)

    def dense_back(ya, yb, gate, x2, mods, tm, rows_per_mod):
        x1 = _merge(ya, yb, gate, x2, mods[2], n_post_mix, wa, wb, wo, tm, rows_per_mod)
        return _ffn(x1, mods[4], mods[3], mods[5], n_pre_ffn, n_post_ffn, wg, wu, wd, tm, rows_per_mod)

    m_p = nb_p * t_p
    xp2 = x_prompt.reshape(m_p, d)
    tm_p = 512
    (qa, ka, va, qb, kb, vb, kab, vab, kbb, vbb, gate, iq, ik, wi, kidx, wit) = dense_front(
        xp2, mods_p, tm_p, t_p, t_p, LOG2E * HEAD_DIM ** -0.5)
    seq3 = lambda a: a.reshape(nb_p, t_p, a.shape[-1])
    mask = _prompt_index(seq3(iq), wit, seq3(ik), tq, topk_p)
    ya = _prompt_dsa(seq3(qa), seq3(kab), seq3(vab), mask, _bias_tiles(rel_bias, tq), tq)
    yb = _prompt_sb(seq3(qb), seq3(kbb), seq3(vbb), tq)
    y_p = dense_back(ya.reshape(m_p, dh), yb.reshape(m_p, dh), gate, xp2, mods_p, tm_p, t_p)
    heads_p = lambda a: a.reshape(nb_p, N_HEADS, HEAD_DIM, t_p).transpose(0, 3, 1, 2)[None]
    outs_p = (heads_p(ka), heads_p(va), kidx.transpose(0, 2, 1)[None], heads_p(kb), heads_p(vb))

    m_s = nb_s * t_s
    xs2 = x_sample.reshape(m_s, d)
    tm_s = _largest_divisor(m_s, 256)
    (qa, ka, va, qb, kb, vb, _, _, _, _, gate, iq, ik, wi, kidx, _) = dense_front(
        xs2, mods_s, tm_s, t_s, m_s, HEAD_DIM ** -0.5)
    q_rows = iq.reshape(nb_s, t_s, N_IDX_HEADS, IDX_DIM).transpose(0, 2, 1, 3).reshape(nb_s, N_IDX_HEADS * t_s, IDX_DIM)
    w_rows = wi.reshape(nb_s, t_s, LANES)[:, :, :N_IDX_HEADS].transpose(0, 2, 1).reshape(nb_s, N_IDX_HEADS * t_s, 1)
    w_rows = jnp.broadcast_to(w_rows, (nb_s, N_IDX_HEADS * t_s, LANES))
    pages_idx = _largest_divisor(n_pages, 16)
    k_new = jnp.pad(kidx[0].reshape(IDX_DIM, nb_s, t_s).transpose(1, 0, 2), ((0, 0), (0, 0), (0, page - t_s)))
    s_past, s_new = _sample_scores(page_table, q_rows, w_rows, k_new, cache_k_idx, pages_idx)
    sel_past, sel_new = _sample_select(s_past.reshape(m_s, past), s_new.reshape(m_s, page), t_s, topk_s,
                                       _largest_divisor(m_s, 128))
    head_rows = lambda a: (jnp.einsum('bthd,hg->bhtgd', a.reshape(nb_s, t_s, N_HEADS, HEAD_DIM).astype(F32),
                                      jnp.eye(N_HEADS, dtype=F32))
                           .reshape(nb_s, N_HEADS * t_s, dh).astype(BF16))
    as_page = lambda a: jnp.pad(a[0].reshape(N_HEADS, HEAD_DIM, nb_s, t_s).transpose(2, 0, 1, 3),
                                ((0, 0), (0, 0), (0, 0), (0, page - t_s)))
    pages_attn = _largest_divisor(n_pages, 8)
    ya, yb = _sample_attn(page_table, head_rows(qa), head_rows(qb),
                          as_page(ka), as_page(va), as_page(kb), as_page(vb),
                          sel_new.reshape(nb_s, t_s, page), sel_past.reshape(nb_s, t_s, past),
                          _sample_bias(rel_bias, t_s, page),
                          cache_k_a, cache_v_a, cache_k_b, cache_v_b, pages_attn, t_s)
    token_rows = lambda y: (jnp.diagonal(y.reshape(nb_s, N_HEADS, t_s, N_HEADS, HEAD_DIM), axis1=1, axis2=3)
                            .transpose(0, 1, 3, 2).reshape(m_s, dh))
    y_s = dense_back(token_rows(ya), token_rows(yb), gate, xs2, mods_s, tm_s, t_s)
    heads_s = lambda a: a[0].T.reshape(depth, nb_s, t_s, N_HEADS, HEAD_DIM)
    outs_s = (heads_s(ka), heads_s(va), kidx[0].T.reshape(depth, nb_s, t_s, IDX_DIM), heads_s(kb), heads_s(vb))

    return (y_p.reshape(nb_p, t_p, d), y_s.reshape(nb_s, t_s, d)) + outs_p + outs_s
```

```python
import functools
import math

import numpy as np
import jax
import jax.numpy as jnp
from jax import lax
from jax.experimental import pallas as pl
from jax.experimental.pallas import tpu as pltpu

F32 = jnp.float32
BF16 = jnp.bfloat16
I32 = jnp.int32
SDS = jax.ShapeDtypeStruct

HEAD_DIM = 64
N_HEADS = 8
N_PAIRS = N_HEADS // 2
N_IDX_HEADS = 4
IDX_DIM = 64
INDEX_TOPK = 256
N_BUCKETS = 32
MAX_EXACT = N_BUCKETS // 2
MAX_DISTANCE = 128
EPS = 1e-6
LANES = 128
SUBLANES = 8
NEG = -1e30
LOG2E = 1.4426950408889634
EXP2_CLAMP = 126.0
VMEM_LIMIT = 56 * 1024 * 1024

_KEY_NEG_INF = np.int32(0x807FFFFF - 2 ** 32)


def _params(*sem):
    return pltpu.CompilerParams(dimension_semantics=sem, vmem_limit_bytes=VMEM_LIMIT)


def _dot(a, b):
    return jnp.dot(a, b, preferred_element_type=F32)


def _dot_nt(a, b):
    return lax.dot_general(a, b, (((1,), (1,)), ((), ())), preferred_element_type=F32)


def _split2(x):
    hi = x.astype(BF16)
    lo = (x - hi.astype(F32)).astype(BF16)
    return hi, lo


def _dot3(a_hi, a_lo, b_hi, b_lo):
    return _dot(a_hi, b_hi) + _dot(a_lo, b_hi) + _dot(a_hi, b_lo)


def _rms(x, g):
    var = jnp.mean(x * x, axis=-1, keepdims=True)
    return x * lax.rsqrt(var + EPS) * g


def _ada_kernel(c_ref, w_ref, b_ref, o_ref):
    c_hi, c_lo = _split2(c_ref[...])
    w_hi, w_lo = _split2(w_ref[...])
    o_ref[...] = _dot3(c_hi, c_lo, w_hi, w_lo) + b_ref[...]


def _ada(c, w, b):
    rows, d = c.shape
    n = w.shape[1]
    tn = 1024
    return pl.pallas_call(
        _ada_kernel,
        grid=(n // tn,),
        in_specs=[pl.BlockSpec((rows, d), lambda j: (0, 0)),
                  pl.BlockSpec((d, tn), lambda j: (0, j)),
                  pl.BlockSpec((1, tn), lambda j: (0, j))],
        out_specs=pl.BlockSpec((rows, tn), lambda j: (0, j)),
        out_shape=SDS((rows, n), F32),
        compiler_params=_params("arbitrary"),
        name="ada",
    )(c, w, b)


def _inproj_kernel(x_ref, sc_ref, sh_ref, g_ref, wm_ref, wg_ref, wih_ref, wil_ref,
                   qa_ref, ka_ref, va_ref, qb_ref, kb_ref, vb_ref,
                   kab_ref, vab_ref, kbb_ref, vbb_ref,
                   gate_ref, iq_ref, ik_ref, wi_ref, kidx_ref, wit_ref, *, q_scale):
    dh = N_HEADS * HEAD_DIM
    h = _rms(x_ref[...], g_ref[...]) * (1.0 + sc_ref[0]) + sh_ref[0]
    h_hi, h_lo = _split2(h)

    def seg(n):
        return _dot(h_hi, wm_ref[:, n * dh:(n + 1) * dh])

    qa_ref[...] = (seg(0) * q_scale).astype(BF16)
    ka = seg(1)
    ka_ref[0] = ka.T
    kab_ref[...] = ka.astype(BF16)
    va = seg(2)
    va_ref[0] = va.T
    vab_ref[...] = va.astype(BF16)
    qb_ref[...] = (seg(3) * q_scale).astype(BF16)
    kb = seg(4)
    kb_ref[0] = kb.T
    kbb_ref[...] = kb.astype(BF16)
    vb = seg(5)
    vb_ref[0] = vb.T
    vbb_ref[...] = vb.astype(BF16)

    d = x_ref.shape[1]
    for n in range(2):
        g = _dot(h_hi, wg_ref[:, n * d:(n + 1) * d])
        gate_ref[:, n * d:(n + 1) * d] = (1.0 / (1.0 + jnp.exp(-g))).astype(BF16)

    nq = N_IDX_HEADS * IDX_DIM
    idx = _dot3(h_hi, h_lo, wih_ref[...], wil_ref[...])
    iq_ref[...] = idx[:, :nq] * IDX_DIM ** -0.5
    ik_ref[...] = idx[:, nq:nq + LANES]
    kidx_ref[0] = idx[:, nq:nq + LANES].T[:IDX_DIM, :]
    wi = idx[:, nq + LANES:nq + 2 * LANES] * N_IDX_HEADS ** -0.5
    wi_ref[...] = wi
    wit_ref[...] = wi.T[:SUBLANES, :]


def _inproj(x, sc, sh, g, wm, wg, wih, wil, tm, rows_per_mod, seq_len, q_scale):
    m, d = x.shape
    dh = N_HEADS * HEAD_DIM
    r = sc.shape[1]
    if r == 1:
        mod_map = lambda i: ((i * tm) // rows_per_mod, 0, 0)
    else:
        mod_map = lambda i: (0, i, 0)
    row = lambda w: pl.BlockSpec((tm, w), lambda i: (i, 0))
    full = lambda a: pl.BlockSpec(a.shape, lambda i: (0, 0), pipeline_mode=pl.Buffered(1))
    nq = N_IDX_HEADS * IDX_DIM
    outs = [(dh, BF16), (dh, None), (dh, None), (dh, BF16), (dh, None), (dh, None),
            (dh, BF16), (dh, BF16), (dh, BF16), (dh, BF16),
            (2 * d, BF16), (nq, F32), (LANES, F32), (LANES, F32), (IDX_DIM, None)]
    tiles_per_seq = seq_len // tm
    token_minor = lambda w: pl.BlockSpec((1, w, tm), lambda i: (i // tiles_per_seq, 0, i % tiles_per_seq))
    return pl.pallas_call(
        functools.partial(_inproj_kernel, q_scale=q_scale),
        grid=(m // tm,),
        in_specs=[row(d), pl.BlockSpec((1, r, d), mod_map), pl.BlockSpec((1, r, d), mod_map),
                  full(g), full(wm), full(wg), full(wih), full(wil)],
        out_specs=[row(w) if dt else token_minor(w) for w, dt in outs]
                  + [pl.BlockSpec((SUBLANES, tm), lambda i: (0, i))],
        out_shape=[SDS((m, w), dt) if dt else SDS((m // seq_len, w, seq_len), F32) for w, dt in outs]
                  + [SDS((SUBLANES, m), F32)],
        compiler_params=_params("arbitrary"),
        name="inproj",
    )(x, sc, sh, g, wm, wg, wih, wil)


ROW_CHUNK = 64
MAX_GROUPS = 256


def _key_to_float(key):
    bits = jnp.where(key < 0, key ^ jnp.int32(0x7FFFFFFF), key)
    f = lax.bitcast_convert_type(bits, F32)
    return jnp.where(key < _KEY_NEG_INF, -jnp.inf, f)


def _float_to_key(f):
    bits = lax.bitcast_convert_type(f, I32)
    return jnp.where(bits < 0, bits ^ jnp.int32(0x7FFFFFFF), bits)


def _group_width(width):
    gw = min(width, 8 * LANES)
    while width % gw:
        gw -= LANES
    return gw


def _count(score_ref, r0, ngroups, gw, cf, strict):
    cfb = jnp.broadcast_to(cf, (ROW_CHUNK, LANES))

    def body(g, acc):
        base = pl.multiple_of(g * gw, gw)
        for c in range(gw // LANES):
            s = score_ref[pl.ds(r0, ROW_CHUNK), pl.ds(base + c * LANES, LANES)]
            hit = (s > cfb) if strict else (s >= cfb)
            acc = acc + jnp.where(hit, 1.0, 0.0)
        return acc

    acc = lax.fori_loop(0, ngroups, body, jnp.zeros((ROW_CHUNK, LANES), F32))
    return jnp.sum(acc, axis=1, keepdims=True)


def _kth_largest(score_ref, r0, ngroups, gw, k):
    ninf = jnp.full((ROW_CHUNK, LANES), -jnp.inf, F32)

    def bounds(g, carry):
        base = pl.multiple_of(g * gw, gw)
        tops = list(carry)
        for c in range(gw // LANES):
            s = score_ref[pl.ds(r0, ROW_CHUNK), pl.ds(base + c * LANES, LANES)]
            tops[c % 2] = jnp.maximum(tops[c % 2], s)
        return tuple(tops)

    top_even, top_odd = lax.fori_loop(0, ngroups, bounds, (ninf, ninf))
    upper = jnp.max(jnp.maximum(top_even, top_odd), axis=1, keepdims=True)
    lower = jnp.min(jnp.minimum(top_even, top_odd), axis=1, keepdims=True)
    lo0 = _float_to_key(lower)
    hi0 = _float_to_key(upper) + 1

    def n_active(lo, hi):
        return jnp.sum(jnp.where(hi > lo + 1, 1.0, 0.0))

    def cond(state):
        return state[2] > 0.0

    def step(state):
        lo, hi, _ = state
        mid = lax.shift_right_arithmetic(lo, 1) + lax.shift_right_arithmetic(hi, 1) + (lo & hi & 1)
        cnt = _count(score_ref, r0, ngroups, gw, _key_to_float(mid), strict=False)
        keep = cnt >= k
        lo = jnp.where(keep, mid, lo)
        hi = jnp.where(cnt == k, mid + 1, jnp.where(keep, hi, mid))
        return lo, hi, n_active(lo, hi)

    lo, _, _ = lax.while_loop(cond, step, (lo0, hi0, n_active(lo0, hi0)))
    return _key_to_float(lo)


def _emit_selection(score_ref, thr_ref, need_ref, ngroups, gw, nblk, bw, k, emit, emit_last=None):
    rows = score_ref.shape[0]
    assert k <= MAX_GROUPS and rows % ROW_CHUNK == 0

    def chunk(n, _):
        r0 = pl.multiple_of(n * ROW_CHUNK, ROW_CHUNK)
        thr = _kth_largest(score_ref, r0, ngroups, gw, k)
        thr_ref[pl.ds(r0, ROW_CHUNK), :] = thr
        need_ref[pl.ds(r0, ROW_CHUNK), :] = k - _count(score_ref, r0, ngroups, gw, thr, strict=True)
        return 0

    lax.fori_loop(0, rows // ROW_CHUNK, chunk, 0)
    thr = thr_ref[...]
    need = need_ref[...]
    finite = jnp.where(thr > -jnp.inf, 1.0, 0.0)
    before = (lax.broadcasted_iota(I32, (bw, bw), 0) < lax.broadcasted_iota(I32, (bw, bw), 1))
    before = jnp.where(before, 1.0, 0.0).astype(BF16)

    def body(j, carry, emit=emit):
        s = score_ref[:, pl.ds(pl.multiple_of(j * bw, bw), bw)]
        tie = jnp.where(s == thr, finite, 0.0)
        rank = _dot(tie.astype(BF16), before) + carry
        sel = jnp.where(s > thr, 1.0, jnp.where(rank < need, tie, 0.0))
        emit(j, sel)
        return carry + jnp.sum(tie, axis=1, keepdims=True)

    n_loop = nblk if emit_last is None else nblk - 1
    carry = lax.fori_loop(0, n_loop, body, jnp.zeros((rows, 1), F32))
    if emit_last is not None:
        body(nblk - 1, carry, emit=emit_last)


def _half_masks(shape):
    lane = lax.broadcasted_iota(I32, shape, 1)
    return lane < HEAD_DIM


def _index_kernel(iq_ref, wit_ref, ik_ref, mask_ref, score_ref, top_ref, *, tq, topk):
    i = pl.program_id(1)
    nblk_total = mask_ref.shape[2] // tq
    assert topk <= tq
    lo_half = _half_masks((tq, LANES))
    q_parts = []
    for h in range(N_IDX_HEADS):
        qp = iq_ref[0, :, (h // 2) * LANES:(h // 2 + 1) * LANES]
        swapped = pltpu.roll(qp, HEAD_DIM, axis=1)
        qh = jnp.where(lo_half, qp, swapped) if h % 2 == 0 else jnp.where(lo_half, swapped, qp)
        q_hi = qh.astype(BF16)
        q_lo = jnp.where(lo_half, qh - q_hi.astype(F32), 0.0).astype(BF16)
        q_parts.append(jnp.concatenate([q_hi, q_lo], axis=1))
    w_rows = [wit_ref[h:h + 1, :] for h in range(N_IDX_HEADS)]
    key = lax.broadcasted_iota(I32, (tq, tq), 0)
    qry = lax.broadcasted_iota(I32, (tq, tq), 1)
    top_ref[...] = jnp.full((tq, tq), -jnp.inf, F32)

    def score_block(j, _):
        base = pl.multiple_of(j * tq, tq)
        kk = ik_ref[0, pl.ds(base, tq), :]
        k_hi = kk.astype(BF16)
        k_hi_f32 = k_hi.astype(F32)
        k_parts = jnp.concatenate([jnp.where(lo_half, k_hi_f32, kk - k_hi_f32).astype(BF16), k_hi], axis=1)
        s = jnp.zeros((tq, tq), F32)
        for h in range(N_IDX_HEADS):
            d = _dot_nt(k_parts, q_parts[h])
            s = s + w_rows[h] * jnp.maximum(d, 0.0)
        s = jnp.where(key + (j - i) * tq <= qry, s, -jnp.inf)
        score_ref[pl.ds(base, tq), :] = s
        return s

    def score_pair(jj, _):
        top_ref[...] = jnp.maximum(top_ref[...], jnp.maximum(score_block(2 * jj, 0), score_block(2 * jj + 1, 0)))
        return 0

    lax.fori_loop(0, (i + 1) // 2, score_pair, 0)

    @pl.when((i + 1) % 2 == 1)
    def _():
        top_ref[...] = jnp.maximum(top_ref[...], score_block(i, 0))

    def count(cf, strict):
        cfb = jnp.broadcast_to(cf, (SUBLANES, tq))

        def body(j, accs):
            blk = score_ref[pl.ds(pl.multiple_of(j * tq, tq), tq), :]
            accs = list(accs)
            for r in range(tq // SUBLANES):
                s = blk[r * SUBLANES:(r + 1) * SUBLANES]
                hit = (s > cfb) if strict else (s >= cfb)
                accs[r % len(accs)] = accs[r % len(accs)] + jnp.where(hit, 1.0, 0.0)
            return tuple(accs)

        zero = jnp.zeros((SUBLANES, tq), F32)
        accs = lax.fori_loop(0, i + 1, body, (zero, zero, zero, zero))
        return jnp.sum((accs[0] + accs[1]) + (accs[2] + accs[3]), axis=0, keepdims=True)

    tops = top_ref[...]
    lo0 = _float_to_key(jnp.min(tops, axis=0, keepdims=True))
    hi0 = _float_to_key(jnp.max(tops, axis=0, keepdims=True)) + 1

    def n_active(lo, hi):
        return jnp.sum(jnp.where(hi > lo + 1, 1.0, 0.0))

    def step(state):
        lo, hi, _ = state
        mid = lax.shift_right_arithmetic(lo, 1) + lax.shift_right_arithmetic(hi, 1) + (lo & hi & 1)
        cnt = count(_key_to_float(mid), strict=False)
        keep = cnt >= topk
        lo = jnp.where(keep, mid, lo)
        hi = jnp.where(cnt == topk, mid + 1, jnp.where(keep, hi, mid))
        return lo, hi, n_active(lo, hi)

    at_least_zero = count(jnp.zeros((1, tq), F32), strict=False)
    above_zero = count(jnp.zeros((1, tq), F32), strict=True)
    zero_kth = jnp.logical_and(above_zero < topk, at_least_zero >= topk)
    lo0 = jnp.where(zero_kth, 0, jnp.where(above_zero >= topk, jnp.maximum(lo0, 1), lo0))
    hi0 = jnp.where(zero_kth, 1, jnp.where(at_least_zero < topk, jnp.minimum(hi0, 0), hi0))

    lo, _, _ = lax.while_loop(lambda state: state[2] > 0.0, step, (lo0, hi0, n_active(lo0, hi0)))
    thr = _key_to_float(lo)
    need = topk - count(thr, strict=True)
    finite = jnp.where(thr > -jnp.inf, 1.0, 0.0)
    before = jnp.where(qry < key, 1.0, 0.0).astype(BF16)
    eye = jnp.where(qry == key, 1.0, 0.0).astype(BF16)

    def emit(j, carry):
        base = pl.multiple_of(j * tq, tq)
        s = score_ref[pl.ds(base, tq), :]
        tie = jnp.where(s == thr, finite, 0.0)
        rank = _dot(before, tie.astype(BF16)) + carry
        sel = jnp.where(s > thr, 1.0, jnp.where(rank < need, tie, 0.0))
        mask_ref[0, :, pl.ds(base, tq)] = _dot_nt(eye, sel.astype(BF16)).astype(jnp.int8)
        return carry + jnp.sum(tie, axis=0, keepdims=True)

    carry = lax.fori_loop(0, (i + 1) // 2, lambda jj, c: emit(2 * jj + 1, emit(2 * jj, c)), jnp.zeros((1, tq), F32))

    @pl.when((i + 1) % 2 == 1)
    def _():
        emit(i, carry)

    def clear(j, _):
        mask_ref[0, :, pl.ds(pl.multiple_of(j * tq, tq), tq)] = jnp.zeros((tq, tq), jnp.int8)
        return 0

    lax.fori_loop(i + 1, nblk_total, clear, 0)


def _prompt_index(iq, wit, ik, tq, topk):
    b, t, _ = iq.shape
    nq = t // tq
    return pl.pallas_call(
        functools.partial(_index_kernel, tq=tq, topk=float(topk)),
        grid=(b, nq),
        in_specs=[pl.BlockSpec((1, tq, iq.shape[2]), lambda bi, i: (bi, i, 0)),
                  pl.BlockSpec((SUBLANES, tq), lambda bi, i: (0, bi * nq + i)),
                  pl.BlockSpec((1, t, LANES), lambda bi, i: (bi, 0, 0))],
        out_specs=pl.BlockSpec((1, tq, t), lambda bi, i: (bi, i, 0)),
        out_shape=SDS((b, t, t), jnp.int8),
        scratch_shapes=[pltpu.VMEM((t, tq), F32), pltpu.VMEM((tq, tq), F32)],
        compiler_params=_params("arbitrary", "arbitrary"),
        name="prompt_index",
    )(iq, wit, ik)


def _t5_bucket(rel):
    rel = jnp.maximum(rel, 0)
    lf = (jnp.log(jnp.maximum(rel, 1).astype(F32) / MAX_EXACT)
          / math.log(MAX_DISTANCE / MAX_EXACT) * (N_BUCKETS - MAX_EXACT))
    large = jnp.minimum(MAX_EXACT + lf.astype(I32), N_BUCKETS - 1)
    return jnp.where(rel < MAX_EXACT, rel, large)


def _bias_lookup(rb_ref, bucket, h):
    val = jnp.zeros(bucket.shape, F32)
    for b in range(N_BUCKETS):
        val = jnp.where(bucket == b, rb_ref[b, h], val)
    return val


def _bias_tiles_kernel(rb_ref, o_ref, *, tq):
    h = pl.program_id(0)
    row = lax.broadcasted_iota(I32, (tq, tq), 0)
    col = lax.broadcasted_iota(I32, (tq, tq), 1)
    far = rb_ref[N_BUCKETS - 1, h]
    for tile in range(2):
        o_ref[tile, 0] = (_bias_lookup(rb_ref, _t5_bucket(row - col + tile * tq), h) - far) * LOG2E
    o_ref[2, 0] = jnp.zeros((tq, tq), F32)


def _bias_tiles(rel_bias, tq):
    assert tq >= MAX_DISTANCE
    return pl.pallas_call(
        functools.partial(_bias_tiles_kernel, tq=tq),
        grid=(N_HEADS,),
        in_specs=[pl.BlockSpec(memory_space=pltpu.SMEM)],
        out_specs=pl.BlockSpec((3, 1, tq, tq), lambda h: (0, h, 0, 0)),
        out_shape=SDS((3, N_HEADS, tq, tq), F32),
        compiler_params=_params("arbitrary"),
        name="bias_tiles",
    )(rel_bias)


def _pair_queries(q_ref, tq):
    lo_half = _half_masks((tq, LANES))
    out = []
    for p in range(N_PAIRS):
        qp = q_ref[0, :, p * LANES:(p + 1) * LANES].astype(F32)
        out.append((jnp.where(lo_half, qp, 0.0).astype(BF16), jnp.where(lo_half, 0.0, qp).astype(BF16)))
    return out, lo_half


def _dsa_kernel(q_ref, k_ref, v_ref, mask_ref, bias_ref, o_ref, logit_ref, madd_ref, m_ref, acc_ref, *, tq):
    i = pl.program_id(1)
    tk = 2 * tq
    qs, lo_half = _pair_queries(q_ref, tq)
    m_ref[...] = jnp.full(m_ref.shape, NEG, F32)
    acc_ref[...] = jnp.zeros(acc_ref.shape, F32)
    keep_lo = jnp.where(_half_masks((tk, LANES)), 1.0, 0.0).astype(BF16)
    keep_hi = 1.0 - keep_lo

    def pair_step(jj, near):
        base = pl.multiple_of(jj * tk, tk)
        madd_ref[...] = jnp.where(mask_ref[0, :, pl.ds(base, tk)].astype(I32) != 0, 0.0, NEG)
        for p in range(N_PAIRS):
            for u in range(2):
                kp = k_ref[0, pl.ds(base + u * tq, tq), p * LANES:(p + 1) * LANES]
                for hh in range(2):
                    h = 2 * p + hh
                    lm = _dot_nt(qs[p][hh], kp) + madd_ref[:, u * tq:(u + 1) * tq]
                    if near:
                        lm = lm + bias_ref[jnp.clip(i - (2 * jj + u), 0, 2), h]
                    logit_ref[h, :, u * tq:(u + 1) * tq] = lm
        for p in range(N_PAIRS):
            vp = v_ref[0, pl.ds(base, tk), p * LANES:(p + 1) * LANES]
            for hh in range(2):
                h = 2 * p + hh
                m_old = m_ref[h]
                m_new = jnp.maximum(m_old, jnp.max(logit_ref[h], axis=1, keepdims=True))
                m_ref[h] = m_new
                pexp = jnp.concatenate(
                    [jnp.exp2(logit_ref[h, :, c * LANES:(c + 1) * LANES] - m_new).astype(BF16)
                     for c in range(tk // LANES)], axis=1)
                vaug = vp * keep_lo + keep_hi if hh == 0 else vp * keep_hi + keep_lo
                acc_ref[h] = jnp.exp2(m_old - m_new) * acc_ref[h] + _dot(pexp, vaug)

    n_pairs = (i + 2) // 2
    n_far = jnp.maximum(n_pairs - 2, 0)

    def far_body(jj, _):
        pair_step(jj, False)
        return 0

    def near_body(jj, _):
        pair_step(jj, True)
        return 0

    lax.fori_loop(0, n_far, far_body, 0)
    lax.fori_loop(n_far, n_pairs, near_body, 0)
    for p in range(N_PAIRS):
        outs = []
        for hh in range(2):
            acc = acc_ref[2 * p + hh]
            outs.append(acc / pltpu.roll(acc, HEAD_DIM, axis=1))
        o_ref[0, :, p * LANES:(p + 1) * LANES] = jnp.where(lo_half, outs[0], outs[1])


def _prompt_dsa(q, k, v, mask, bias, tq):
    b, t, dh = q.shape
    assert t % (2 * tq) == 0
    once = pl.Buffered(1)
    seq = pl.BlockSpec((1, t, dh), lambda bi, i: (bi, 0, 0), pipeline_mode=once)
    return pl.pallas_call(
        functools.partial(_dsa_kernel, tq=tq),
        grid=(b, t // tq),
        in_specs=[pl.BlockSpec((1, tq, dh), lambda bi, i: (bi, i, 0)), seq, seq,
                  pl.BlockSpec((1, tq, t), lambda bi, i: (bi, i, 0)),
                  pl.BlockSpec(bias.shape, lambda bi, i: (0, 0, 0, 0), pipeline_mode=once)],
        out_specs=pl.BlockSpec((1, tq, dh), lambda bi, i: (bi, i, 0)),
        out_shape=SDS((b, t, dh), F32),
        scratch_shapes=[pltpu.VMEM((N_HEADS, tq, 2 * tq), F32), pltpu.VMEM((tq, 2 * tq), F32),
                        pltpu.VMEM((N_HEADS, tq, LANES), F32), pltpu.VMEM((N_HEADS, tq, LANES), F32)],
        compiler_params=_params("arbitrary", "arbitrary"),
        name="prompt_dsa",
    )(q, k, v, mask, bias)


def _sb_kernel(q_ref, k_ref, v_ref, o_ref, z_ref, tail_ref, c_ref, acc_ref, *, tq):
    i = pl.program_id(1)
    qs, lo_half = _pair_queries(q_ref, tq)
    c_ref[...] = jnp.zeros(c_ref.shape, F32)
    acc_ref[...] = jnp.zeros(acc_ref.shape, F32)
    row = lax.broadcasted_iota(I32, (tq, tq), 0)
    col = lax.broadcasted_iota(I32, (tq, tq), 1)
    not_before = jnp.where(row >= col, 1.0, 0.0).astype(BF16)
    earlier = col < row

    def block(j, diag):
        base = pl.multiple_of(j * tq, tq)
        for p in range(N_PAIRS):
            kp = k_ref[0, pl.ds(base, tq), p * LANES:(p + 1) * LANES]
            for hh in range(2):
                z_ref[2 * p + hh] = _dot_nt(qs[p][hh], kp)
        for h in range(N_HEADS):
            z = z_ref[h]
            sp = jnp.maximum(z, jnp.log2(1.0 + jnp.exp2(jnp.minimum(z, EXP2_CLAMP))))
            if diag:
                sp = jnp.where(earlier, sp, 0.0)
            tail_ref[h] = _dot(sp.astype(BF16), not_before)
        for p in range(N_PAIRS):
            vp = v_ref[0, pl.ds(base, tq), p * LANES:(p + 1) * LANES]
            pvs = []
            for hh in range(2):
                h = 2 * p + hh
                c_old = c_ref[h]
                chunks = []
                for n in range(tq // LANES):
                    sl = slice(n * LANES, (n + 1) * LANES)
                    a = jnp.exp2(z_ref[h, :, sl] - tail_ref[h, :, sl] - c_old)
                    if diag:
                        a = jnp.where(earlier[:, sl], a, 0.0)
                    chunks.append(a.astype(BF16))
                c_ref[h] = c_old + tail_ref[h, :, :1]
                pvs.append(_dot(jnp.concatenate(chunks, axis=1), vp))
            sl = slice(p * LANES, (p + 1) * LANES)
            acc_ref[:, sl] = acc_ref[:, sl] + jnp.where(lo_half, pvs[0], pvs[1])

    block(i, True)

    def body(step, _):
        block(i - 1 - step, False)
        return 0

    lax.fori_loop(0, i, body, 0)
    o_ref[0] = acc_ref[...]


def _prompt_sb(q, k, v, tq):
    b, t, dh = q.shape
    seq = pl.BlockSpec((1, t, dh), lambda bi, i: (bi, 0, 0), pipeline_mode=pl.Buffered(1))
    return pl.pallas_call(
        functools.partial(_sb_kernel, tq=tq),
        grid=(b, t // tq),
        in_specs=[pl.BlockSpec((1, tq, dh), lambda bi, i: (bi, i, 0)), seq, seq],
        out_specs=pl.BlockSpec((1, tq, dh), lambda bi, i: (bi, i, 0)),
        out_shape=SDS((b, t, dh), F32),
        scratch_shapes=[pltpu.VMEM((N_HEADS, tq, tq), F32), pltpu.VMEM((N_HEADS, tq, tq), F32),
                        pltpu.VMEM((N_HEADS, tq, LANES), F32),
                        pltpu.VMEM((tq, dh), F32)],
        compiler_params=_params("arbitrary", "arbitrary"),
        name="prompt_sb",
    )(q, k, v)


def _merge_kernel(ya_ref, yb_ref, gate_ref, x_ref, g1_ref, n_ref, wa_ref, wb_ref, wo_ref, o_ref):
    d = x_ref.shape[1]
    ma = _dot(ya_ref[...].astype(BF16), wa_ref[...])
    mb = _dot(yb_ref[...].astype(BF16), wb_ref[...])
    merged = gate_ref[:, :d].astype(F32) * ma + gate_ref[:, d:].astype(F32) * mb
    mix = _dot(merged.astype(BF16), wo_ref[...])
    o_ref[...] = x_ref[...] + g1_ref[0] * _rms(mix, n_ref[...])


def _mod_spec(mod, tm, rows_per_mod):
    d = mod.shape[2]
    if mod.shape[1] == 1:
        return pl.BlockSpec((1, 1, d), lambda i: ((i * tm) // rows_per_mod, 0, 0))
    return pl.BlockSpec((1, tm, d), lambda i: (0, i, 0))


def _merge(ya, yb, gate, x, g1, norm, wa, wb, wo, tm, rows_per_mod):
    m, d = x.shape
    row = lambda a: pl.BlockSpec((tm, a.shape[1]), lambda i: (i, 0))
    full = lambda a: pl.BlockSpec(a.shape, lambda i: (0, 0), pipeline_mode=pl.Buffered(1))
    return pl.pallas_call(
        _merge_kernel,
        grid=(m // tm,),
        in_specs=[row(ya), row(yb), row(gate), row(x), _mod_spec(g1, tm, rows_per_mod),
                  full(norm), full(wa), full(wb), full(wo)],
        out_specs=pl.BlockSpec((tm, d), lambda i: (i, 0)),
        out_shape=SDS((m, d), F32),
        compiler_params=_params("arbitrary"),
        name="merge",
    )(ya, yb, gate, x, g1, norm, wa, wb, wo)


def _ffn_kernel(x_ref, sc_ref, sh_ref, g2_ref, npre_ref, npost_ref, wg_ref, wu_ref, wd_ref, o_ref, *, chunk):
    x = x_ref[...]
    h = (_rms(x, npre_ref[...]) * (1.0 + sc_ref[0]) + sh_ref[0]).astype(BF16)
    f = jnp.zeros(x.shape, F32)
    for c0 in range(0, wg_ref.shape[1], chunk):
        gate = _dot(h, wg_ref[:, c0:c0 + chunk])
        up = _dot(h, wu_ref[:, c0:c0 + chunk])
        act = gate / (1.0 + jnp.exp(-gate)) * up
        f = f + _dot(act.astype(BF16), wd_ref[c0:c0 + chunk, :])
    o_ref[...] = x + g2_ref[0] * _rms(f, npost_ref[...])


def _ffn(x, sc, sh, g2, npre, npost, wg, wu, wd, tm, rows_per_mod):
    m, d = x.shape
    full = lambda a: pl.BlockSpec(a.shape, lambda i: (0, 0), pipeline_mode=pl.Buffered(1))
    mod = lambda a: _mod_spec(a, tm, rows_per_mod)
    return pl.pallas_call(
        functools.partial(_ffn_kernel, chunk=256),
        grid=(m // tm,),
        in_specs=[pl.BlockSpec((tm, d), lambda i: (i, 0)), mod(sc), mod(sh), mod(g2),
                  full(npre), full(npost), full(wg), full(wu), full(wd)],
        out_specs=pl.BlockSpec((tm, d), lambda i: (i, 0)),
        out_shape=SDS((m, d), F32),
        compiler_params=_params("arbitrary"),
        name="ffn",
    )(x, sc, sh, g2, npre, npost, wg, wu, wd)


def _sample_score_kernel(pt_ref, q_ref, w_ref, knew_ref, *rest, pages):
    page_refs = rest[:pages]
    past_ref, new_ref = rest[pages:]
    q_hi, q_lo = _split2(q_ref[0])
    w = w_ref[0]
    t = q_ref.shape[1] // N_IDX_HEADS

    def score(keys):
        k_hi, k_lo = _split2(keys)
        r = jnp.maximum(_dot3(q_hi, q_lo, k_hi, k_lo), 0.0) * w
        s = r[0:t]
        for h in range(1, N_IDX_HEADS):
            s = s + r[h * t:(h + 1) * t]
        return s

    for n in range(pages):
        past_ref[0, :, n * LANES:(n + 1) * LANES] = score(page_refs[n][...])
    new_ref[0] = score(knew_ref[0])


def _sample_scores(page_table, q_rows, w_rows, k_new, cache_k_idx, pages):
    nb, n_pages = page_table.shape
    page = cache_k_idx.shape[2]
    assert page == LANES
    rows = q_rows.shape[1]
    t = rows // N_IDX_HEADS

    def page_spec(n):
        return pl.BlockSpec((None, None, IDX_DIM, page), lambda b, g, pt: (0, pt[b, g * pages + n], 0, 0))

    grid_spec = pltpu.PrefetchScalarGridSpec(
        num_scalar_prefetch=1,
        grid=(nb, n_pages // pages),
        in_specs=[pl.BlockSpec((1, rows, IDX_DIM), lambda b, g, pt: (b, 0, 0)),
                  pl.BlockSpec((1, rows, LANES), lambda b, g, pt: (b, 0, 0)),
                  pl.BlockSpec((1, IDX_DIM, page), lambda b, g, pt: (b, 0, 0))]
                 + [page_spec(n) for n in range(pages)],
        out_specs=[pl.BlockSpec((1, t, pages * page), lambda b, g, pt: (b, 0, g)),
                   pl.BlockSpec((1, t, page), lambda b, g, pt: (b, 0, 0))],
    )
    return pl.pallas_call(
        functools.partial(_sample_score_kernel, pages=pages),
        grid_spec=grid_spec,
        out_shape=[SDS((nb, t, n_pages * page), F32), SDS((nb, t, page), F32)],
        compiler_params=_params("arbitrary", "arbitrary"),
        name="sample_scores",
    )(page_table, q_rows, w_rows, k_new, *([cache_k_idx.transpose(0, 1, 3, 2)] * pages))


def _sample_select_kernel(past_ref, new_ref, mpast_ref, mnew_ref, score_ref, thr_ref, need_ref, *, t, topk):
    rows, past = past_ref.shape
    score_ref[:, :past] = past_ref[...]
    q_t = lax.broadcasted_iota(I32, (rows, LANES), 0) % t
    col = lax.broadcasted_iota(I32, (rows, LANES), 1)
    score_ref[:, past:] = jnp.where(col <= q_t, new_ref[...], -jnp.inf)

    def emit(j, sel):
        mpast_ref[:, pl.ds(pl.multiple_of(j * LANES, LANES), LANES)] = sel

    def emit_new(j, sel):
        mnew_ref[...] = sel

    width = score_ref.shape[1]
    gw = _group_width(width)
    _emit_selection(score_ref, thr_ref, need_ref, width // gw, gw, width // LANES, LANES, topk, emit, emit_new)


def _sample_select(s_past, s_new, t, topk, tr):
    rows, past = s_past.shape
    width = past + LANES
    return pl.pallas_call(
        functools.partial(_sample_select_kernel, t=t, topk=float(topk)),
        grid=(rows // tr,),
        in_specs=[pl.BlockSpec((tr, past), lambda i: (i, 0)), pl.BlockSpec((tr, LANES), lambda i: (i, 0))],
        out_specs=[pl.BlockSpec((tr, past), lambda i: (i, 0)), pl.BlockSpec((tr, LANES), lambda i: (i, 0))],
        out_shape=[SDS((rows, past), F32), SDS((rows, LANES), F32)],
        scratch_shapes=[pltpu.VMEM((tr, width), F32), pltpu.VMEM((tr, 1), F32), pltpu.VMEM((tr, 1), F32)],
        compiler_params=_params("arbitrary"),
        name="sample_select",
    )(s_past, s_new)


def _sample_bias_kernel(rb_ref, o_ref, *, t, page):
    rows = N_HEADS * t
    row = lax.broadcasted_iota(I32, (rows, page), 0)
    key = lax.broadcasted_iota(I32, (rows, page), 1)
    q_t, head = row % t, row // t
    for tile in range(3):
        bucket = _t5_bucket(q_t - key + tile * page)
        val = jnp.zeros((rows, page), F32)
        for h in range(N_HEADS):
            val = jnp.where(head == h, _bias_lookup(rb_ref, bucket, h), val)
        o_ref[tile] = val


def _sample_bias(rel_bias, t, page):
    assert page >= MAX_DISTANCE
    return pl.pallas_call(
        functools.partial(_sample_bias_kernel, t=t, page=page),
        in_specs=[pl.BlockSpec(memory_space=pltpu.SMEM)],
        out_shape=SDS((3, N_HEADS * t, page), F32),
        name="sample_bias",
    )(rel_bias)


def _sample_attn_kernel(pt_ref, qa_ref, qb_ref, knew_a_ref, vnew_a_ref, knew_b_ref, vnew_b_ref,
                        fnew_ref, flags_ref, bias_ref, *rest, pages, t):
    ka_refs = rest[0 * pages:1 * pages]
    va_refs = rest[1 * pages:2 * pages]
    kb_refs = rest[2 * pages:3 * pages]
    vb_refs = rest[3 * pages:4 * pages]
    ya_ref, yb_ref, m_ref, l_ref, acca_ref, c_ref, accb_ref = rest[4 * pages:]
    g = pl.program_id(1)
    n_groups = pl.num_programs(1)
    rows = N_HEADS * t
    qa = qa_ref[0]
    qb = qb_ref[0]
    srow = lax.broadcasted_iota(I32, (LANES, LANES), 0)
    scol = lax.broadcasted_iota(I32, (LANES, LANES), 1)
    not_before = jnp.where(srow >= scol, 1.0, 0.0).astype(BF16)
    not_before = jnp.concatenate([not_before, not_before], axis=0)

    def scores(q, k):
        return _dot(q, k.reshape(N_HEADS * HEAD_DIM, LANES).astype(BF16))

    def weighted(p, v):
        return _dot_nt(p.astype(BF16), v.reshape(N_HEADS * HEAD_DIM, LANES).astype(BF16))

    def dsa_update(blocks):
        logits = [scores(qa, k) + bias + (1.0 - jnp.concatenate([flags] * N_HEADS, axis=0)) * NEG
                  for k, _, bias, flags in blocks]
        m_old = m_ref[...]
        m_new = m_old
        for lg in logits:
            m_new = jnp.maximum(m_new, jnp.max(lg, axis=1, keepdims=True))
        alpha = jnp.exp(m_old - m_new)
        l_new = alpha * l_ref[...]
        contrib = jnp.zeros(acca_ref.shape, F32)
        for lg, (_, v, _, _) in zip(logits, blocks):
            pexp = jnp.exp(lg - m_new)
            l_new = l_new + jnp.sum(pexp, axis=1, keepdims=True)
            contrib = contrib + weighted(pexp, v)
        m_ref[...] = m_new
        l_ref[...] = l_new
        acca_ref[...] = alpha * acca_ref[...] + contrib

    def sb_update(blocks):
        zs = [scores(qb, k) for k, _, _ in blocks]
        tails = []
        for z, (_, _, earlier) in zip(zs, blocks):
            sp = jnp.maximum(z, 0.0) + jnp.log(1.0 + jnp.exp(-jnp.abs(z)))
            if earlier is not None:
                sp = jnp.where(earlier, sp, 0.0)
            hi, lo = _split2(sp)
            tails.append(_dot(jnp.concatenate([hi, lo], axis=1), not_before))
        carry = c_ref[...]
        contrib = jnp.zeros(accb_ref.shape, F32)
        for z, tail, (_, v, earlier) in zip(zs, tails, blocks):
            a = jnp.exp(z - tail - carry)
            if earlier is not None:
                a = jnp.where(earlier, a, 0.0)
            carry = carry + tail[:, :1]
            contrib = contrib + weighted(a, v)
        c_ref[...] = carry
        accb_ref[...] = accb_ref[...] + contrib

    @pl.when(g == 0)
    def _():
        m_ref[...] = jnp.full(m_ref.shape, NEG, F32)
        l_ref[...] = jnp.zeros(l_ref.shape, F32)
        acca_ref[...] = jnp.zeros(acca_ref.shape, F32)
        c_ref[...] = jnp.zeros(c_ref.shape, F32)
        accb_ref[...] = jnp.zeros(accb_ref.shape, F32)
        key = lax.broadcasted_iota(I32, (rows, LANES), 1)
        q_t = lax.broadcasted_iota(I32, (rows, LANES), 0) % t
        dsa_update([(knew_a_ref[0], vnew_a_ref[0], bias_ref[0], fnew_ref[0])])
        sb_update([(knew_b_ref[0], vnew_b_ref[0], key < q_t)])

    blocks = []
    for n in range(pages):
        flags = flags_ref[0, :, (pages - 1 - n) * LANES:(pages - n) * LANES]
        bias = bias_ref[2] if n else jnp.where(g == 0, bias_ref[1], bias_ref[2])
        blocks.append((ka_refs[n][...], va_refs[n][...], bias, flags))
    dsa_update(blocks)
    sb_update([(kb_refs[n][...], vb_refs[n][...], None) for n in range(pages)])

    @pl.when(g == n_groups - 1)
    def _():
        ya_ref[0] = acca_ref[...] / l_ref[...]
        yb_ref[0] = accb_ref[...]


def _sample_attn(page_table, qa_rows, qb_rows, knew_a, vnew_a, knew_b, vnew_b, flags_new, flags_past, bias,
                 cache_k_a, cache_v_a, cache_k_b, cache_v_b, pages, t):
    nb, n_pages = page_table.shape
    rows = N_HEADS * t
    n_groups = n_pages // pages
    dh = N_HEADS * HEAD_DIM
    page_shape = (N_HEADS, HEAD_DIM, LANES)

    def page_spec(n):
        return pl.BlockSpec((None, None) + page_shape,
                            lambda b, g, pt: (0, pt[b, n_pages - 1 - (g * pages + n)], 0, 0, 0))

    per_seq = lambda shape: pl.BlockSpec((1,) + shape, lambda b, g, pt: (b,) + (0,) * len(shape))
    grid_spec = pltpu.PrefetchScalarGridSpec(
        num_scalar_prefetch=1,
        grid=(nb, n_groups),
        in_specs=[per_seq((rows, dh)), per_seq((rows, dh)),
                  per_seq(page_shape), per_seq(page_shape), per_seq(page_shape), per_seq(page_shape),
                  per_seq((t, LANES)),
                  pl.BlockSpec((1, t, pages * LANES), lambda b, g, pt: (b, 0, n_groups - 1 - g)),
                  pl.BlockSpec(bias.shape, lambda b, g, pt: (0, 0, 0))]
                 + [page_spec(n) for n in range(pages)] * 4,
        out_specs=[per_seq((rows, dh)), per_seq((rows, dh))],
        scratch_shapes=[pltpu.VMEM((rows, 1), F32), pltpu.VMEM((rows, 1), F32), pltpu.VMEM((rows, dh), F32),
                        pltpu.VMEM((rows, 1), F32), pltpu.VMEM((rows, dh), F32)],
    )
    key_minor = lambda c: c.transpose(0, 1, 3, 4, 2)
    return pl.pallas_call(
        functools.partial(_sample_attn_kernel, pages=pages, t=t),
        grid_spec=grid_spec,
        out_shape=[SDS((nb, rows, dh), F32), SDS((nb, rows, dh), F32)],
        compiler_params=_params("arbitrary", "arbitrary"),
        name="sample_attn",
    )(page_table, qa_rows, qb_rows, knew_a, vnew_a, knew_b, vnew_b, flags_new, flags_past, bias,
      *([key_minor(cache_k_a)] * pages), *([key_minor(cache_v_a)] * pages),
      *([key_minor(cache_k_b)] * pages), *([key_minor(cache_v_b)] * pages))


def _largest_divisor(n, cap):
    d = min(n, cap)
    while n % d:
        d -= 1
    return d


def kernel(x_prompt, x_sample, c_prompt, c_sample, cache_k_a, cache_v_a, cache_k_idx, cache_k_b, cache_v_b,
           page_table, rel_bias, w_ada, b_ada, norm_pre_mix, norm_post_mix, norm_pre_ffn, norm_post_ffn,
           w_in, w_br_a, w_br_b, w_out, w_ffn_gate, w_ffn_up, w_ffn_down):
    depth = w_in.shape[0]
    assert depth == 1
    nb_p, t_p, d = x_prompt.shape
    nb_s, t_s, _ = x_sample.shape
    n_pages = page_table.shape[1]
    page = cache_k_a.shape[2]
    past = n_pages * page
    dh = N_HEADS * HEAD_DIM
    nq = N_IDX_HEADS * IDX_DIM
    topk_p = min(INDEX_TOPK, t_p // 4)
    topk_s = min(INDEX_TOPK, (past + t_s) // 4)
    tq = 256
    assert t_p % tq == 0 and topk_p <= tq and t_s * N_HEADS <= LANES and page == LANES

    w = w_in[0]
    o = np.cumsum([0, dh, dh, dh, nq, IDX_DIM, N_IDX_HEADS, dh, dh, dh, d, d])
    col = lambda n: w[:, o[n]:o[n + 1]]
    w_main = jnp.concatenate([col(0), col(1), col(2), col(6), col(7), col(8)], axis=1).astype(BF16)
    w_gate = jnp.concatenate([col(9), col(10)], axis=1).astype(BF16)
    w_idx = jnp.concatenate([col(3), col(4), col(4), col(5),
                             jnp.zeros((d, LANES - N_IDX_HEADS), F32)], axis=1)
    w_idx_hi = w_idx.astype(BF16)
    w_idx_lo = (w_idx - w_idx_hi.astype(F32)).astype(BF16)
    wa, wb, wo = w_br_a[0].astype(BF16), w_br_b[0].astype(BF16), w_out[0].astype(BF16)
    wg, wu, wd = w_ffn_gate[0].astype(BF16), w_ffn_up[0].astype(BF16), w_ffn_down[0].astype(BF16)
    n_pre_mix, n_post_mix = norm_pre_mix[0][None], norm_post_mix[0][None]
    n_pre_ffn, n_post_ffn = norm_pre_ffn[0][None], norm_post_ffn[0][None]

    n_c = nb_p + nb_s
    c_all = jnp.concatenate([c_prompt, c_sample, jnp.zeros((-n_c % 8, d), F32)], axis=0)
    ada = _ada(c_all, w_ada[0], b_ada[0][None])
    mods_p = [m[:, None, :] for m in jnp.split(ada[:nb_p], 6, axis=-1)]
    mods_s = [jnp.repeat(m, t_s, axis=0)[None] for m in jnp.split(ada[nb_p:n_c], 6, axis=-1)]

    def dense_front(x2, mods, tm, rows_per_mod, seq_len, q_scale):
        return _inproj(x2, mods[1], mods[0], n_pre_mix, w_main, w_gate, w_idx_hi, w_idx_lo, tm, rows_per_mod,
                       seq_len, q_scale)

    def dense_back(ya, yb, gate, x2, mods, tm, rows_per_mod):
        x1 = _merge(ya, yb, gate, x2, mods[2], n_post_mix, wa, wb, wo, tm, rows_per_mod)
        return _ffn(x1, mods[4], mods[3], mods[5], n_pre_ffn, n_post_ffn, wg, wu, wd, tm, rows_per_mod)

    m_p = nb_p * t_p
    xp2 = x_prompt.reshape(m_p, d)
    tm_p = 512
    (qa, ka, va, qb, kb, vb, kab, vab, kbb, vbb, gate, iq, ik, wi, kidx, wit) = dense_front(
        xp2, mods_p, tm_p, t_p, t_p, LOG2E * HEAD_DIM ** -0.5)
    seq3 = lambda a: a.reshape(nb_p, t_p, a.shape[-1])
    mask = _prompt_index(seq3(iq), wit, seq3(ik), tq, topk_p)
    ya = _prompt_dsa(seq3(qa), seq3(kab), seq3(vab), mask, _bias_tiles(rel_bias, tq), tq)
    yb = _prompt_sb(seq3(qb), seq3(kbb), seq3(vbb), tq)
    y_p = dense_back(ya.reshape(m_p, dh), yb.reshape(m_p, dh), gate, xp2, mods_p, tm_p, t_p)
    heads_p = lambda a: a.reshape(nb_p, N_HEADS, HEAD_DIM, t_p).transpose(0, 3, 1, 2)[None]
    outs_p = (heads_p(ka), heads_p(va), kidx.transpose(0, 2, 1)[None], heads_p(kb), heads_p(vb))

    m_s = nb_s * t_s
    xs2 = x_sample.reshape(m_s, d)
    tm_s = _largest_divisor(m_s, 256)
    (qa, ka, va, qb, kb, vb, _, _, _, _, gate, iq, ik, wi, kidx, _) = dense_front(
        xs2, mods_s, tm_s, t_s, m_s, HEAD_DIM ** -0.5)
    q_rows = iq.reshape(nb_s, t_s, N_IDX_HEADS, IDX_DIM).transpose(0, 2, 1, 3).reshape(nb_s, N_IDX_HEADS * t_s, IDX_DIM)
    w_rows = wi.reshape(nb_s, t_s, LANES)[:, :, :N_IDX_HEADS].transpose(0, 2, 1).reshape(nb_s, N_IDX_HEADS * t_s, 1)
    w_rows = jnp.broadcast_to(w_rows, (nb_s, N_IDX_HEADS * t_s, LANES))
    pages_idx = _largest_divisor(n_pages, 16)
    k_new = jnp.pad(kidx[0].reshape(IDX_DIM, nb_s, t_s).transpose(1, 0, 2), ((0, 0), (0, 0), (0, page - t_s)))
    s_past, s_new = _sample_scores(page_table, q_rows, w_rows, k_new, cache_k_idx, pages_idx)
    sel_past, sel_new = _sample_select(s_past.reshape(m_s, past), s_new.reshape(m_s, page), t_s, topk_s,
                                       _largest_divisor(m_s, 128))
    head_rows = lambda a: (jnp.einsum('bthd,hg->bhtgd', a.reshape(nb_s, t_s, N_HEADS, HEAD_DIM).astype(F32),
                                      jnp.eye(N_HEADS, dtype=F32))
                           .reshape(nb_s, N_HEADS * t_s, dh).astype(BF16))
    as_page = lambda a: jnp.pad(a[0].reshape(N_HEADS, HEAD_DIM, nb_s, t_s).transpose(2, 0, 1, 3),
                                ((0, 0), (0, 0), (0, 0), (0, page - t_s)))
    pages_attn = _largest_divisor(n_pages, 16)
    ya, yb = _sample_attn(page_table, head_rows(qa), head_rows(qb),
                          as_page(ka), as_page(va), as_page(kb), as_page(vb),
                          sel_new.reshape(nb_s, t_s, page), sel_past.reshape(nb_s, t_s, past),
                          _sample_bias(rel_bias, t_s, page),
                          cache_k_a, cache_v_a, cache_k_b, cache_v_b, pages_attn, t_s)
    token_rows = lambda y: (jnp.diagonal(y.reshape(nb_s, N_HEADS, t_s, N_HEADS, HEAD_DIM), axis1=1, axis2=3)
                            .transpose(0, 1, 3, 2).reshape(m_s, dh))
    y_s = dense_back(token_rows(ya), token_rows(yb), gate, xs2, mods_s, tm_s, t_s)
    heads_s = lambda a: a[0].T.reshape(depth, nb_s, t_s, N_HEADS, HEAD_DIM)
    outs_s = (heads_s(ka), heads_s(va), kidx[0].T.reshape(depth, nb_s, t_s, IDX_DIM), heads_s(kb), heads_s(vb))

    return (y_p.reshape(nb_p, t_p, d), y_s.reshape(nb_s, t_s, d)) + outs_p + outs_s
```

```python
import functools
import math

import numpy as np
import jax
import jax.numpy as jnp
from jax import lax
from jax.experimental import pallas as pl
from jax.experimental.pallas import tpu as pltpu

F32 = jnp.float32
BF16 = jnp.bfloat16
I32 = jnp.int32
SDS = jax.ShapeDtypeStruct

HEAD_DIM = 64
N_HEADS = 8
N_PAIRS = N_HEADS // 2
N_IDX_HEADS = 4
IDX_DIM = 64
INDEX_TOPK = 256
N_BUCKETS = 32
MAX_EXACT = N_BUCKETS // 2
MAX_DISTANCE = 128
EPS = 1e-6
LANES = 128
SUBLANES = 8
NEG = -1e30
LOG2E = 1.4426950408889634
EXP2_CLAMP = 126.0
VMEM_LIMIT = 56 * 1024 * 1024

_KEY_NEG_INF = np.int32(0x807FFFFF - 2 ** 32)


def _params(*sem):
    return pltpu.CompilerParams(dimension_semantics=sem, vmem_limit_bytes=VMEM_LIMIT)


def _dot(a, b):
    return jnp.dot(a, b, preferred_element_type=F32)


def _dot_nt(a, b):
    return lax.dot_general(a, b, (((1,), (1,)), ((), ())), preferred_element_type=F32)


def _split2(x):
    hi = x.astype(BF16)
    lo = (x - hi.astype(F32)).astype(BF16)
    return hi, lo


def _dot3(a_hi, a_lo, b_hi, b_lo):
    return _dot(a_hi, b_hi) + _dot(a_lo, b_hi) + _dot(a_hi, b_lo)


def _rms(x, g):
    var = jnp.mean(x * x, axis=-1, keepdims=True)
    return x * lax.rsqrt(var + EPS) * g


def _ada_kernel(c_ref, w_ref, b_ref, o_ref):
    c_hi, c_lo = _split2(c_ref[...])
    w_hi, w_lo = _split2(w_ref[...])
    o_ref[...] = _dot3(c_hi, c_lo, w_hi, w_lo) + b_ref[...]


def _ada(c, w, b):
    rows, d = c.shape
    n = w.shape[1]
    tn = 1024
    return pl.pallas_call(
        _ada_kernel,
        grid=(n // tn,),
        in_specs=[pl.BlockSpec((rows, d), lambda j: (0, 0)),
                  pl.BlockSpec((d, tn), lambda j: (0, j)),
                  pl.BlockSpec((1, tn), lambda j: (0, j))],
        out_specs=pl.BlockSpec((rows, tn), lambda j: (0, j)),
        out_shape=SDS((rows, n), F32),
        compiler_params=_params("arbitrary"),
        name="ada",
    )(c, w, b)


def _inproj_kernel(x_ref, sc_ref, sh_ref, g_ref, wm_ref, wg_ref, wih_ref, wil_ref,
                   qa_ref, ka_ref, va_ref, qb_ref, kb_ref, vb_ref,
                   kab_ref, vab_ref, kbb_ref, vbb_ref,
                   gate_ref, iq_ref, ik_ref, wi_ref, kidx_ref, wit_ref, *, q_scale):
    dh = N_HEADS * HEAD_DIM
    h = _rms(x_ref[...], g_ref[...]) * (1.0 + sc_ref[0]) + sh_ref[0]
    h_hi, h_lo = _split2(h)

    def seg(n):
        return _dot(h_hi, wm_ref[:, n * dh:(n + 1) * dh])

    qa_ref[...] = (seg(0) * q_scale).astype(BF16)
    ka = seg(1)
    ka_ref[0] = ka.T
    kab_ref[...] = ka.astype(BF16)
    va = seg(2)
    va_ref[0] = va.T
    vab_ref[...] = va.astype(BF16)
    qb_ref[...] = (seg(3) * q_scale).astype(BF16)
    kb = seg(4)
    kb_ref[0] = kb.T
    kbb_ref[...] = kb.astype(BF16)
    vb = seg(5)
    vb_ref[0] = vb.T
    vbb_ref[...] = vb.astype(BF16)

    d = x_ref.shape[1]
    for n in range(2):
        g = _dot(h_hi, wg_ref[:, n * d:(n + 1) * d])
        gate_ref[:, n * d:(n + 1) * d] = (1.0 / (1.0 + jnp.exp(-g))).astype(BF16)

    nq = N_IDX_HEADS * IDX_DIM
    idx = _dot3(h_hi, h_lo, wih_ref[...], wil_ref[...])
    iq_ref[...] = idx[:, :nq] * IDX_DIM ** -0.5
    ik_ref[...] = idx[:, nq:nq + LANES]
    kidx_ref[0] = idx[:, nq:nq + LANES].T[:IDX_DIM, :]
    wi = idx[:, nq + LANES:nq + 2 * LANES] * N_IDX_HEADS ** -0.5
    wi_ref[...] = wi
    wit_ref[...] = wi.T[:SUBLANES, :]


def _inproj(x, sc, sh, g, wm, wg, wih, wil, tm, rows_per_mod, seq_len, q_scale):
    m, d = x.shape
    dh = N_HEADS * HEAD_DIM
    r = sc.shape[1]
    if r == 1:
        mod_map = lambda i: ((i * tm) // rows_per_mod, 0, 0)
    else:
        mod_map = lambda i: (0, i, 0)
    row = lambda w: pl.BlockSpec((tm, w), lambda i: (i, 0))
    full = lambda a: pl.BlockSpec(a.shape, lambda i: (0, 0), pipeline_mode=pl.Buffered(1))
    nq = N_IDX_HEADS * IDX_DIM
    outs = [(dh, BF16), (dh, None), (dh, None), (dh, BF16), (dh, None), (dh, None),
            (dh, BF16), (dh, BF16), (dh, BF16), (dh, BF16),
            (2 * d, BF16), (nq, F32), (LANES, F32), (LANES, F32), (IDX_DIM, None)]
    tiles_per_seq = seq_len // tm
    token_minor = lambda w: pl.BlockSpec((1, w, tm), lambda i: (i // tiles_per_seq, 0, i % tiles_per_seq))
    return pl.pallas_call(
        functools.partial(_inproj_kernel, q_scale=q_scale),
        grid=(m // tm,),
        in_specs=[row(d), pl.BlockSpec((1, r, d), mod_map), pl.BlockSpec((1, r, d), mod_map),
                  full(g), full(wm), full(wg), full(wih), full(wil)],
        out_specs=[row(w) if dt else token_minor(w) for w, dt in outs]
                  + [pl.BlockSpec((SUBLANES, tm), lambda i: (0, i))],
        out_shape=[SDS((m, w), dt) if dt else SDS((m // seq_len, w, seq_len), F32) for w, dt in outs]
                  + [SDS((SUBLANES, m), F32)],
        compiler_params=_params("arbitrary"),
        name="inproj",
    )(x, sc, sh, g, wm, wg, wih, wil)


ROW_CHUNK = 64
MAX_GROUPS = 256


def _key_to_float(key):
    bits = jnp.where(key < 0, key ^ jnp.int32(0x7FFFFFFF), key)
    f = lax.bitcast_convert_type(bits, F32)
    return jnp.where(key < _KEY_NEG_INF, -jnp.inf, f)


def _float_to_key(f):
    bits = lax.bitcast_convert_type(f, I32)
    return jnp.where(bits < 0, bits ^ jnp.int32(0x7FFFFFFF), bits)


def _group_width(width):
    gw = min(width, 8 * LANES)
    while width % gw:
        gw -= LANES
    return gw


def _count(score_ref, r0, ngroups, gw, cf, strict):
    cfb = jnp.broadcast_to(cf, (ROW_CHUNK, LANES))

    def body(g, acc):
        base = pl.multiple_of(g * gw, gw)
        for c in range(gw // LANES):
            s = score_ref[pl.ds(r0, ROW_CHUNK), pl.ds(base + c * LANES, LANES)]
            hit = (s > cfb) if strict else (s >= cfb)
            acc = acc + jnp.where(hit, 1.0, 0.0)
        return acc

    acc = lax.fori_loop(0, ngroups, body, jnp.zeros((ROW_CHUNK, LANES), F32))
    return jnp.sum(acc, axis=1, keepdims=True)


def _kth_largest(score_ref, r0, ngroups, gw, k):
    ninf = jnp.full((ROW_CHUNK, LANES), -jnp.inf, F32)

    def bounds(g, carry):
        base = pl.multiple_of(g * gw, gw)
        tops = list(carry)
        for c in range(gw // LANES):
            s = score_ref[pl.ds(r0, ROW_CHUNK), pl.ds(base + c * LANES, LANES)]
            tops[c % 2] = jnp.maximum(tops[c % 2], s)
        return tuple(tops)

    top_even, top_odd = lax.fori_loop(0, ngroups, bounds, (ninf, ninf))
    upper = jnp.max(jnp.maximum(top_even, top_odd), axis=1, keepdims=True)
    lower = jnp.min(jnp.minimum(top_even, top_odd), axis=1, keepdims=True)
    lo0 = _float_to_key(lower)
    hi0 = _float_to_key(upper) + 1

    def n_active(lo, hi):
        return jnp.sum(jnp.where(hi > lo + 1, 1.0, 0.0))

    def cond(state):
        return state[2] > 0.0

    def step(state):
        lo, hi, _ = state
        mid = lax.shift_right_arithmetic(lo, 1) + lax.shift_right_arithmetic(hi, 1) + (lo & hi & 1)
        cnt = _count(score_ref, r0, ngroups, gw, _key_to_float(mid), strict=False)
        keep = cnt >= k
        lo = jnp.where(keep, mid, lo)
        hi = jnp.where(cnt == k, mid + 1, jnp.where(keep, hi, mid))
        return lo, hi, n_active(lo, hi)

    lo, _, _ = lax.while_loop(cond, step, (lo0, hi0, n_active(lo0, hi0)))
    return _key_to_float(lo)


def _emit_selection(score_ref, thr_ref, need_ref, ngroups, gw, nblk, bw, k, emit, emit_last=None):
    rows = score_ref.shape[0]
    assert k <= MAX_GROUPS and rows % ROW_CHUNK == 0

    def chunk(n, _):
        r0 = pl.multiple_of(n * ROW_CHUNK, ROW_CHUNK)
        thr = _kth_largest(score_ref, r0, ngroups, gw, k)
        thr_ref[pl.ds(r0, ROW_CHUNK), :] = thr
        need_ref[pl.ds(r0, ROW_CHUNK), :] = k - _count(score_ref, r0, ngroups, gw, thr, strict=True)
        return 0

    lax.fori_loop(0, rows // ROW_CHUNK, chunk, 0)
    thr = thr_ref[...]
    need = need_ref[...]
    finite = jnp.where(thr > -jnp.inf, 1.0, 0.0)
    before = (lax.broadcasted_iota(I32, (bw, bw), 0) < lax.broadcasted_iota(I32, (bw, bw), 1))
    before = jnp.where(before, 1.0, 0.0).astype(BF16)

    def body(j, carry, emit=emit):
        s = score_ref[:, pl.ds(pl.multiple_of(j * bw, bw), bw)]
        tie = jnp.where(s == thr, finite, 0.0)
        rank = _dot(tie.astype(BF16), before) + carry
        sel = jnp.where(s > thr, 1.0, jnp.where(rank < need, tie, 0.0))
        emit(j, sel)
        return carry + jnp.sum(tie, axis=1, keepdims=True)

    n_loop = nblk if emit_last is None else nblk - 1
    carry = lax.fori_loop(0, n_loop, body, jnp.zeros((rows, 1), F32))
    if emit_last is not None:
        body(nblk - 1, carry, emit=emit_last)


def _half_masks(shape):
    lane = lax.broadcasted_iota(I32, shape, 1)
    return lane < HEAD_DIM


def _index_kernel(iq_ref, wit_ref, ik_ref, mask_ref, score_ref, top_ref, *, tq, topk):
    i = pl.program_id(1)
    nblk_total = mask_ref.shape[2] // tq
    assert topk <= tq
    lo_half = _half_masks((tq, LANES))
    q_parts = []
    for h in range(N_IDX_HEADS):
        qp = iq_ref[0, :, (h // 2) * LANES:(h // 2 + 1) * LANES]
        swapped = pltpu.roll(qp, HEAD_DIM, axis=1)
        qh = jnp.where(lo_half, qp, swapped) if h % 2 == 0 else jnp.where(lo_half, swapped, qp)
        q_hi = qh.astype(BF16)
        q_lo = jnp.where(lo_half, qh - q_hi.astype(F32), 0.0).astype(BF16)
        q_parts.append(jnp.concatenate([q_hi, q_lo], axis=1))
    w_rows = [wit_ref[h:h + 1, :] for h in range(N_IDX_HEADS)]
    key = lax.broadcasted_iota(I32, (tq, tq), 0)
    qry = lax.broadcasted_iota(I32, (tq, tq), 1)
    top_ref[...] = jnp.full((tq, tq), -jnp.inf, F32)

    def score_block(j, _):
        base = pl.multiple_of(j * tq, tq)
        kk = ik_ref[0, pl.ds(base, tq), :]
        k_hi = kk.astype(BF16)
        k_hi_f32 = k_hi.astype(F32)
        k_parts = jnp.concatenate([jnp.where(lo_half, k_hi_f32, kk - k_hi_f32).astype(BF16), k_hi], axis=1)
        s = jnp.zeros((tq, tq), F32)
        for h in range(N_IDX_HEADS):
            d = _dot_nt(k_parts, q_parts[h])
            s = s + w_rows[h] * jnp.maximum(d, 0.0)
        s = jnp.where(key + (j - i) * tq <= qry, s, -jnp.inf)
        score_ref[pl.ds(base, tq), :] = s
        return s

    def score_pair(jj, _):
        top_ref[...] = jnp.maximum(top_ref[...], jnp.maximum(score_block(2 * jj, 0), score_block(2 * jj + 1, 0)))
        return 0

    lax.fori_loop(0, (i + 1) // 2, score_pair, 0)

    @pl.when((i + 1) % 2 == 1)
    def _():
        top_ref[...] = jnp.maximum(top_ref[...], score_block(i, 0))

    def count(cf, strict):
        cfb = jnp.broadcast_to(cf, (SUBLANES, tq))

        def body(j, accs):
            blk = score_ref[pl.ds(pl.multiple_of(j * tq, tq), tq), :]
            accs = list(accs)
            for r in range(tq // SUBLANES):
                s = blk[r * SUBLANES:(r + 1) * SUBLANES]
                hit = (s > cfb) if strict else (s >= cfb)
                accs[r % len(accs)] = accs[r % len(accs)] + jnp.where(hit, 1.0, 0.0)
            return tuple(accs)

        zero = jnp.zeros((SUBLANES, tq), F32)
        accs = lax.fori_loop(0, i + 1, body, (zero, zero, zero, zero))
        return jnp.sum((accs[0] + accs[1]) + (accs[2] + accs[3]), axis=0, keepdims=True)

    tops = top_ref[...]
    lo0 = _float_to_key(jnp.min(tops, axis=0, keepdims=True))
    hi0 = _float_to_key(jnp.max(tops, axis=0, keepdims=True)) + 1

    def n_active(lo, hi):
        return jnp.sum(jnp.where(hi > lo + 1, 1.0, 0.0))

    def step(state):
        lo, hi, _ = state
        mid = lax.shift_right_arithmetic(lo, 1) + lax.shift_right_arithmetic(hi, 1) + (lo & hi & 1)
        cnt = count(_key_to_float(mid), strict=False)
        keep = cnt >= topk
        lo = jnp.where(keep, mid, lo)
        hi = jnp.where(cnt == topk, mid + 1, jnp.where(keep, hi, mid))
        return lo, hi, n_active(lo, hi)

    at_least_zero = count(jnp.zeros((1, tq), F32), strict=False)
    above_zero = count(jnp.zeros((1, tq), F32), strict=True)
    zero_kth = jnp.logical_and(above_zero < topk, at_least_zero >= topk)
    lo0 = jnp.where(zero_kth, 0, jnp.where(above_zero >= topk, jnp.maximum(lo0, 1), lo0))
    hi0 = jnp.where(zero_kth, 1, jnp.where(at_least_zero < topk, jnp.minimum(hi0, 0), hi0))

    lo, _, _ = lax.while_loop(lambda state: state[2] > 0.0, step, (lo0, hi0, n_active(lo0, hi0)))
    thr = _key_to_float(lo)
    need = topk - count(thr, strict=True)
    finite = jnp.where(thr > -jnp.inf, 1.0, 0.0)
    before = jnp.where(qry < key, 1.0, 0.0).astype(BF16)
    eye = jnp.where(qry == key, 1.0, 0.0).astype(BF16)

    def emit(j, carry):
        base = pl.multiple_of(j * tq, tq)
        s = score_ref[pl.ds(base, tq), :]
        tie = jnp.where(s == thr, finite, 0.0)
        rank = _dot(before, tie.astype(BF16)) + carry
        sel = jnp.where(s > thr, 1.0, jnp.where(rank < need, tie, 0.0))
        mask_ref[0, :, pl.ds(base, tq)] = _dot_nt(eye, sel.astype(BF16)).astype(jnp.int8)
        return carry + jnp.sum(tie, axis=0, keepdims=True)

    carry = lax.fori_loop(0, (i + 1) // 2, lambda jj, c: emit(2 * jj + 1, emit(2 * jj, c)), jnp.zeros((1, tq), F32))

    @pl.when((i + 1) % 2 == 1)
    def _():
        emit(i, carry)

    def clear(j, _):
        mask_ref[0, :, pl.ds(pl.multiple_of(j * tq, tq), tq)] = jnp.zeros((tq, tq), jnp.int8)
        return 0

    lax.fori_loop(i + 1, nblk_total, clear, 0)


def _prompt_index(iq, wit, ik, tq, topk):
    b, t, _ = iq.shape
    nq = t // tq
    return pl.pallas_call(
        functools.partial(_index_kernel, tq=tq, topk=float(topk)),
        grid=(b, nq),
        in_specs=[pl.BlockSpec((1, tq, iq.shape[2]), lambda bi, i: (bi, i, 0)),
                  pl.BlockSpec((SUBLANES, tq), lambda bi, i: (0, bi * nq + i)),
                  pl.BlockSpec((1, t, LANES), lambda bi, i: (bi, 0, 0))],
        out_specs=pl.BlockSpec((1, tq, t), lambda bi, i: (bi, i, 0)),
        out_shape=SDS((b, t, t), jnp.int8),
        scratch_shapes=[pltpu.VMEM((t, tq), F32), pltpu.VMEM((tq, tq), F32)],
        compiler_params=_params("arbitrary", "arbitrary"),
        name="prompt_index",
    )(iq, wit, ik)


def _t5_bucket(rel):
    rel = jnp.maximum(rel, 0)
    lf = (jnp.log(jnp.maximum(rel, 1).astype(F32) / MAX_EXACT)
          / math.log(MAX_DISTANCE / MAX_EXACT) * (N_BUCKETS - MAX_EXACT))
    large = jnp.minimum(MAX_EXACT + lf.astype(I32), N_BUCKETS - 1)
    return jnp.where(rel < MAX_EXACT, rel, large)


def _bias_lookup(rb_ref, bucket, h):
    val = jnp.zeros(bucket.shape, F32)
    for b in range(N_BUCKETS):
        val = jnp.where(bucket == b, rb_ref[b, h], val)
    return val


def _bias_tiles_kernel(rb_ref, o_ref, *, tq):
    h = pl.program_id(0)
    row = lax.broadcasted_iota(I32, (tq, tq), 0)
    col = lax.broadcasted_iota(I32, (tq, tq), 1)
    far = rb_ref[N_BUCKETS - 1, h]
    for tile in range(2):
        o_ref[tile, 0] = (_bias_lookup(rb_ref, _t5_bucket(row - col + tile * tq), h) - far) * LOG2E
    o_ref[2, 0] = jnp.zeros((tq, tq), F32)


def _bias_tiles(rel_bias, tq):
    assert tq >= MAX_DISTANCE
    return pl.pallas_call(
        functools.partial(_bias_tiles_kernel, tq=tq),
        grid=(N_HEADS,),
        in_specs=[pl.BlockSpec(memory_space=pltpu.SMEM)],
        out_specs=pl.BlockSpec((3, 1, tq, tq), lambda h: (0, h, 0, 0)),
        out_shape=SDS((3, N_HEADS, tq, tq), F32),
        compiler_params=_params("arbitrary"),
        name="bias_tiles",
    )(rel_bias)


def _pair_queries(q_ref, tq):
    lo_half = _half_masks((tq, LANES))
    out = []
    for p in range(N_PAIRS):
        qp = q_ref[0, :, p * LANES:(p + 1) * LANES].astype(F32)
        out.append((jnp.where(lo_half, qp, 0.0).astype(BF16), jnp.where(lo_half, 0.0, qp).astype(BF16)))
    return out, lo_half


def _dsa_kernel(q_ref, k_ref, v_ref, mask_ref, bias_ref, o_ref, logit_ref, madd_ref, m_ref, acc_ref, *, tq):
    i = pl.program_id(1)
    tk = 2 * tq
    qs, lo_half = _pair_queries(q_ref, tq)
    m_ref[...] = jnp.full(m_ref.shape, NEG, F32)
    acc_ref[...] = jnp.zeros(acc_ref.shape, F32)
    keep_lo = jnp.where(_half_masks((tk, LANES)), 1.0, 0.0).astype(BF16)
    keep_hi = 1.0 - keep_lo

    def pair_step(jj, near):
        base = pl.multiple_of(jj * tk, tk)
        madd_ref[...] = jnp.where(mask_ref[0, :, pl.ds(base, tk)].astype(I32) != 0, 0.0, NEG)
        for p in range(N_PAIRS):
            for u in range(2):
                kp = k_ref[0, pl.ds(base + u * tq, tq), p * LANES:(p + 1) * LANES]
                for hh in range(2):
                    h = 2 * p + hh
                    lm = _dot_nt(qs[p][hh], kp) + madd_ref[:, u * tq:(u + 1) * tq]
                    if near:
                        lm = lm + bias_ref[jnp.clip(i - (2 * jj + u), 0, 2), h]
                    logit_ref[h, :, u * tq:(u + 1) * tq] = lm
        for p in range(N_PAIRS):
            vp = v_ref[0, pl.ds(base, tk), p * LANES:(p + 1) * LANES]
            for hh in range(2):
                h = 2 * p + hh
                m_old = m_ref[h]
                m_new = jnp.maximum(m_old, jnp.max(logit_ref[h], axis=1, keepdims=True))
                m_ref[h] = m_new
                pexp = jnp.concatenate(
                    [jnp.exp2(logit_ref[h, :, c * LANES:(c + 1) * LANES] - m_new).astype(BF16)
                     for c in range(tk // LANES)], axis=1)
                vaug = vp * keep_lo + keep_hi if hh == 0 else vp * keep_hi + keep_lo
                acc_ref[h] = jnp.exp2(m_old - m_new) * acc_ref[h] + _dot(pexp, vaug)

    n_pairs = (i + 2) // 2
    n_far = jnp.maximum(n_pairs - 2, 0)

    def far_body(jj, _):
        pair_step(jj, False)
        return 0

    def near_body(jj, _):
        pair_step(jj, True)
        return 0

    lax.fori_loop(0, n_far, far_body, 0)
    lax.fori_loop(n_far, n_pairs, near_body, 0)
    for p in range(N_PAIRS):
        outs = []
        for hh in range(2):
            acc = acc_ref[2 * p + hh]
            outs.append(acc / pltpu.roll(acc, HEAD_DIM, axis=1))
        o_ref[0, :, p * LANES:(p + 1) * LANES] = jnp.where(lo_half, outs[0], outs[1])


def _prompt_dsa(q, k, v, mask, bias, tq):
    b, t, dh = q.shape
    assert t % (2 * tq) == 0
    once = pl.Buffered(1)
    seq = pl.BlockSpec((1, t, dh), lambda bi, i: (bi, 0, 0), pipeline_mode=once)
    return pl.pallas_call(
        functools.partial(_dsa_kernel, tq=tq),
        grid=(b, t // tq),
        in_specs=[pl.BlockSpec((1, tq, dh), lambda bi, i: (bi, i, 0)), seq, seq,
                  pl.BlockSpec((1, tq, t), lambda bi, i: (bi, i, 0)),
                  pl.BlockSpec(bias.shape, lambda bi, i: (0, 0, 0, 0), pipeline_mode=once)],
        out_specs=pl.BlockSpec((1, tq, dh), lambda bi, i: (bi, i, 0)),
        out_shape=SDS((b, t, dh), F32),
        scratch_shapes=[pltpu.VMEM((N_HEADS, tq, 2 * tq), F32), pltpu.VMEM((tq, 2 * tq), F32),
                        pltpu.VMEM((N_HEADS, tq, LANES), F32), pltpu.VMEM((N_HEADS, tq, LANES), F32)],
        compiler_params=_params("arbitrary", "arbitrary"),
        name="prompt_dsa",
    )(q, k, v, mask, bias)


def _sb_kernel(q_ref, k_ref, v_ref, o_ref, z_ref, tail_ref, c_ref, acc_ref, *, tq):
    i = pl.program_id(1)
    qs, lo_half = _pair_queries(q_ref, tq)
    c_ref[...] = jnp.zeros(c_ref.shape, F32)
    acc_ref[...] = jnp.zeros(acc_ref.shape, F32)
    row = lax.broadcasted_iota(I32, (tq, tq), 0)
    col = lax.broadcasted_iota(I32, (tq, tq), 1)
    not_before = jnp.where(row >= col, 1.0, 0.0).astype(BF16)
    earlier = col < row

    def stage_scores(j, slot):
        base = pl.multiple_of(j * tq, tq)
        for p in range(N_PAIRS):
            kp = k_ref[0, pl.ds(base, tq), p * LANES:(p + 1) * LANES]
            for hh in range(2):
                z_ref[slot, 2 * p + hh] = _dot_nt(qs[p][hh], kp)

    def stage_tails(slot, diag):
        for h in range(N_HEADS):
            z = z_ref[slot, h]
            sp = jnp.maximum(z, jnp.log2(1.0 + jnp.exp2(jnp.minimum(z, EXP2_CLAMP))))
            if diag:
                sp = jnp.where(earlier, sp, 0.0)
            tail_ref[slot, h] = _dot(sp.astype(BF16), not_before)

    def stage_outputs(j, slot, diag):
        base = pl.multiple_of(j * tq, tq)
        for p in range(N_PAIRS):
            vp = v_ref[0, pl.ds(base, tq), p * LANES:(p + 1) * LANES]
            pvs = []
            for hh in range(2):
                h = 2 * p + hh
                c_old = c_ref[h]
                chunks = []
                for n in range(tq // LANES):
                    sl = slice(n * LANES, (n + 1) * LANES)
                    a = jnp.exp2(z_ref[slot, h, :, sl] - tail_ref[slot, h, :, sl] - c_old)
                    if diag:
                        a = jnp.where(earlier[:, sl], a, 0.0)
                    chunks.append(a.astype(BF16))
                c_ref[h] = c_old + tail_ref[slot, h, :, :1]
                pvs.append(_dot(jnp.concatenate(chunks, axis=1), vp))
            sl = slice(p * LANES, (p + 1) * LANES)
            acc_ref[:, sl] = acc_ref[:, sl] + jnp.where(lo_half, pvs[0], pvs[1])

    def block(j, diag):
        stage_scores(j, 0)
        stage_tails(0, diag)
        stage_outputs(j, 0, diag)

    block(i, True)

    def pair(step, _):
        j = i - 1 - 2 * step
        stage_scores(j, 0)
        stage_tails(0, False)
        stage_scores(j - 1, 1)
        stage_tails(1, False)
        stage_outputs(j, 0, False)
        stage_outputs(j - 1, 1, False)
        return 0

    lax.fori_loop(0, i // 2, pair, 0)

    @pl.when(i % 2 == 1)
    def _():
        block(0, False)

    o_ref[0] = acc_ref[...]


def _prompt_sb(q, k, v, tq):
    b, t, dh = q.shape
    seq = pl.BlockSpec((1, t, dh), lambda bi, i: (bi, 0, 0), pipeline_mode=pl.Buffered(1))
    return pl.pallas_call(
        functools.partial(_sb_kernel, tq=tq),
        grid=(b, t // tq),
        in_specs=[pl.BlockSpec((1, tq, dh), lambda bi, i: (bi, i, 0)), seq, seq],
        out_specs=pl.BlockSpec((1, tq, dh), lambda bi, i: (bi, i, 0)),
        out_shape=SDS((b, t, dh), F32),
        scratch_shapes=[pltpu.VMEM((2, N_HEADS, tq, tq), F32), pltpu.VMEM((2, N_HEADS, tq, tq), F32),
                        pltpu.VMEM((N_HEADS, tq, LANES), F32),
                        pltpu.VMEM((tq, dh), F32)],
        compiler_params=_params("arbitrary", "arbitrary"),
        name="prompt_sb",
    )(q, k, v)


def _merge_kernel(ya_ref, yb_ref, gate_ref, x_ref, g1_ref, n_ref, wa_ref, wb_ref, wo_ref, o_ref):
    d = x_ref.shape[1]
    ma = _dot(ya_ref[...].astype(BF16), wa_ref[...])
    mb = _dot(yb_ref[...].astype(BF16), wb_ref[...])
    merged = gate_ref[:, :d].astype(F32) * ma + gate_ref[:, d:].astype(F32) * mb
    mix = _dot(merged.astype(BF16), wo_ref[...])
    o_ref[...] = x_ref[...] + g1_ref[0] * _rms(mix, n_ref[...])


def _mod_spec(mod, tm, rows_per_mod):
    d = mod.shape[2]
    if mod.shape[1] == 1:
        return pl.BlockSpec((1, 1, d), lambda i: ((i * tm) // rows_per_mod, 0, 0))
    return pl.BlockSpec((1, tm, d), lambda i: (0, i, 0))


def _merge(ya, yb, gate, x, g1, norm, wa, wb, wo, tm, rows_per_mod):
    m, d = x.shape
    row = lambda a: pl.BlockSpec((tm, a.shape[1]), lambda i: (i, 0))
    full = lambda a: pl.BlockSpec(a.shape, lambda i: (0, 0), pipeline_mode=pl.Buffered(1))
    return pl.pallas_call(
        _merge_kernel,
        grid=(m // tm,),
        in_specs=[row(ya), row(yb), row(gate), row(x), _mod_spec(g1, tm, rows_per_mod),
                  full(norm), full(wa), full(wb), full(wo)],
        out_specs=pl.BlockSpec((tm, d), lambda i: (i, 0)),
        out_shape=SDS((m, d), F32),
        compiler_params=_params("arbitrary"),
        name="merge",
    )(ya, yb, gate, x, g1, norm, wa, wb, wo)


def _ffn_kernel(x_ref, sc_ref, sh_ref, g2_ref, npre_ref, npost_ref, wg_ref, wu_ref, wd_ref, o_ref, *, chunk):
    x = x_ref[...]
    h = (_rms(x, npre_ref[...]) * (1.0 + sc_ref[0]) + sh_ref[0]).astype(BF16)
    f = jnp.zeros(x.shape, F32)
    for c0 in range(0, wg_ref.shape[1], chunk):
        gate = _dot(h, wg_ref[:, c0:c0 + chunk])
        up = _dot(h, wu_ref[:, c0:c0 + chunk])
        act = gate / (1.0 + jnp.exp(-gate)) * up
        f = f + _dot(act.astype(BF16), wd_ref[c0:c0 + chunk, :])
    o_ref[...] = x + g2_ref[0] * _rms(f, npost_ref[...])


def _ffn(x, sc, sh, g2, npre, npost, wg, wu, wd, tm, rows_per_mod):
    m, d = x.shape
    full = lambda a: pl.BlockSpec(a.shape, lambda i: (0, 0), pipeline_mode=pl.Buffered(1))
    mod = lambda a: _mod_spec(a, tm, rows_per_mod)
    return pl.pallas_call(
        functools.partial(_ffn_kernel, chunk=256),
        grid=(m // tm,),
        in_specs=[pl.BlockSpec((tm, d), lambda i: (i, 0)), mod(sc), mod(sh), mod(g2),
                  full(npre), full(npost), full(wg), full(wu), full(wd)],
        out_specs=pl.BlockSpec((tm, d), lambda i: (i, 0)),
        out_shape=SDS((m, d), F32),
        compiler_params=_params("arbitrary"),
        name="ffn",
    )(x, sc, sh, g2, npre, npost, wg, wu, wd)


def _sample_score_kernel(pt_ref, q_ref, w_ref, knew_ref, *rest, pages):
    page_refs = rest[:pages]
    past_ref, new_ref = rest[pages:]
    q_hi, q_lo = _split2(q_ref[0])
    w = w_ref[0]
    t = q_ref.shape[1] // N_IDX_HEADS

    def score(keys):
        k_hi, k_lo = _split2(keys)
        r = jnp.maximum(_dot3(q_hi, q_lo, k_hi, k_lo), 0.0) * w
        s = r[0:t]
        for h in range(1, N_IDX_HEADS):
            s = s + r[h * t:(h + 1) * t]
        return s

    for n in range(pages):
        past_ref[0, :, n * LANES:(n + 1) * LANES] = score(page_refs[n][...])
    new_ref[0] = score(knew_ref[0])


def _sample_scores(page_table, q_rows, w_rows, k_new, cache_k_idx, pages):
    nb, n_pages = page_table.shape
    page = cache_k_idx.shape[2]
    assert page == LANES
    rows = q_rows.shape[1]
    t = rows // N_IDX_HEADS

    def page_spec(n):
        return pl.BlockSpec((None, None, IDX_DIM, page), lambda b, g, pt: (0, pt[b, g * pages + n], 0, 0))

    grid_spec = pltpu.PrefetchScalarGridSpec(
        num_scalar_prefetch=1,
        grid=(nb, n_pages // pages),
        in_specs=[pl.BlockSpec((1, rows, IDX_DIM), lambda b, g, pt: (b, 0, 0)),
                  pl.BlockSpec((1, rows, LANES), lambda b, g, pt: (b, 0, 0)),
                  pl.BlockSpec((1, IDX_DIM, page), lambda b, g, pt: (b, 0, 0))]
                 + [page_spec(n) for n in range(pages)],
        out_specs=[pl.BlockSpec((1, t, pages * page), lambda b, g, pt: (b, 0, g)),
                   pl.BlockSpec((1, t, page), lambda b, g, pt: (b, 0, 0))],
    )
    return pl.pallas_call(
        functools.partial(_sample_score_kernel, pages=pages),
        grid_spec=grid_spec,
        out_shape=[SDS((nb, t, n_pages * page), F32), SDS((nb, t, page), F32)],
        compiler_params=_params("arbitrary", "arbitrary"),
        name="sample_scores",
    )(page_table, q_rows, w_rows, k_new, *([cache_k_idx.transpose(0, 1, 3, 2)] * pages))


def _sample_select_kernel(past_ref, new_ref, mpast_ref, mnew_ref, score_ref, thr_ref, need_ref, *, t, topk):
    rows, past = past_ref.shape
    score_ref[:, :past] = past_ref[...]
    q_t = lax.broadcasted_iota(I32, (rows, LANES), 0) % t
    col = lax.broadcasted_iota(I32, (rows, LANES), 1)
    score_ref[:, past:] = jnp.where(col <= q_t, new_ref[...], -jnp.inf)

    def emit(j, sel):
        mpast_ref[:, pl.ds(pl.multiple_of(j * LANES, LANES), LANES)] = sel

    def emit_new(j, sel):
        mnew_ref[...] = sel

    width = score_ref.shape[1]
    gw = _group_width(width)
    _emit_selection(score_ref, thr_ref, need_ref, width // gw, gw, width // LANES, LANES, topk, emit, emit_new)


def _sample_select(s_past, s_new, t, topk, tr):
    rows, past = s_past.shape
    width = past + LANES
    return pl.pallas_call(
        functools.partial(_sample_select_kernel, t=t, topk=float(topk)),
        grid=(rows // tr,),
        in_specs=[pl.BlockSpec((tr, past), lambda i: (i, 0)), pl.BlockSpec((tr, LANES), lambda i: (i, 0))],
        out_specs=[pl.BlockSpec((tr, past), lambda i: (i, 0)), pl.BlockSpec((tr, LANES), lambda i: (i, 0))],
        out_shape=[SDS((rows, past), F32), SDS((rows, LANES), F32)],
        scratch_shapes=[pltpu.VMEM((tr, width), F32), pltpu.VMEM((tr, 1), F32), pltpu.VMEM((tr, 1), F32)],
        compiler_params=_params("arbitrary"),
        name="sample_select",
    )(s_past, s_new)


def _sample_bias_kernel(rb_ref, o_ref, *, t, page):
    rows = N_HEADS * t
    row = lax.broadcasted_iota(I32, (rows, page), 0)
    key = lax.broadcasted_iota(I32, (rows, page), 1)
    q_t, head = row % t, row // t
    for tile in range(3):
        bucket = _t5_bucket(q_t - key + tile * page)
        val = jnp.zeros((rows, page), F32)
        for h in range(N_HEADS):
            val = jnp.where(head == h, _bias_lookup(rb_ref, bucket, h), val)
        o_ref[tile] = val


def _sample_bias(rel_bias, t, page):
    assert page >= MAX_DISTANCE
    return pl.pallas_call(
        functools.partial(_sample_bias_kernel, t=t, page=page),
        in_specs=[pl.BlockSpec(memory_space=pltpu.SMEM)],
        out_shape=SDS((3, N_HEADS * t, page), F32),
        name="sample_bias",
    )(rel_bias)


def _sample_attn_kernel(pt_ref, qa_ref, qb_ref, knew_a_ref, vnew_a_ref, knew_b_ref, vnew_b_ref,
                        fnew_ref, flags_ref, bias_ref, *rest, pages, t):
    ka_refs = rest[0 * pages:1 * pages]
    va_refs = rest[1 * pages:2 * pages]
    kb_refs = rest[2 * pages:3 * pages]
    vb_refs = rest[3 * pages:4 * pages]
    ya_ref, yb_ref, m_ref, l_ref, acca_ref, c_ref, accb_ref = rest[4 * pages:]
    g = pl.program_id(1)
    n_groups = pl.num_programs(1)
    rows = N_HEADS * t
    qa = qa_ref[0]
    qb = qb_ref[0]
    srow = lax.broadcasted_iota(I32, (LANES, LANES), 0)
    scol = lax.broadcasted_iota(I32, (LANES, LANES), 1)
    not_before = jnp.where(srow >= scol, 1.0, 0.0).astype(BF16)
    not_before = jnp.concatenate([not_before, not_before], axis=0)

    def scores(q, k):
        return _dot(q, k.reshape(N_HEADS * HEAD_DIM, LANES).astype(BF16))

    def weighted(p, v):
        return _dot_nt(p.astype(BF16), v.reshape(N_HEADS * HEAD_DIM, LANES).astype(BF16))

    def dsa_update(blocks):
        logits = [scores(qa, k) + bias + (1.0 - jnp.concatenate([flags] * N_HEADS, axis=0)) * NEG
                  for k, _, bias, flags in blocks]
        m_old = m_ref[...]
        m_new = m_old
        for lg in logits:
            m_new = jnp.maximum(m_new, jnp.max(lg, axis=1, keepdims=True))
        alpha = jnp.exp(m_old - m_new)
        l_new = alpha * l_ref[...]
        contrib = jnp.zeros(acca_ref.shape, F32)
        for lg, (_, v, _, _) in zip(logits, blocks):
            pexp = jnp.exp(lg - m_new)
            l_new = l_new + jnp.sum(pexp, axis=1, keepdims=True)
            contrib = contrib + weighted(pexp, v)
        m_ref[...] = m_new
        l_ref[...] = l_new
        acca_ref[...] = alpha * acca_ref[...] + contrib

    def sb_update(blocks):
        zs = [scores(qb, k) for k, _, _ in blocks]
        tails = []
        for z, (_, _, earlier) in zip(zs, blocks):
            sp = jnp.maximum(z, 0.0) + jnp.log(1.0 + jnp.exp(-jnp.abs(z)))
            if earlier is not None:
                sp = jnp.where(earlier, sp, 0.0)
            hi, lo = _split2(sp)
            tails.append(_dot(jnp.concatenate([hi, lo], axis=1), not_before))
        carry = c_ref[...]
        contrib = jnp.zeros(accb_ref.shape, F32)
        for z, tail, (_, v, earlier) in zip(zs, tails, blocks):
            a = jnp.exp(z - tail - carry)
            if earlier is not None:
                a = jnp.where(earlier, a, 0.0)
            carry = carry + tail[:, :1]
            contrib = contrib + weighted(a, v)
        c_ref[...] = carry
        accb_ref[...] = accb_ref[...] + contrib

    @pl.when(g == 0)
    def _():
        m_ref[...] = jnp.full(m_ref.shape, NEG, F32)
        l_ref[...] = jnp.zeros(l_ref.shape, F32)
        acca_ref[...] = jnp.zeros(acca_ref.shape, F32)
        c_ref[...] = jnp.zeros(c_ref.shape, F32)
        accb_ref[...] = jnp.zeros(accb_ref.shape, F32)
        key = lax.broadcasted_iota(I32, (rows, LANES), 1)
        q_t = lax.broadcasted_iota(I32, (rows, LANES), 0) % t
        dsa_update([(knew_a_ref[0], vnew_a_ref[0], bias_ref[0], fnew_ref[0])])
        sb_update([(knew_b_ref[0], vnew_b_ref[0], key < q_t)])

    blocks = []
    for n in range(pages):
        flags = flags_ref[0, :, (pages - 1 - n) * LANES:(pages - n) * LANES]
        bias = bias_ref[2] if n else jnp.where(g == 0, bias_ref[1], bias_ref[2])
        blocks.append((ka_refs[n][...], va_refs[n][...], bias, flags))
    dsa_update(blocks)
    sb_update([(kb_refs[n][...], vb_refs[n][...], None) for n in range(pages)])

    @pl.when(g == n_groups - 1)
    def _():
        ya_ref[0] = acca_ref[...] / l_ref[...]
        yb_ref[0] = accb_ref[...]


def _sample_attn(page_table, qa_rows, qb_rows, knew_a, vnew_a, knew_b, vnew_b, flags_new, flags_past, bias,
                 cache_k_a, cache_v_a, cache_k_b, cache_v_b, pages, t):
    nb, n_pages = page_table.shape
    rows = N_HEADS * t
    n_groups = n_pages // pages
    dh = N_HEADS * HEAD_DIM
    page_shape = (N_HEADS, HEAD_DIM, LANES)

    def page_spec(n):
        return pl.BlockSpec((None, None) + page_shape,
                            lambda b, g, pt: (0, pt[b, n_pages - 1 - (g * pages + n)], 0, 0, 0))

    per_seq = lambda shape: pl.BlockSpec((1,) + shape, lambda b, g, pt: (b,) + (0,) * len(shape))
    grid_spec = pltpu.PrefetchScalarGridSpec(
        num_scalar_prefetch=1,
        grid=(nb, n_groups),
        in_specs=[per_seq((rows, dh)), per_seq((rows, dh)),
                  per_seq(page_shape), per_seq(page_shape), per_seq(page_shape), per_seq(page_shape),
                  per_seq((t, LANES)),
                  pl.BlockSpec((1, t, pages * LANES), lambda b, g, pt: (b, 0, n_groups - 1 - g)),
                  pl.BlockSpec(bias.shape, lambda b, g, pt: (0, 0, 0))]
                 + [page_spec(n) for n in range(pages)] * 4,
        out_specs=[per_seq((rows, dh)), per_seq((rows, dh))],
        scratch_shapes=[pltpu.VMEM((rows, 1), F32), pltpu.VMEM((rows, 1), F32), pltpu.VMEM((rows, dh), F32),
                        pltpu.VMEM((rows, 1), F32), pltpu.VMEM((rows, dh), F32)],
    )
    key_minor = lambda c: c.transpose(0, 1, 3, 4, 2)
    return pl.pallas_call(
        functools.partial(_sample_attn_kernel, pages=pages, t=t),
        grid_spec=grid_spec,
        out_shape=[SDS((nb, rows, dh), F32), SDS((nb, rows, dh), F32)],
        compiler_params=_params("arbitrary", "arbitrary"),
        name="sample_attn",
    )(page_table, qa_rows, qb_rows, knew_a, vnew_a, knew_b, vnew_b, flags_new, flags_past, bias,
      *([key_minor(cache_k_a)] * pages), *([key_minor(cache_v_a)] * pages),
      *([key_minor(cache_k_b)] * pages), *([key_minor(cache_v_b)] * pages))


def _largest_divisor(n, cap):
    d = min(n, cap)
    while n % d:
        d -= 1
    return d


def kernel(x_prompt, x_sample, c_prompt, c_sample, cache_k_a, cache_v_a, cache_k_idx, cache_k_b, cache_v_b,
           page_table, rel_bias, w_ada, b_ada, norm_pre_mix, norm_post_mix, norm_pre_ffn, norm_post_ffn,
           w_in, w_br_a, w_br_b, w_out, w_ffn_gate, w_ffn_up, w_ffn_down):
    depth = w_in.shape[0]
    assert depth == 1
    nb_p, t_p, d = x_prompt.shape
    nb_s, t_s, _ = x_sample.shape
    n_pages = page_table.shape[1]
    page = cache_k_a.shape[2]
    past = n_pages * page
    dh = N_HEADS * HEAD_DIM
    nq = N_IDX_HEADS * IDX_DIM
    topk_p = min(INDEX_TOPK, t_p // 4)
    topk_s = min(INDEX_TOPK, (past + t_s) // 4)
    tq = 256
    assert t_p % tq == 0 and topk_p <= tq and t_s * N_HEADS <= LANES and page == LANES

    w = w_in[0]
    o = np.cumsum([0, dh, dh, dh, nq, IDX_DIM, N_IDX_HEADS, dh, dh, dh, d, d])
    col = lambda n: w[:, o[n]:o[n + 1]]
    w_main = jnp.concatenate([col(0), col(1), col(2), col(6), col(7), col(8)], axis=1).astype(BF16)
    w_gate = jnp.concatenate([col(9), col(10)], axis=1).astype(BF16)
    w_idx = jnp.concatenate([col(3), col(4), col(4), col(5),
                             jnp.zeros((d, LANES - N_IDX_HEADS), F32)], axis=1)
    w_idx_hi = w_idx.astype(BF16)
    w_idx_lo = (w_idx - w_idx_hi.astype(F32)).astype(BF16)
    wa, wb, wo = w_br_a[0].astype(BF16), w_br_b[0].astype(BF16), w_out[0].astype(BF16)
    wg, wu, wd = w_ffn_gate[0].astype(BF16), w_ffn_up[0].astype(BF16), w_ffn_down[0].astype(BF16)
    n_pre_mix, n_post_mix = norm_pre_mix[0][None], norm_post_mix[0][None]
    n_pre_ffn, n_post_ffn = norm_pre_ffn[0][None], norm_post_ffn[0][None]

    n_c = nb_p + nb_s
    c_all = jnp.concatenate([c_prompt, c_sample, jnp.zeros((-n_c % 8, d), F32)], axis=0)
    ada = _ada(c_all, w_ada[0], b_ada[0][None])
    mods_p = [m[:, None, :] for m in jnp.split(ada[:nb_p], 6, axis=-1)]
    mods_s = [jnp.repeat(m, t_s, axis=0)[None] for m in jnp.split(ada[nb_p:n_c], 6, axis=-1)]

    def dense_front(x2, mods, tm, rows_per_mod, seq_len, q_scale):
        return _inproj(x2, mods[1], mods[0], n_pre_mix, w_main, w_gate, w_idx_hi, w_idx_lo, tm, rows_per_mod,
                       seq_len, q_scale)

    def dense_back(ya, yb, gate, x2, mods, tm, rows_per_mod):
        x1 = _merge(ya, yb, gate, x2, mods[2], n_post_mix, wa, wb, wo, tm, rows_per_mod)
        return _ffn(x1, mods[4], mods[3], mods[5], n_pre_ffn, n_post_ffn, wg, wu, wd, tm, rows_per_mod)

    m_p = nb_p * t_p
    xp2 = x_prompt.reshape(m_p, d)
    tm_p = 512
    (qa, ka, va, qb, kb, vb, kab, vab, kbb, vbb, gate, iq, ik, wi, kidx, wit) = dense_front(
        xp2, mods_p, tm_p, t_p, t_p, LOG2E * HEAD_DIM ** -0.5)
    seq3 = lambda a: a.reshape(nb_p, t_p, a.shape[-1])
    mask = _prompt_index(seq3(iq), wit, seq3(ik), tq, topk_p)
    ya = _prompt_dsa(seq3(qa), seq3(kab), seq3(vab), mask, _bias_tiles(rel_bias, tq), tq)
    yb = _prompt_sb(seq3(qb), seq3(kbb), seq3(vbb), tq)
    y_p = dense_back(ya.reshape(m_p, dh), yb.reshape(m_p, dh), gate, xp2, mods_p, tm_p, t_p)
    heads_p = lambda a: a.reshape(nb_p, N_HEADS, HEAD_DIM, t_p).transpose(0, 3, 1, 2)[None]
    outs_p = (heads_p(ka), heads_p(va), kidx.transpose(0, 2, 1)[None], heads_p(kb), heads_p(vb))

    m_s = nb_s * t_s
    xs2 = x_sample.reshape(m_s, d)
    tm_s = _largest_divisor(m_s, 256)
    (qa, ka, va, qb, kb, vb, _, _, _, _, gate, iq, ik, wi, kidx, _) = dense_front(
        xs2, mods_s, tm_s, t_s, m_s, HEAD_DIM ** -0.5)
    q_rows = iq.reshape(nb_s, t_s, N_IDX_HEADS, IDX_DIM).transpose(0, 2, 1, 3).reshape(nb_s, N_IDX_HEADS * t_s, IDX_DIM)
    w_rows = wi.reshape(nb_s, t_s, LANES)[:, :, :N_IDX_HEADS].transpose(0, 2, 1).reshape(nb_s, N_IDX_HEADS * t_s, 1)
    w_rows = jnp.broadcast_to(w_rows, (nb_s, N_IDX_HEADS * t_s, LANES))
    pages_idx = _largest_divisor(n_pages, 16)
    k_new = jnp.pad(kidx[0].reshape(IDX_DIM, nb_s, t_s).transpose(1, 0, 2), ((0, 0), (0, 0), (0, page - t_s)))
    s_past, s_new = _sample_scores(page_table, q_rows, w_rows, k_new, cache_k_idx, pages_idx)
    sel_past, sel_new = _sample_select(s_past.reshape(m_s, past), s_new.reshape(m_s, page), t_s, topk_s,
                                       _largest_divisor(m_s, 128))
    head_rows = lambda a: (jnp.einsum('bthd,hg->bhtgd', a.reshape(nb_s, t_s, N_HEADS, HEAD_DIM).astype(F32),
                                      jnp.eye(N_HEADS, dtype=F32))
                           .reshape(nb_s, N_HEADS * t_s, dh).astype(BF16))
    as_page = lambda a: jnp.pad(a[0].reshape(N_HEADS, HEAD_DIM, nb_s, t_s).transpose(2, 0, 1, 3),
                                ((0, 0), (0, 0), (0, 0), (0, page - t_s)))
    pages_attn = _largest_divisor(n_pages, 16)
    ya, yb = _sample_attn(page_table, head_rows(qa), head_rows(qb),
                          as_page(ka), as_page(va), as_page(kb), as_page(vb),
                          sel_new.reshape(nb_s, t_s, page), sel_past.reshape(nb_s, t_s, past),
                          _sample_bias(rel_bias, t_s, page),
                          cache_k_a, cache_v_a, cache_k_b, cache_v_b, pages_attn, t_s)
    token_rows = lambda y: (jnp.diagonal(y.reshape(nb_s, N_HEADS, t_s, N_HEADS, HEAD_DIM), axis1=1, axis2=3)
                            .transpose(0, 1, 3, 2).reshape(m_s, dh))
    y_s = dense_back(token_rows(ya), token_rows(yb), gate, xs2, mods_s, tm_s, t_s)
    heads_s = lambda a: a[0].T.reshape(depth, nb_s, t_s, N_HEADS, HEAD_DIM)
    outs_s = (heads_s(ka), heads_s(va), kidx[0].T.reshape(depth, nb_s, t_s, IDX_DIM), heads_s(kb), heads_s(vb))

    return (y_p.reshape(nb_p, t_p, d), y_s.reshape(nb_s, t_s, d)) + outs_p + outs_s
```
